```python
import math
import jax
import jax.numpy as jnp
from jax import lax
import numpy as np

D_MODEL = 1024
BATCH = 2
SEQ = 8192
DEPTH = 2

GRID_W = 64
CTX_LEN = 256
N_DIR = 2
SHORT_CONV = 3

D_MIX = D_MODEL
D_MLSTM = D_MIX // 4
MLSTM_HEAD_DIM = 64
MLSTM_HEADS = D_MLSTM // MLSTM_HEAD_DIM
MLSTM_GATES = N_DIR * 2 * MLSTM_HEADS
MLSTM_CHUNK = 128

D_ATTN = D_MIX // 2
ATTN_HEAD_DIM = 64
ATTN_HEADS = D_ATTN // ATTN_HEAD_DIM
ATTN_KV_HEADS = 2
ATTN_GROUP = ATTN_HEADS // ATTN_KV_HEADS
D_KV = ATTN_KV_HEADS * ATTN_HEAD_DIM
ATTN_BLOCK = 128
ROPE_THETA = 10000.0

D_HYENA = D_MIX - D_MLSTM - D_ATTN
HYENA_ORDER = 2
HYENA_BANDS = 16
HYENA_EMB = 1 + 2 * HYENA_BANDS
HYENA_FILTER_HIDDEN = 64
HYENA_FAST_DECAY = 0.3
HYENA_SLOW_DECAY = 1.5
HYENA_DECAY_TARGET = 1e-2
HYENA_WINDOW_SHIFT = 0.05

N_IN = 4 * D_MLSTM + MLSTM_GATES + D_ATTN + 2 * D_KV + 3 * D_HYENA

N_EXPERTS = 16
EC_CAPACITY_FACTOR = 2
D_FF_EXPERT = 2816

N_MOD = 6
LN_EPS = 1e-5
RMS_EPS = 1e-6

kernel_name = "hybrid_mlstm_gqa_hyena_ecmoe_dit"


def _layer_norm(x, w=None, b=None):
    xf = x.astype(jnp.float32)
    mu = jnp.mean(xf, axis=-1, keepdims=True)
    var = jnp.mean(jnp.square(xf - mu), axis=-1, keepdims=True)
    y = (xf - mu) * lax.rsqrt(var + LN_EPS)
    if w is not None:
        y = y * w.astype(jnp.float32) + b.astype(jnp.float32)
    return y.astype(x.dtype)


def _rms_norm(x, w):
    xf = x.astype(jnp.float32)
    y = xf * lax.rsqrt(jnp.mean(xf * xf, axis=-1, keepdims=True) + RMS_EPS)
    return (y * w.astype(jnp.float32)).astype(x.dtype)


def _modulate(x, shift, scale):
    return _layer_norm(x) * (1 + scale) + shift


def _centred_dwconv(x, w, b):
    ch = x.shape[-1]
    pad = SHORT_CONV // 2
    y = lax.conv_general_dilated(x, w.astype(x.dtype)[:, None, :], window_strides=(1,),
                                 padding=[(pad, pad)], dimension_numbers=("NWC", "WIO", "NWC"),
                                 feature_group_count=ch)
    return y + b.astype(x.dtype)


def _split_projection(p):
    sizes = (2 * D_MLSTM, D_MLSTM, D_MLSTM, MLSTM_GATES, D_ATTN, D_KV, D_KV, 3 * D_HYENA)
    idx, acc = [], 0
    for s in sizes[:-1]:
        acc += s
        idx.append(acc)
    return jnp.split(p, idx, axis=-1)


def _mlstm_chunk_scan(q, k, v, li, lf, state):
    n_dir, bsz, nh, L, dh = q.shape
    nc = L // MLSTM_CHUNK

    def chunks(a):
        a = a.reshape(a.shape[:3] + (nc, MLSTM_CHUNK) + a.shape[4:])
        return jnp.moveaxis(a, 3, 0)

    lower = jnp.tril(jnp.ones((MLSTM_CHUNK, MLSTM_CHUNK), dtype=bool))

    def step(carry, inp):
        c0, n0, m0 = carry
        qc, kc, vc, ic, fc = inp
        bcum = jnp.cumsum(fc, axis=-1)
        log_inter = bcum + m0[..., None]
        log_intra = jnp.where(lower, bcum[..., :, None] - bcum[..., None, :] + ic[..., None, :], -jnp.inf)
        m = jnp.maximum(log_inter, jnp.max(log_intra, axis=-1))
        w_inter = jnp.exp(log_inter - m)
        scores = jnp.einsum("nbhjd,nbhsd->nbhjs", qc, kc) * jnp.exp(log_intra - m[..., None])
        num = (w_inter[..., None] * jnp.einsum("nbhed,nbhjd->nbhje", c0, qc)
               + jnp.einsum("nbhjs,nbhse->nbhje", scores, vc))
        den = w_inter * jnp.einsum("nbhd,nbhjd->nbhj", n0, qc) + jnp.sum(scores, axis=-1)
        hc = num / jnp.maximum(jnp.abs(den), jnp.exp(-m))[..., None]
        b_last = bcum[..., -1:]
        log_src = b_last - bcum + ic
        m_new = jnp.maximum(b_last[..., 0] + m0, jnp.max(log_src, axis=-1))
        w_src = jnp.exp(log_src - m_new[..., None])
        decay = jnp.exp(b_last[..., 0] + m0 - m_new)
        c_new = decay[..., None, None] * c0 + jnp.einsum("nbhs,nbhse,nbhsd->nbhed", w_src, vc, kc)
        n_new = decay[..., None] * n0 + jnp.einsum("nbhs,nbhsd->nbhd", w_src, kc)
        return (c_new, n_new, m_new), hc

    state, hs = lax.scan(step, state, tuple(chunks(a) for a in (q, k, v, li, lf)))
    hs = jnp.moveaxis(hs, 0, 3).reshape(n_dir, bsz, nh, L, dh)
    return hs, state


def _mlstm_zero_state(bsz):
    f32 = jnp.float32
    return (jnp.zeros((N_DIR, bsz, MLSTM_HEADS, MLSTM_HEAD_DIM, MLSTM_HEAD_DIM), f32),
            jnp.zeros((N_DIR, bsz, MLSTM_HEADS, MLSTM_HEAD_DIM), f32),
            jnp.zeros((N_DIR, bsz, MLSTM_HEADS), f32))


def _mlstm_stream(q, k, v, gates, gate_b, state):
    bsz, n, _ = q.shape

    def heads(a):
        return a.reshape(bsz, n, MLSTM_HEADS, MLSTM_HEAD_DIM).transpose(0, 2, 1, 3).astype(jnp.float32)

    def both_dirs(a):
        return jnp.stack([a, jnp.flip(a, axis=2)])

    g = gates.astype(jnp.float32).reshape(bsz, n, N_DIR, 2, MLSTM_HEADS) + gate_b.astype(jnp.float32)
    g = jnp.transpose(g, (2, 3, 0, 4, 1))
    li = jnp.stack([g[0, 0], jnp.flip(g[1, 0], axis=-1)])
    lf = jax.nn.log_sigmoid(jnp.stack([g[0, 1], jnp.flip(g[1, 1], axis=-1)]))
    h, state = _mlstm_chunk_scan(both_dirs(heads(q) * MLSTM_HEAD_DIM ** -0.5), both_dirs(heads(k)),
                                 both_dirs(heads(v)), li, lf, state)
    return h[0] + jnp.flip(h[1], axis=2), state


def _mlstm_output(h, o, norm_w):
    bsz, _, n, _ = h.shape
    h = _rms_norm(jnp.transpose(h, (0, 2, 1, 3)), norm_w.reshape(MLSTM_HEADS, MLSTM_HEAD_DIM))
    return (jax.nn.sigmoid(o.astype(jnp.float32)) * h.reshape(bsz, n, D_MLSTM)).astype(o.dtype)


def _axial_rope_tables(n_lat):
    rows = n_lat // GRID_W
    row = jnp.repeat(jnp.arange(rows, dtype=jnp.float32), GRID_W)
    col = (jnp.arange(n_lat) % GRID_W).astype(jnp.float32)
    nf = ATTN_HEAD_DIM // 4
    inv = ROPE_THETA ** (-jnp.arange(nf, dtype=jnp.float32) / nf)
    ar = row[:, None] * inv
    ac = col[:, None] * inv
    ang = jnp.concatenate([ar, ar, ac, ac], axis=-1)
    return jnp.cos(ang), jnp.sin(ang)


def _apply_rope(x, cos, sin):
    nf = ATTN_HEAD_DIM // 4
    xf = x.astype(jnp.float32)
    xr = xf.reshape(xf.shape[:-1] + (2, 2, nf))
    rot = jnp.stack([-xr[..., 1, :], xr[..., 0, :]], axis=-2).reshape(xf.shape)
    return (xf * cos + rot * sin).astype(x.dtype)


def _attend(q, k, v):
    s = jnp.einsum("bkgqd,bksd->bkgqs", q, k, preferred_element_type=jnp.float32)
    p = jax.nn.softmax(s, axis=-1).astype(v.dtype)
    return jnp.einsum("bkgqs,bksd->bkgqd", p, v)


def _gqa_mixer(q_lat, k_lat, v_lat, q_ctx, k_ctx, v_ctx, q_norm_w, k_norm_w, with_ctx_out):
    bsz, n_lat, _ = q_lat.shape
    n_ctx = q_ctx.shape[1]
    scale = ATTN_HEAD_DIM ** -0.5

    def q_heads(q):
        n = q.shape[1]
        q = q.reshape(bsz, n, ATTN_KV_HEADS, ATTN_GROUP, ATTN_HEAD_DIM).transpose(0, 2, 3, 1, 4)
        return _rms_norm(q, q_norm_w) * scale

    def kv_heads(t):
        n = t.shape[1]
        return t.reshape(bsz, n, ATTN_KV_HEADS, ATTN_HEAD_DIM).transpose(0, 2, 1, 3)

    cos, sin = _axial_rope_tables(n_lat)
    ql = _apply_rope(q_heads(q_lat), cos, sin)
    kl = _apply_rope(_rms_norm(kv_heads(k_lat), k_norm_w), cos, sin)
    kc = _rms_norm(kv_heads(k_ctx), k_norm_w)
    vc = kv_heads(v_ctx)
    k_all = jnp.concatenate([kl, kc], axis=2)
    v_all = jnp.concatenate([kv_heads(v_lat), vc], axis=2)
    nb = n_lat // ATTN_BLOCK
    qb = jnp.moveaxis(ql.reshape(bsz, ATTN_KV_HEADS, ATTN_GROUP, nb, ATTN_BLOCK, ATTN_HEAD_DIM), 3, 0)
    o = lax.map(lambda qi: _attend(qi, k_all, v_all), qb)
    y_lat = jnp.transpose(o, (1, 0, 4, 2, 3, 5)).reshape(bsz, n_lat, D_ATTN)
    if not with_ctx_out:
        return y_lat, None
    oc = _attend(q_heads(q_ctx), kc, vc)
    return y_lat, oc.transpose(0, 3, 1, 2, 4).reshape(bsz, n_ctx, D_ATTN)


def _hyena_filters(n, f_w1, f_b1, f_freq, f_w2, f_b2, f_w3):
    f32 = jnp.float32
    t = jnp.arange(n, dtype=f32) / n
    bands = jnp.arange(1, HYENA_BANDS + 1, dtype=f32)
    ang = (2.0 * math.pi) * t[:, None] * bands
    feats = jnp.concatenate([t[:, None], jnp.cos(ang), jnp.sin(ang)], axis=-1)
    hid = jnp.sin(f_freq[0].astype(f32) * (feats @ f_w1.astype(f32) + f_b1.astype(f32)))
    hid = jnp.sin(f_freq[1].astype(f32) * (hid @ f_w2.astype(f32) + f_b2.astype(f32)))
    filt = (hid @ f_w3.astype(f32)).reshape(n, N_DIR, HYENA_ORDER, D_HYENA)
    log_target = abs(math.log(HYENA_DECAY_TARGET))
    deltas = jnp.linspace(log_target / HYENA_SLOW_DECAY, log_target / HYENA_FAST_DECAY, D_HYENA, dtype=f32)
    window = jnp.exp(-t[:, None] * deltas) + HYENA_WINDOW_SHIFT
    filt = filt * window[:, None, None, :]
    fwd = filt[:, 0]
    bwd = filt[1:, 1]
    l1 = jnp.sum(jnp.abs(fwd), axis=0) + jnp.sum(jnp.abs(bwd), axis=0)
    taps = jnp.concatenate([fwd, jnp.zeros((1, HYENA_ORDER, D_HYENA), f32), bwd[::-1]], axis=0)
    return taps / l1


def _fft_long_conv(z, taps):
    n = z.shape[1]
    zf = jnp.fft.rfft(z, n=2 * n, axis=1)
    tf = jnp.fft.rfft(taps, axis=0)
    return jnp.fft.irfft(zf * tf[None], n=2 * n, axis=1)[:, :n]


def _hyena(u, filter_params, skip):
    n = u.shape[1]
    taps = _hyena_filters(n, *filter_params)
    v, x1, x2 = jnp.split(u.astype(jnp.float32), 3, axis=-1)
    z = v
    for o, gate in enumerate((x1, x2)):
        z = gate * (_fft_long_conv(z, taps[:, o]) + skip[o].astype(jnp.float32) * z)
    return z.astype(u.dtype)


def _mixing_sublayer(h_lat, h_ctx, w_in, mlstm_conv_w, mlstm_conv_b, mlstm_gate_b, mlstm_norm_w,
                     attn_q_norm_w, attn_k_norm_w, hyena_conv_w, hyena_conv_b, hyena_filter,
                     hyena_skip, w_out, with_ctx_out):
    bsz = h_lat.shape[0]
    mqk_l, mv_l, mo_l, mg_l, aq_l, ak_l, av_l, hy_l = _split_projection(h_lat @ w_in)
    mqk_c, mv_c, mo_c, mg_c, aq_c, ak_c, av_c, hy_c = _split_projection(h_ctx @ w_in)
    q_c, k_c = jnp.split(jax.nn.silu(_centred_dwconv(mqk_c, mlstm_conv_w, mlstm_conv_b)), 2, axis=-1)
    q_l, k_l = jnp.split(jax.nn.silu(_centred_dwconv(mqk_l, mlstm_conv_w, mlstm_conv_b)), 2, axis=-1)
    hm_c, ctx_state = _mlstm_stream(q_c, k_c, mv_c, mg_c, mlstm_gate_b, _mlstm_zero_state(bsz))
    hm_l, _ = _mlstm_stream(q_l, k_l, mv_l, mg_l, mlstm_gate_b, ctx_state)
    ym_l = _mlstm_output(hm_l, mo_l, mlstm_norm_w)
    ya_l, ya_c = _gqa_mixer(aq_l, ak_l, av_l, aq_c, ak_c, av_c, attn_q_norm_w, attn_k_norm_w, with_ctx_out)
    yh_l = _hyena(_centred_dwconv(hy_l, hyena_conv_w, hyena_conv_b), hyena_filter, hyena_skip)
    y_lat = jnp.concatenate([ym_l, ya_l, yh_l], axis=-1) @ w_out
    if not with_ctx_out:
        return y_lat, None
    ym_c = _mlstm_output(hm_c, mo_c, mlstm_norm_w)
    yh_c = _hyena(_centred_dwconv(hy_c, hyena_conv_w, hyena_conv_b), hyena_filter, hyena_skip)
    y_ctx = jnp.concatenate([ym_c, ya_c, yh_c], axis=-1) @ w_out
    return y_lat, y_ctx


def _expert_choice_ffn(h, router_w, router_b, w_gate, w_up, w_down):
    bsz, n, d = h.shape
    cap = EC_CAPACITY_FACTOR * n // N_EXPERTS
    logits = (h @ router_w).astype(jnp.float32) + router_b.astype(jnp.float32)
    aff = jax.nn.softmax(logits, axis=-1)
    top_val, top_idx = lax.top_k(jnp.swapaxes(aff, 1, 2), cap)
    xs = jax.vmap(lambda hb, ib: hb[ib])(h, top_idx)
    g = jnp.einsum("becd,edf->becf", xs, w_gate)
    u = jnp.einsum("becd,edf->becf", xs, w_up)
    y = jnp.einsum("becf,efd->becd", jax.nn.silu(g) * u, w_down) * top_val[..., None].astype(h.dtype)
    return jax.vmap(lambda yb, ib: jnp.zeros((n, d), yb.dtype).at[ib.reshape(-1)].add(yb.reshape(-1, d)))(y, top_idx)


def setup_inputs(seed: int = 0) -> dict:
    key = jax.random.key(seed)
    ks = iter(jax.random.split(key, 40))
    f32 = jnp.float32

    def nrm(shape, scale):
        return scale * jax.random.normal(next(ks), shape, f32)

    beta = (8.0 * DEPTH) ** -0.25
    fg_base = jnp.linspace(3.0, 6.0, MLSTM_HEADS, dtype=f32)
    mlstm_gate_b = jnp.concatenate([nrm((DEPTH, N_DIR, 1, MLSTM_HEADS), 0.1),
                                    fg_base + nrm((DEPTH, N_DIR, 1, MLSTM_HEADS), 0.1)], axis=2)
    return {
        "x": nrm((BATCH, SEQ, D_MODEL), 1.0),
        "c": nrm((BATCH, D_MODEL), 1.0),
        "ctx": nrm((BATCH, CTX_LEN, D_MODEL), 1.0),
        "c_ctx": nrm((D_MODEL,), 1.0),
        "w_mod": nrm((DEPTH, D_MODEL, N_MOD * D_MODEL), 0.5 * D_MODEL ** -0.5),
        "b_mod": nrm((DEPTH, N_MOD * D_MODEL), 0.02),
        "w_in": nrm((DEPTH, D_MODEL, N_IN), D_MODEL ** -0.5),
        "mlstm_conv_w": nrm((DEPTH, SHORT_CONV, 2 * D_MLSTM), SHORT_CONV ** -0.5),
        "mlstm_conv_b": nrm((DEPTH, 2 * D_MLSTM), 0.02),
        "mlstm_gate_b": mlstm_gate_b,
        "mlstm_norm_w": 1.0 + nrm((DEPTH, D_MLSTM), 0.02),
        "attn_q_norm_w": 1.0 + nrm((DEPTH, ATTN_HEAD_DIM), 0.02),
        "attn_k_norm_w": 1.0 + nrm((DEPTH, ATTN_HEAD_DIM), 0.02),
        "hyena_conv_w": nrm((DEPTH, SHORT_CONV, 3 * D_HYENA), SHORT_CONV ** -0.5),
        "hyena_conv_b": nrm((DEPTH, 3 * D_HYENA), 0.02),
        "hyena_f_w1": nrm((DEPTH, HYENA_EMB, HYENA_FILTER_HIDDEN), HYENA_EMB ** -0.5),
        "hyena_f_b1": nrm((DEPTH, HYENA_FILTER_HIDDEN), 0.5),
        "hyena_f_freq": 1.0 + nrm((DEPTH, 2, HYENA_FILTER_HIDDEN), 0.1),
        "hyena_f_w2": nrm((DEPTH, HYENA_FILTER_HIDDEN, HYENA_FILTER_HIDDEN), HYENA_FILTER_HIDDEN ** -0.5),
        "hyena_f_b2": nrm((DEPTH, HYENA_FILTER_HIDDEN), 0.5),
        "hyena_f_w3": nrm((DEPTH, HYENA_FILTER_HIDDEN, N_DIR * HYENA_ORDER * D_HYENA), HYENA_FILTER_HIDDEN ** -0.5),
        "hyena_skip": nrm((DEPTH, HYENA_ORDER, D_HYENA), 0.5),
        "w_out": nrm((DEPTH, D_MIX, D_MODEL), beta * D_MIX ** -0.5),
        "ln_mix_w": 1.0 + nrm((DEPTH, D_MODEL), 0.02),
        "ln_mix_b": nrm((DEPTH, D_MODEL), 0.02),
        "router_w": nrm((DEPTH, D_MODEL, N_EXPERTS), D_MODEL ** -0.5),
        "router_b": nrm((DEPTH, N_EXPERTS), 0.01),
        "exp_w_gate": nrm((DEPTH, N_EXPERTS, D_MODEL, D_FF_EXPERT), D_MODEL ** -0.5),
        "exp_w_up": nrm((DEPTH, N_EXPERTS, D_MODEL, D_FF_EXPERT), D_MODEL ** -0.5),
        "exp_w_down": nrm((DEPTH, N_EXPERTS, D_FF_EXPERT, D_MODEL), beta * D_FF_EXPERT ** -0.5),
        "ln_ffn_w": 1.0 + nrm((DEPTH, D_MODEL), 0.02),
        "ln_ffn_b": nrm((DEPTH, D_MODEL), 0.02),
    }


def reference(x, c, ctx, c_ctx, w_mod, b_mod, w_in, mlstm_conv_w, mlstm_conv_b, mlstm_gate_b,
              mlstm_norm_w, attn_q_norm_w, attn_k_norm_w, hyena_conv_w, hyena_conv_b,
              hyena_f_w1, hyena_f_b1, hyena_f_freq, hyena_f_w2, hyena_f_b2, hyena_f_w3,
              hyena_skip, w_out, ln_mix_w, ln_mix_b, router_w, router_b,
              exp_w_gate, exp_w_up, exp_w_down, ln_ffn_w, ln_ffn_b):
    alpha = (2.0 * DEPTH) ** 0.25
    for l in range(DEPTH):
        last = l == DEPTH - 1
        mod_x = jnp.split((jax.nn.silu(c) @ w_mod[l] + b_mod[l])[:, None, :], N_MOD, axis=-1)
        mod_c = jnp.split((jax.nn.silu(c_ctx) @ w_mod[l] + b_mod[l])[None, None, :], N_MOD, axis=-1)
        mix_x, mix_c = _mixing_sublayer(
            _modulate(x, mod_x[0], mod_x[1]), _modulate(ctx, mod_c[0], mod_c[1]),
            w_in[l], mlstm_conv_w[l], mlstm_conv_b[l], mlstm_gate_b[l], mlstm_norm_w[l],
            attn_q_norm_w[l], attn_k_norm_w[l], hyena_conv_w[l], hyena_conv_b[l],
            (hyena_f_w1[l], hyena_f_b1[l], hyena_f_freq[l], hyena_f_w2[l], hyena_f_b2[l], hyena_f_w3[l]),
            hyena_skip[l], w_out[l], not last)
        x = _layer_norm(alpha * x + mod_x[2] * mix_x, ln_mix_w[l], ln_mix_b[l])
        ffn_x = _expert_choice_ffn(_modulate(x, mod_x[3], mod_x[4]), router_w[l], router_b[l],
                                   exp_w_gate[l], exp_w_up[l], exp_w_down[l])
        x = _layer_norm(alpha * x + mod_x[5] * ffn_x, ln_ffn_w[l], ln_ffn_b[l])
        if not last:
            ctx = _layer_norm(alpha * ctx + mod_c[2] * mix_c, ln_mix_w[l], ln_mix_b[l])
            ffn_c = _expert_choice_ffn(_modulate(ctx, mod_c[3], mod_c[4]), router_w[l], router_b[l],
                                       exp_w_gate[l], exp_w_up[l], exp_w_down[l])
            ctx = _layer_norm(alpha * ctx + mod_c[5] * ffn_c, ln_ffn_w[l], ln_ffn_b[l])
    return x
```

```python
import functools
import math

import numpy as np
import jax
import jax.numpy as jnp
from jax import lax
from jax.experimental import pallas as pl
from jax.experimental.pallas import tpu as pltpu

F32 = jnp.float32
BF16 = jnp.bfloat16
HI = lax.Precision.HIGHEST

D_MODEL = 1024
DEPTH = 2
GRID_W = 64
CTX_LEN = 256
N_DIR = 2
SHORT_CONV = 3

D_MLSTM = 256
MLSTM_HEAD_DIM = 64
MLSTM_HEADS = 4
MLSTM_GATES = 16
MLSTM_CHUNK = 128

D_ATTN = 512
ATTN_HEAD_DIM = 64
ATTN_HEADS = 8
ATTN_KV_HEADS = 2
ATTN_GROUP = 4
D_KV = 128
ROPE_THETA = 10000.0

D_HYENA = 256
HYENA_ORDER = 2
HYENA_BANDS = 16
HYENA_FAST_DECAY = 0.3
HYENA_SLOW_DECAY = 1.5
HYENA_DECAY_TARGET = 1e-2
HYENA_WINDOW_SHIFT = 0.05

N_IN = 4 * D_MLSTM + MLSTM_GATES + D_ATTN + 2 * D_KV + 3 * D_HYENA
N_EXPERTS = 16
EC_CAPACITY_FACTOR = 2
D_FF_EXPERT = 2816
N_MOD = 6
LN_EPS = 1e-5
RMS_EPS = 1e-6

LANES = 128
TOKEN_TILE = 256
FFT_R = 128
FFT_GROUP = 8
VMEM_LIMIT = 56 * 1024 * 1024


def _cparams(sem, vmem=None):
    return pltpu.CompilerParams(dimension_semantics=sem, vmem_limit_bytes=vmem)


def _mod_kernel(c_ref, w_ref, b_ref, o_ref):
    cs = c_ref[...]
    cs = cs * jax.nn.sigmoid(cs)
    o_ref[...] = jnp.dot(cs, w_ref[0], precision=HI, preferred_element_type=F32) + b_ref[...]


def _modulation(crows, w_mod, b_mod, layer):
    rows, d = crows.shape
    n = w_mod.shape[2]
    tn = 1024
    return pl.pallas_call(
        _mod_kernel,
        out_shape=jax.ShapeDtypeStruct((rows, n), F32),
        grid=(n // tn,),
        in_specs=[pl.BlockSpec((rows, d), lambda j: (0, 0)),
                  pl.BlockSpec((1, d, tn), lambda j: (layer, 0, j)),
                  pl.BlockSpec((1, tn), lambda j: (0, j))],
        out_specs=pl.BlockSpec((rows, tn), lambda j: (0, j)),
        compiler_params=_cparams(("parallel",)),
        name="adaln_mod",
    )(crows, w_mod, b_mod.reshape(1, n))


def _ln_rows(x):
    mu = jnp.mean(x, axis=-1, keepdims=True)
    xc = x - mu
    var = jnp.mean(xc * xc, axis=-1, keepdims=True)
    return xc * lax.rsqrt(var + LN_EPS)


def _mod_spec():
    return pl.BlockSpec((1, 1, N_MOD, D_MODEL), lambda b, i: (b, jnp.minimum(i, 1), 0, 0))


GATE_PAD = 128
IN_GROUPS = (("mqk", 2 * D_MLSTM, 0), ("mv", D_MLSTM, 2 * D_MLSTM), ("mo", D_MLSTM, 3 * D_MLSTM),
             ("aq", D_ATTN, 4 * D_MLSTM + MLSTM_GATES), ("ak", D_KV, 4 * D_MLSTM + MLSTM_GATES + D_ATTN),
             ("av", D_KV, 4 * D_MLSTM + MLSTM_GATES + D_ATTN + D_KV),
             ("hy", 3 * D_HYENA, 4 * D_MLSTM + MLSTM_GATES + D_ATTN + 2 * D_KV),
             ("mg", GATE_PAD, 4 * D_MLSTM))


def _inproj_kernel(x_ref, mod_ref, w_ref, *o_refs):
    y = _ln_rows(x_ref[0])
    h = (y * (1.0 + mod_ref[0, 0, 1:2, :]) + mod_ref[0, 0, 0:1, :]).astype(BF16)
    off = 0
    for (_, width, _), o_ref in zip(IN_GROUPS, o_refs):
        o_ref[0] = jnp.dot(h, w_ref[:, off:off + width], preferred_element_type=F32)
        off += width


def _in_projection(x, mod, w_in):
    b, na, d = x.shape
    assert IN_GROUPS[-1][0] == "mg"
    cols = [w_in[:, start:start + width] for _, width, start in IN_GROUPS[:-1]]
    cols += [w_in[:, IN_GROUPS[-1][2]:IN_GROUPS[-1][2] + MLSTM_GATES],
             jnp.zeros((d, GATE_PAD - MLSTM_GATES), w_in.dtype)]
    w = jnp.concatenate(cols, axis=1).astype(BF16)
    n = w.shape[1]
    return pl.pallas_call(
        _inproj_kernel,
        out_shape=[jax.ShapeDtypeStruct((b, na, width), F32) for _, width, _ in IN_GROUPS],
        grid=(b, na // TOKEN_TILE),
        in_specs=[pl.BlockSpec((1, TOKEN_TILE, d), lambda bi, i: (bi, i, 0)),
                  _mod_spec(),
                  pl.BlockSpec((d, n), lambda bi, i: (0, 0))],
        out_specs=[pl.BlockSpec((1, TOKEN_TILE, width), lambda bi, i: (bi, i, 0)) for _, width, _ in IN_GROUPS],
        compiler_params=_cparams(("parallel", "parallel"), VMEM_LIMIT),
        name="in_proj",
    )(x, mod, w)


def _outproj_kernel(alpha, hm0_ref, hm1_ref, mo_ref, nw_ref, seg_ref, ya_ref, yh_ref, x_ref, mod_ref, w_ref,
                    lw_ref, lb_ref, o_ref):
    hm = hm0_ref[0, 0] + hm1_ref[0, 0]
    ss = jnp.dot(hm * hm, seg_ref[...], precision=HI, preferred_element_type=F32)
    hn = hm * lax.rsqrt(ss * (1.0 / MLSTM_HEAD_DIM) + RMS_EPS) * nw_ref[...]
    ym = jax.nn.sigmoid(mo_ref[0]) * hn
    mix = jnp.dot(ym.astype(BF16), w_ref[0:D_MLSTM, :], preferred_element_type=F32)
    off = D_MLSTM
    for y_ref in (ya_ref, yh_ref):
        width = y_ref.shape[2]
        mix = mix + jnp.dot(y_ref[0].astype(BF16), w_ref[off:off + width, :], preferred_element_type=F32)
        off += width
    r = alpha * x_ref[0] + mod_ref[0, 0, 2:3, :] * mix
    o_ref[0] = _ln_rows(r) * lw_ref[...] + lb_ref[...]


def _out_projection(hm, mo, norm_w, ya, yh, x, mod, w_bf16, ln_w, ln_b, alpha):
    b, na, d = x.shape
    dm = mo.shape[2]
    head = np.arange(dm) // MLSTM_HEAD_DIM
    seg = jnp.asarray((head[:, None] == head[None, :]).astype(np.float32))

    def tile(width):
        return pl.BlockSpec((1, TOKEN_TILE, width), lambda bi, i: (bi, i, 0))

    def hm_spec(direction):
        return pl.BlockSpec((1, 1, TOKEN_TILE, dm), lambda bi, i: (direction, bi, i, 0))

    vec = pl.BlockSpec((1, d), lambda bi, i: (0, 0))
    return pl.pallas_call(
        functools.partial(_outproj_kernel, alpha),
        out_shape=jax.ShapeDtypeStruct((b, na, d), F32),
        grid=(b, na // TOKEN_TILE),
        in_specs=[hm_spec(0), hm_spec(1), tile(dm), pl.BlockSpec((1, dm), lambda bi, i: (0, 0)),
                  pl.BlockSpec((dm, dm), lambda bi, i: (0, 0)), tile(ya.shape[2]), tile(yh.shape[2]), tile(d),
                  _mod_spec(), pl.BlockSpec((d, d), lambda bi, i: (0, 0)), vec, vec],
        out_specs=tile(d),
        compiler_params=_cparams(("parallel", "parallel"), VMEM_LIMIT),
        name="out_proj_ln",
    )(hm, hm, mo, norm_w.reshape(1, dm), seg, ya, yh, x, mod, w_bf16, ln_w.reshape(1, d), ln_b.reshape(1, d))


def _resid_ln_kernel(alpha, f_ref, x_ref, mod_ref, lw_ref, lb_ref, o_ref):
    f = jnp.concatenate([f_ref[0, :, j, :] for j in range(f_ref.shape[2])], axis=1)
    r = alpha * x_ref[0] + mod_ref[0, 0, 5:6, :] * f
    o_ref[0] = _ln_rows(r) * lw_ref[...] + lb_ref[...]


def _ffn_residual_ln(f, x, mod, ln_w, ln_b, alpha):
    b, na, d = x.shape
    tile = pl.BlockSpec((1, TOKEN_TILE, d), lambda bi, i: (bi, i, 0))
    vec = pl.BlockSpec((1, d), lambda bi, i: (0, 0))
    return pl.pallas_call(
        functools.partial(_resid_ln_kernel, alpha),
        out_shape=jax.ShapeDtypeStruct((b, na, d), F32),
        grid=(b, na // TOKEN_TILE),
        in_specs=[pl.BlockSpec((1, TOKEN_TILE, d // LANES, LANES), lambda bi, i: (bi, i, 0, 0)),
                  tile, _mod_spec(), vec, vec],
        out_specs=tile,
        compiler_params=_cparams(("parallel", "parallel")),
        name="ffn_resid_ln",
    )(f, x, mod, ln_w.reshape(1, d), ln_b.reshape(1, d))


def _mlstm_kernel(qk_ref, v_ref, gr_ref, gc_ref, o_ref, c_sc, n_sc, m_sc):
    t = MLSTM_CHUNK
    nh = MLSTM_HEADS
    d = pl.program_id(0)

    @pl.when(pl.program_id(2) == 0)
    def _():
        c_sc[...] = jnp.zeros_like(c_sc)
        n_sc[...] = jnp.zeros_like(n_sc)
        m_sc[...] = jnp.zeros_like(m_sc)

    row = lax.broadcasted_iota(jnp.int32, (t, t), 0)
    col = lax.broadcasted_iota(jnp.int32, (t, t), 1)
    mask = jnp.where(d == 0, col - row, row - col) <= 0
    maskf = mask.astype(F32)
    grows = gr_ref[0, 0]
    gcols = gc_ref[0, 0]
    cum_cols = jnp.dot(maskf, gcols, precision=HI, preferred_element_type=F32)
    cum_rows = lax.dot_general(grows, maskf, (((1,), (1,)), ((), ())), precision=HI,
                               preferred_element_type=F32)
    lane8 = lax.broadcasted_iota(jnp.int32, (t, 2 * nh), 1)

    def column(a, idx):
        return jnp.sum(jnp.where(lane8 == idx, a, 0.0), axis=1, keepdims=True)

    dh = MLSTM_HEAD_DIM
    for h in range(nh):
        qc = qk_ref[0, :, h * dh:(h + 1) * dh] * (dh ** -0.5)
        kc = qk_ref[0, :, (nh + h) * dh:(nh + h + 1) * dh]
        vc = v_ref[0, :, h * dh:(h + 1) * dh]
        ic_row = grows[h:h + 1, :]
        ic_col = column(gcols, h)
        bcol = column(cum_cols, nh + h)
        brow = cum_rows[nh + h:nh + h + 1, :]
        tot = jnp.sum(grows[nh + h:nh + h + 1, :], axis=1, keepdims=True)
        m0 = m_sc[h:h + 1, 0:1]
        log_inter = bcol + m0
        log_intra = jnp.where(mask, bcol - brow + ic_row, -jnp.inf)
        mrow = jnp.maximum(log_inter, jnp.max(log_intra, axis=1, keepdims=True))
        w_inter = jnp.exp(log_inter - mrow)
        qb = qc.astype(BF16)
        vb = vc.astype(BF16)
        scores = lax.dot_general(qb, kc.astype(BF16), (((1,), (1,)), ((), ())),
                                 preferred_element_type=F32) * jnp.exp(log_intra - mrow)
        ct = c_sc[h]
        n0 = n_sc[h]
        num = (w_inter * jnp.dot(qb, ct.astype(BF16), preferred_element_type=F32)
               + jnp.dot(scores.astype(BF16), vb, preferred_element_type=F32))
        den = w_inter * jnp.sum(qc * n0, axis=1, keepdims=True) + jnp.sum(scores, axis=1, keepdims=True)
        o_ref[0, 0, :, h * dh:(h + 1) * dh] = num / jnp.maximum(jnp.abs(den), jnp.exp(-mrow))
        log_src = tot - bcol + ic_col
        m_new = jnp.maximum(tot + m0, jnp.max(log_src, axis=0, keepdims=True))
        wk = jnp.exp(log_src - m_new) * kc
        decay = jnp.exp(tot + m0 - m_new)
        c_sc[h] = decay * ct + lax.dot_general(wk.astype(BF16), vb, (((0,), (0,)), ((), ())),
                                               preferred_element_type=F32)
        n_sc[h] = decay * n0 + jnp.sum(wk, axis=0, keepdims=True)
        m_sc[h:h + 1, :] = jnp.broadcast_to(m_new, (1, 128))


def _mlstm_scan(qk, v, grows, gcols):
    b, na, dv = v.shape
    nh, dh = MLSTM_HEADS, MLSTM_HEAD_DIM
    t = MLSTM_CHUNK
    nc = na // t
    nctx = CTX_LEN // t

    def chunk(d, c):
        rev = jnp.where(c < nctx, nctx - 1 - c, nc + nctx - 1 - c)
        return jnp.where(d == 0, c, rev)

    return pl.pallas_call(
        _mlstm_kernel,
        out_shape=jax.ShapeDtypeStruct((N_DIR, b, na, dv), F32),
        grid=(N_DIR, b, nc),
        in_specs=[pl.BlockSpec((1, t, 2 * dv), lambda d, bi, c: (bi, chunk(d, c), 0)),
                  pl.BlockSpec((1, t, dv), lambda d, bi, c: (bi, chunk(d, c), 0)),
                  pl.BlockSpec((1, 1, 2 * nh, t), lambda d, bi, c: (d, bi, 0, chunk(d, c))),
                  pl.BlockSpec((1, 1, t, 2 * nh), lambda d, bi, c: (d, bi, chunk(d, c), 0))],
        out_specs=pl.BlockSpec((1, 1, t, dv), lambda d, bi, c: (d, bi, chunk(d, c), 0)),
        scratch_shapes=[pltpu.VMEM((nh, dh, dh), F32), pltpu.VMEM((nh, 1, dh), F32), pltpu.VMEM((8, 128), F32)],
        compiler_params=_cparams(("parallel", "parallel", "arbitrary")),
        name="mlstm_scan",
    )(qk, v, grows, gcols)


ATTN_TQ = 256
LOG2E = 1.4426950408889634


def _attn_kernel(q_ref, k_ref, v_ref, o_ref, s_ref, p_ref):
    dh = q_ref.shape[3]
    s_ref[...] = lax.dot_general(q_ref[0, 0], k_ref[0, 0], (((1,), (1,)), ((), ())), preferred_element_type=F32)
    m = jnp.max(s_ref[...], axis=1, keepdims=True)
    p_ref[...] = jnp.exp2(s_ref[...] - m).astype(BF16)
    acc = jnp.dot(p_ref[...], v_ref[0, 0], preferred_element_type=F32)
    o_ref[0, 0] = acc[:, 0:dh] / acc[:, dh:dh + 1]


def _attention(q, k, v):
    b, hk, nq, dh = q.shape
    nk = k.shape[2]
    qb = (q * LOG2E).astype(BF16)
    kb = k.astype(BF16)
    vb = jnp.concatenate([v, jnp.ones((b, hk, nk, 1), F32), jnp.zeros((b, hk, nk, LANES - dh - 1), F32)],
                         axis=3).astype(BF16)
    kv_spec = lambda width: pl.BlockSpec((1, 1, nk, width), lambda bi, h, i: (bi, h, 0, 0))
    return pl.pallas_call(
        _attn_kernel,
        out_shape=jax.ShapeDtypeStruct((b, hk, nq, dh), F32),
        grid=(b, hk, nq // ATTN_TQ),
        in_specs=[pl.BlockSpec((1, 1, ATTN_TQ, dh), lambda bi, h, i: (bi, h, i, 0)), kv_spec(dh), kv_spec(LANES)],
        out_specs=pl.BlockSpec((1, 1, ATTN_TQ, dh), lambda bi, h, i: (bi, h, i, 0)),
        scratch_shapes=[pltpu.VMEM((ATTN_TQ, nk), F32), pltpu.VMEM((ATTN_TQ, nk), BF16)],
        compiler_params=_cparams(("parallel", "parallel", "parallel"), VMEM_LIMIT),
        name="attention",
    )(qb, kb, vb)


def _dft_tables(t2_len):
    r = FFT_R
    n = r * r
    idx = np.arange(r, dtype=np.float64)
    kb = idx[None, :, None]
    t1 = idx[:, None, None]
    t2 = np.arange(t2_len, dtype=np.float64)[None, None, :]
    ang = -2.0 * np.pi * (t2 * kb / r + t1 * kb / n)
    f1 = np.concatenate([np.cos(ang), np.sin(ang)], axis=1)
    ang2 = -2.0 * np.pi * np.outer(idx, idx) / r
    f2 = np.stack([np.cos(ang2), np.sin(ang2)])
    return f1.astype(np.float32), f2.astype(np.float32)


def _idft_table(t2_len):
    r = FFT_R
    n = r * r
    t1 = np.arange(r, dtype=np.float64)[:, None, None]
    t2 = np.arange(t2_len, dtype=np.float64)[None, :, None]
    kb = np.arange(r, dtype=np.float64)[None, None, :]
    ang = 2.0 * np.pi * (t2 * kb / r + t1 * kb / n)
    return (np.stack([np.cos(ang), np.sin(ang)], axis=1) / n).astype(np.float32)


def _dot_hi(table, x):
    return jnp.dot(table, x.astype(BF16), preferred_element_type=F32)


def _fft1_kernel(f_ref, z_ref, g_ref):
    for i in range(FFT_GROUP):
        g_ref[0, i] = _dot_hi(f_ref[i], z_ref[0, :, i, :])


def _fft_stage1(z4, f1):
    b, t2_len, r, c = z4.shape
    g = FFT_GROUP
    return pl.pallas_call(
        _fft1_kernel,
        out_shape=jax.ShapeDtypeStruct((b, r, 2 * r, c), F32),
        grid=(b, r // g),
        in_specs=[pl.BlockSpec((g, 2 * r, t2_len), lambda bi, j: (j, 0, 0)),
                  pl.BlockSpec((1, t2_len, g, c), lambda bi, j: (bi, 0, j, 0))],
        out_specs=pl.BlockSpec((1, g, 2 * r, c), lambda bi, j: (bi, j, 0, 0)),
        compiler_params=_cparams(("parallel", "parallel")),
        name="fft_stage1",
    )(f1, z4)


def _fft2_kernel(with_filter, f_ref, gr_ref, gi_ref, aux_ref, h_ref):
    fr = f_ref[0]
    fi = f_ref[1]
    for i in range(FFT_GROUP):
        gr = gr_ref[0, :, i, :]
        gi = gi_ref[0, :, i, :]
        xr = _dot_hi(fr, gr) - _dot_hi(fi, gi)
        xi = _dot_hi(fr, gi) + _dot_hi(fi, gr)
        if with_filter:
            tr = aux_ref[0, 0, i]
            ti = aux_ref[0, 1, i]
            yr = xr * tr - xi * ti
            yi = xr * ti + xi * tr
            xr = _dot_hi(fr, yr) + _dot_hi(fi, yi)
            xi = _dot_hi(fr, yi) - _dot_hi(fi, yr)
        else:
            xr = xr * aux_ref[0]
            xi = xi * aux_ref[0]
        h_ref[0, 0, i] = xr
        h_ref[0, 1, i] = xi


def _fft_stage2(g, f2, tf=None, scale=None):
    b, r, _, c = g.shape
    grp = FFT_GROUP
    nblk = r // grp
    in_specs = [pl.BlockSpec((2, r, r), lambda bi, j: (0, 0, 0)),
                pl.BlockSpec((1, r, grp, c), lambda bi, j: (bi, 0, j, 0)),
                pl.BlockSpec((1, r, grp, c), lambda bi, j: (bi, 0, nblk + j, 0))]
    args = [f2, g, g]
    if tf is not None:
        in_specs.append(pl.BlockSpec((1, 2, grp, r, c), lambda bi, j: (0, 0, j, 0, 0)))
        args.append(tf)
    else:
        in_specs.append(pl.BlockSpec((1, 1, c), lambda bi, j: (bi, 0, 0)))
        args.append(scale)
    return pl.pallas_call(
        functools.partial(_fft2_kernel, tf is not None),
        out_shape=jax.ShapeDtypeStruct((b, 2, r, r, c), F32),
        grid=(b, nblk),
        in_specs=in_specs,
        out_specs=pl.BlockSpec((1, 2, grp, r, c), lambda bi, j: (bi, 0, j, 0, 0)),
        compiler_params=_cparams(("parallel", "parallel"), VMEM_LIMIT),
        name="fft_stage2",
    )(*args)


def _ifft_kernel(e_ref, h_ref, z_ref, x_ref, skip_ref, o_ref):
    for i in range(FFT_GROUP):
        y = _dot_hi(e_ref[i, 0], h_ref[0, 0, :, i, :]) - _dot_hi(e_ref[i, 1], h_ref[0, 1, :, i, :])
        y = y * (1.0 / (FFT_R * FFT_R))
        o_ref[0, :, i, :] = x_ref[0, :, i, :] * (y + skip_ref[...] * z_ref[0, :, i, :])


def _ifft_gate(h, e, z4, x4, skip):
    b, t2_len, r, c = z4.shape
    g = FFT_GROUP
    tok = pl.BlockSpec((1, t2_len, g, c), lambda bi, j: (bi, 0, j, 0))
    return pl.pallas_call(
        _ifft_kernel,
        out_shape=jax.ShapeDtypeStruct(z4.shape, F32),
        grid=(b, r // g),
        in_specs=[pl.BlockSpec((g, 2, t2_len, r), lambda bi, j: (j, 0, 0, 0)),
                  pl.BlockSpec((1, 2, r, g, c), lambda bi, j: (bi, 0, 0, j, 0)),
                  tok, tok, pl.BlockSpec((1, c), lambda bi, j: (0, 0))],
        out_specs=tok,
        compiler_params=_cparams(("parallel", "parallel"), VMEM_LIMIT),
        name="ifft_gate",
    )(e, h, z4, x4, skip.reshape(1, c))


TAPS_ROWS = 1024
FEAT_PAD = 128


def _taps_kernel(n, feat_ref, w1_ref, b1_ref, freq_ref, w2_ref, b2_ref, w3_ref, delta_ref, taps_ref, l1_ref):
    c = D_HYENA
    rows = feat_ref.shape[0]
    step = pl.program_id(0)

    @pl.when(step == 0)
    def _():
        l1_ref[...] = jnp.zeros_like(l1_ref)

    f = feat_ref[...]
    hid = jnp.sin(freq_ref[0:1, :] * (jnp.dot(f, w1_ref[...], precision=HI, preferred_element_type=F32) + b1_ref[...]))
    hid = jnp.sin(freq_ref[1:2, :] * (jnp.dot(hid, w2_ref[...], precision=HI, preferred_element_type=F32) + b2_ref[...]))
    filt = jnp.dot(hid, w3_ref[...], precision=HI, preferred_element_type=F32)
    window = jnp.exp(-f[:, 0:1] * delta_ref[...]) + HYENA_WINDOW_SHIFT
    i = step * rows + lax.broadcasted_iota(jnp.int32, (rows, c), 0)
    for o in range(HYENA_ORDER):
        fwd = filt[:, o * c:(o + 1) * c]
        bwd = filt[:, (HYENA_ORDER + o) * c:(HYENA_ORDER + o + 1) * c]
        tap = jnp.where(i < n, fwd, jnp.where(i > n, bwd, 0.0)) * window
        taps_ref[o] = tap
        l1_ref[o:o + 1, :] += jnp.sum(jnp.abs(tap), axis=0, keepdims=True)


def _hyena_taps(n, f_w1, f_b1, f_freq, f_w2, f_b2, f_w3):
    i = np.arange(2 * n)
    t = (np.where(i < n, i, 2 * n - i).astype(np.float32) / np.float32(n)).astype(np.float32)
    ang = (np.float32(2.0 * math.pi) * t[:, None]) * np.arange(1, HYENA_BANDS + 1, dtype=np.float32)
    feats = np.zeros((2 * n, FEAT_PAD), np.float32)
    feats[:, 0] = t
    feats[:, 1:1 + HYENA_BANDS] = np.cos(ang.astype(np.float64))
    feats[:, 1 + HYENA_BANDS:1 + 2 * HYENA_BANDS] = np.sin(ang.astype(np.float64))
    log_target = abs(math.log(HYENA_DECAY_TARGET))
    deltas = jnp.linspace(log_target / HYENA_SLOW_DECAY, log_target / HYENA_FAST_DECAY, D_HYENA, dtype=F32)
    hid = f_w2.shape[0]
    w1 = jnp.concatenate([f_w1, jnp.zeros((FEAT_PAD - f_w1.shape[0], hid), F32)], axis=0)
    rows = min(TAPS_ROWS, 2 * n)
    full = lambda shape: pl.BlockSpec(shape, lambda j: (0,) * len(shape))
    return pl.pallas_call(
        functools.partial(_taps_kernel, n),
        out_shape=(jax.ShapeDtypeStruct((HYENA_ORDER, 2 * n, D_HYENA), F32),
                   jax.ShapeDtypeStruct((HYENA_ORDER, D_HYENA), F32)),
        grid=(2 * n // rows,),
        in_specs=[pl.BlockSpec((rows, FEAT_PAD), lambda j: (j, 0)), full((FEAT_PAD, hid)), full((1, hid)),
                  full((2, hid)), full((hid, hid)), full((1, hid)), full(f_w3.shape), full((1, D_HYENA))],
        out_specs=(pl.BlockSpec((HYENA_ORDER, rows, D_HYENA), lambda j: (0, j, 0)), full((HYENA_ORDER, D_HYENA))),
        compiler_params=_cparams(("arbitrary",)),
        name="hyena_taps",
    )(jnp.asarray(feats), w1, f_b1.reshape(1, hid), f_freq, f_w2, f_b2.reshape(1, hid), f_w3, deltas.reshape(1, -1))


def _hyena_latent(v, x1, x2, taps, l1, skip):
    b, n, c = v.shape
    r = FFT_R
    t2_len = n // r
    f1_full, f2 = _dft_tables(r)
    f1_half = jnp.asarray(f1_full[:, :, :t2_len]).astype(BF16)
    f1_full = jnp.asarray(f1_full).astype(BF16)
    f2 = jnp.asarray(f2).astype(BF16)
    e = jnp.asarray(_idft_table(t2_len) * (r * r)).astype(BF16)
    taps4 = taps.reshape(HYENA_ORDER, r, r, c)
    tf = _fft_stage2(_fft_stage1(taps4, f1_full), f2, scale=(1.0 / l1)[:, None, :])
    z = v.reshape(b, t2_len, r, c)
    for o, gate in enumerate((x1, x2)):
        g = _fft_stage1(z, f1_half)
        h = _fft_stage2(g, f2, tf[o:o + 1])
        z = _ifft_gate(h, e, z, gate.reshape(b, t2_len, r, c), skip[o])
    return z.reshape(b, n, c)


def _hyena_ctx_kernel(n, v_ref, x1_ref, x2_ref, k_ref, skip_ref, o_ref, z_sc):
    z_sc[...] = v_ref[0]
    for o, gate_ref in enumerate((x1_ref, x2_ref)):
        def body(s, acc):
            return acc + k_ref[o, pl.ds(n - 1 - s, n), :] * z_sc[pl.ds(s, 1), :]
        conv = lax.fori_loop(0, n, body, jnp.zeros(z_sc.shape, F32))
        z_sc[...] = gate_ref[0] * (conv + skip_ref[o:o + 1, :] * z_sc[...])
    o_ref[0] = z_sc[...]


def _hyena_context(v, x1, x2, taps, l1, skip):
    b, n, c = v.shape
    cb = 128
    k2 = jnp.roll(taps, n - 1, axis=1) / l1[:, None, :]
    tok = pl.BlockSpec((1, n, cb), lambda bi, j: (bi, 0, j))
    return pl.pallas_call(
        functools.partial(_hyena_ctx_kernel, n),
        out_shape=jax.ShapeDtypeStruct((b, n, c), F32),
        grid=(b, c // cb),
        in_specs=[tok, tok, tok,
                  pl.BlockSpec((HYENA_ORDER, 2 * n, cb), lambda bi, j: (0, 0, j)),
                  pl.BlockSpec((HYENA_ORDER, cb), lambda bi, j: (0, j))],
        out_specs=tok,
        scratch_shapes=[pltpu.VMEM((n, cb), F32)],
        compiler_params=_cparams(("parallel", "parallel")),
        name="hyena_ctx",
    )(v, x1, x2, k2, skip)


def _router_kernel(x_ref, mod_ref, rw_ref, rb_ref, h_ref, aff_ref):
    y = _ln_rows(x_ref[0])
    h = y * (1.0 + mod_ref[0, 0, 4:5, :]) + mod_ref[0, 0, 3:4, :]
    for j in range(h_ref.shape[2]):
        h_ref[0, :, j, :] = h[:, j * LANES:(j + 1) * LANES]
    logits = lax.dot_general(rw_ref[...], h, (((1,), (1,)), ((), ())), precision=HI,
                             preferred_element_type=F32) + rb_ref[...]
    z = jnp.exp(logits - jnp.max(logits, axis=0, keepdims=True))
    aff_ref[0] = z / jnp.sum(z, axis=0, keepdims=True)


def _router(x, mod, router_w, router_b):
    b, na, d = x.shape
    e = router_w.shape[1]
    return pl.pallas_call(
        _router_kernel,
        out_shape=(jax.ShapeDtypeStruct((b, na, d // LANES, LANES), F32), jax.ShapeDtypeStruct((b, e, na), F32)),
        grid=(b, na // TOKEN_TILE),
        in_specs=[pl.BlockSpec((1, TOKEN_TILE, d), lambda bi, i: (bi, i, 0)),
                  _mod_spec(),
                  pl.BlockSpec((e, d), lambda bi, i: (0, 0)),
                  pl.BlockSpec((e, 1), lambda bi, i: (0, 0))],
        out_specs=(pl.BlockSpec((1, TOKEN_TILE, d // LANES, LANES), lambda bi, i: (bi, i, 0, 0)),
                   pl.BlockSpec((1, e, TOKEN_TILE), lambda bi, i: (bi, 0, i))),
        compiler_params=_cparams(("parallel", "parallel")),
        name="router",
    )(x, mod, router_w.T, router_b.reshape(e, 1))


def _prefix_count(x):
    n = x.shape[1]
    lane = lax.broadcasted_iota(jnp.int32, x.shape, 1)
    sh = 1
    while sh < n:
        x = x + jnp.where(lane >= sh, pltpu.roll(x, sh, axis=1), 0)
        sh *= 2
    return x


SELECT_BISECTIONS = 160


def _select_kernel(segments, aff_ref, sel_ref, pos_ref):
    nb, ne, _ = aff_ref.shape
    parts = [(b, s0, s1, cap) for b in range(nb) for (s0, s1, cap) in segments]

    def body(_, bounds):
        out = []
        for (b, s0, s1, cap), (lo, hi) in zip(parts, bounds):
            mid = 0.5 * (lo + hi)
            cnt = jnp.sum((aff_ref[b, :, s0:s1] >= mid).astype(jnp.int32), axis=1, keepdims=True)
            ok = cnt >= cap
            out.append((jnp.where(ok, mid, lo), jnp.where(ok, hi, mid)))
        return tuple(out)

    init = tuple((jnp.zeros((ne, 1), F32), jnp.full((ne, 1), 2.0, F32)) for _ in parts)
    bounds = lax.fori_loop(0, SELECT_BISECTIONS, body, init)
    for (b, s0, s1, cap), (lo, hi) in zip(parts, bounds):
        a = aff_ref[b, :, s0:s1]
        above = (a >= hi).astype(jnp.int32)
        tied = jnp.where(a >= lo, 1, 0) - above
        need = cap - jnp.sum(above, axis=1, keepdims=True)
        tie_rank = _prefix_count(tied) - tied
        sel = above + tied * (tie_rank < need).astype(jnp.int32)
        sel_ref[b, :, s0:s1] = sel
        pos_ref[b, :, s0:s1] = _prefix_count(sel) - sel


def _select(aff, segments):
    b, e, na = aff.shape
    blk = pl.BlockSpec((b, e, na), lambda i: (0, 0, 0))
    return pl.pallas_call(
        functools.partial(_select_kernel, segments),
        out_shape=(jax.ShapeDtypeStruct((b, e, na), jnp.int32), jax.ShapeDtypeStruct((b, e, na), jnp.int32)),
        grid=(1,),
        in_specs=[blk],
        out_specs=(blk, blk),
        compiler_params=_cparams(("arbitrary",)),
        name="expert_select",
    )(aff)


MOE_ROW_TILE = 264


def _row_tile(r):
    return next(t for t in range(MOE_ROW_TILE, 7, -8) if r % t == 0)


DMA_UNROLL = 8


def _for_each_row(tr, fn):
    def body(i, c):
        fn(i)
        return c
    lax.fori_loop(0, tr, body, 0, unroll=DMA_UNROLL)


def _gather_kernel(tr, idx_ref, h_hbm, o_ref, buf, sem):
    base = (pl.program_id(0) * pl.num_programs(1) + pl.program_id(1)) * tr
    _for_each_row(tr, lambda i: pltpu.make_async_copy(
        h_hbm.at[pl.ds(idx_ref[base + i], 1)], buf.at[pl.ds(i, 1)], sem).start())
    pltpu.make_async_copy(h_hbm.at[pl.ds(0, tr)], buf, sem).wait()
    for j in range(buf.shape[1]):
        o_ref[0, :, j * LANES:(j + 1) * LANES] = buf[:, j, :].astype(BF16)


def _gather_rows(h3, idx):
    e, r = idx.shape
    _, s, lanes = h3.shape
    tr = _row_tile(r)
    return pl.pallas_call(
        functools.partial(_gather_kernel, tr),
        out_shape=jax.ShapeDtypeStruct((e, r, s * lanes), BF16),
        grid_spec=pltpu.PrefetchScalarGridSpec(
            num_scalar_prefetch=1, grid=(e, r // tr),
            in_specs=[pl.BlockSpec(memory_space=pl.ANY)],
            out_specs=pl.BlockSpec((1, tr, s * lanes), lambda ei, j, idx_ref: (ei, j, 0)),
            scratch_shapes=[pltpu.VMEM((tr, s, lanes), F32), pltpu.SemaphoreType.DMA(())]),
        compiler_params=_cparams(("arbitrary", "arbitrary")),
        name="moe_gather",
    )(idx.reshape(-1), h3)


MOE_TF = 256
MOE_ROW_CHUNKS = 4


def _expert_ffn_kernel(xs_ref, wg_ref, wu_ref, wd_ref, tv_ref, o_ref):
    j = pl.program_id(1)
    wg = wg_ref[0, 0].astype(BF16)
    wu = wu_ref[0, 0].astype(BF16)
    wd = wd_ref[0, 0].astype(BF16)
    rows = xs_ref.shape[1]
    rc = rows // MOE_ROW_CHUNKS

    @pl.when(j == 0)
    def _():
        o_ref[...] = jnp.zeros_like(o_ref)

    for ci in range(MOE_ROW_CHUNKS):
        sl = slice(ci * rc, (ci + 1) * rc)
        x = xs_ref[0, sl, :]
        g = jnp.dot(x, wg, preferred_element_type=F32)
        u = jnp.dot(x, wu, preferred_element_type=F32)
        a = (g * jax.nn.sigmoid(g) * u).astype(BF16)
        o_ref[0, sl, :] += jnp.dot(a, wd, preferred_element_type=F32)

    @pl.when(j == pl.num_programs(1) - 1)
    def _():
        o_ref[0] = o_ref[0] * tv_ref[0]


def _expert_ffn(xs, w_gate, w_up, w_down, tv, layer):
    e, r, d = xs.shape
    f = w_gate.shape[3]
    tf = MOE_TF
    rows = pl.BlockSpec((1, r, d), lambda ei, j: (ei, 0, 0))
    return pl.pallas_call(
        _expert_ffn_kernel,
        out_shape=jax.ShapeDtypeStruct((e, r, d), F32),
        grid=(e, f // tf),
        in_specs=[rows,
                  pl.BlockSpec((1, 1, d, tf), lambda ei, j: (layer, ei, 0, j)),
                  pl.BlockSpec((1, 1, d, tf), lambda ei, j: (layer, ei, 0, j)),
                  pl.BlockSpec((1, 1, tf, d), lambda ei, j: (layer, ei, j, 0)),
                  pl.BlockSpec((1, r, 1), lambda ei, j: (ei, 0, 0))],
        out_specs=rows,
        compiler_params=_cparams(("parallel", "arbitrary"), VMEM_LIMIT),
        name="expert_ffn",
    )(xs, w_gate, w_up, w_down, tv)


def _scatter_kernel(tr, idx_ref, y_ref, acc_in, acc_out, buf, sem_in, sem_out):
    del acc_in
    base = (pl.program_id(0) * pl.num_programs(1) + pl.program_id(1)) * tr

    _for_each_row(tr, lambda i: pltpu.make_async_copy(
        acc_out.at[pl.ds(idx_ref[base + i], 1)], buf.at[pl.ds(i, 1)], sem_in).start())
    pltpu.make_async_copy(acc_out.at[pl.ds(0, tr)], buf, sem_in).wait()
    for j in range(buf.shape[1]):
        buf[:, j, :] = buf[:, j, :] + y_ref[0, :, j * LANES:(j + 1) * LANES]
    _for_each_row(tr, lambda i: pltpu.make_async_copy(
        buf.at[pl.ds(i, 1)], acc_out.at[pl.ds(idx_ref[base + i], 1)], sem_out).start())
    pltpu.make_async_copy(buf, acc_out.at[pl.ds(0, tr)], sem_out).wait()


def _scatter_add_rows(y, idx, n_rows):
    e, r, d = y.shape
    tr = _row_tile(r)
    acc0 = jnp.zeros((n_rows, d // LANES, LANES), F32)
    return pl.pallas_call(
        functools.partial(_scatter_kernel, tr),
        out_shape=jax.ShapeDtypeStruct(acc0.shape, F32),
        grid_spec=pltpu.PrefetchScalarGridSpec(
            num_scalar_prefetch=1, grid=(e, r // tr),
            in_specs=[pl.BlockSpec((1, tr, d), lambda ei, j, idx_ref: (ei, j, 0)),
                      pl.BlockSpec(memory_space=pl.ANY)],
            out_specs=pl.BlockSpec(memory_space=pl.ANY),
            scratch_shapes=[pltpu.VMEM((tr, d // LANES, LANES), F32), pltpu.SemaphoreType.DMA(()),
                            pltpu.SemaphoreType.DMA(())]),
        input_output_aliases={2: 0},
        compiler_params=_cparams(("arbitrary", "arbitrary")),
        name="moe_scatter_add",
    )(idx.reshape(-1), y, acc0)


def _moe(x, mod, router_w, router_b, w_gate, w_up, w_down, layer):
    b, na, d = x.shape
    n_lat = na - CTX_LEN
    cap_c = EC_CAPACITY_FACTOR * CTX_LEN // N_EXPERTS
    cap_l = EC_CAPACITY_FACTOR * n_lat // N_EXPERTS
    h, aff = _router(x, mod, router_w, router_b)
    sel, _ = _select(aff, ((0, CTX_LEN, cap_c), (CTX_LEN, na, cap_l)))
    idx_c = jnp.argsort(1 - sel[:, :, :CTX_LEN], axis=-1, stable=True)[..., :cap_c]
    idx_l = jnp.argsort(1 - sel[:, :, CTX_LEN:], axis=-1, stable=True)[..., :cap_l] + CTX_LEN
    idx = jnp.concatenate([idx_l, idx_c], axis=-1).astype(jnp.int32)
    tv = jnp.take_along_axis(aff, idx, axis=-1)
    flat = idx + (jnp.arange(b, dtype=jnp.int32) * na)[:, None, None]
    flat = jnp.transpose(flat, (1, 0, 2)).reshape(N_EXPERTS, -1)
    tv = jnp.transpose(tv, (1, 0, 2)).reshape(N_EXPERTS, -1, 1)
    xs = _gather_rows(h.reshape(b * na, d // LANES, LANES), flat)
    y = _expert_ffn(xs, w_gate, w_up, w_down, tv, layer)
    return _scatter_add_rows(y, flat, b * na).reshape(b, na, d // LANES, LANES)


def _dwconv_kernel(silu, x_ref, w_ref, b_ref, o_ref):
    x = x_ref[0]
    na = x.shape[0]
    row = lax.broadcasted_iota(jnp.int32, x.shape, 0)
    first = (row == 0) | (row == CTX_LEN)
    last = (row == CTX_LEN - 1) | (row == na - 1)
    prev = jnp.where(first, 0.0, pltpu.roll(x, 1, axis=0))
    nxt = jnp.where(last, 0.0, pltpu.roll(x, na - 1, axis=0))
    y = w_ref[0:1, :] * prev + w_ref[1:2, :] * x + w_ref[2:3, :] * nxt + b_ref[...]
    if silu:
        y = y * jax.nn.sigmoid(y)
    o_ref[0, 0] = y


def _segment_dwconv(p, w, bias, groups, silu):
    b, na, c = p.shape
    per = c // groups // LANES
    return pl.pallas_call(
        functools.partial(_dwconv_kernel, silu),
        out_shape=jax.ShapeDtypeStruct((groups, b, na, c // groups), F32),
        grid=(b, c // LANES),
        in_specs=[pl.BlockSpec((1, na, LANES), lambda bi, j: (bi, 0, j)),
                  pl.BlockSpec((SHORT_CONV, LANES), lambda bi, j: (0, j)),
                  pl.BlockSpec((1, LANES), lambda bi, j: (0, j))],
        out_specs=pl.BlockSpec((1, 1, na, LANES), lambda bi, j: (j // per, bi, 0, j % per)),
        compiler_params=_cparams(("parallel", "parallel"), VMEM_LIMIT),
        name="short_conv",
    )(p, w, bias.reshape(1, c))


def _rms(x, w):
    return x * lax.rsqrt(jnp.mean(x * x, axis=-1, keepdims=True) + RMS_EPS) * w


def _rope_tables(n_lat):
    rows = n_lat // GRID_W
    row = jnp.repeat(jnp.arange(rows, dtype=F32), GRID_W)
    col = (jnp.arange(n_lat) % GRID_W).astype(F32)
    nf = ATTN_HEAD_DIM // 4
    inv = ROPE_THETA ** (-jnp.arange(nf, dtype=F32) / nf)
    ar = row[:, None] * inv
    ac = col[:, None] * inv
    ang = jnp.concatenate([ar, ar, ac, ac], axis=-1)
    return jnp.cos(ang), jnp.sin(ang)


def _rope(x, cos, sin):
    nf = ATTN_HEAD_DIM // 4
    xr = x.reshape(x.shape[:-1] + (2, 2, nf))
    rot = jnp.stack([-xr[..., 1, :], xr[..., 0, :]], axis=-2).reshape(x.shape)
    return x * cos + rot * sin


def _mixing(x, mod, w_in, mlstm_conv_w, mlstm_conv_b, mlstm_gate_b, mlstm_norm_w, attn_q_norm_w,
            attn_k_norm_w, hyena_conv_w, hyena_conv_b, hyena_filter, hyena_skip, with_ctx_out):
    b, na, _ = x.shape
    n_lat = na - CTX_LEN
    mqk, mv, mo, aq, ak, av, hy, mg = _in_projection(x, mod, w_in)
    mg = mg[..., :MLSTM_GATES]

    qk = _segment_dwconv(mqk, mlstm_conv_w, mlstm_conv_b, groups=1, silu=True)[0]
    g = mg.reshape(b, na, N_DIR, 2, MLSTM_HEADS) + mlstm_gate_b
    g = jnp.stack([g[:, :, :, 0], jax.nn.log_sigmoid(g[:, :, :, 1])], axis=3)
    gcols = jnp.transpose(g, (2, 0, 1, 3, 4)).reshape(N_DIR, b, na, 2 * MLSTM_HEADS)
    grows = jnp.transpose(gcols, (0, 1, 3, 2))
    hm = _mlstm_scan(qk, mv, grows, gcols)

    scale = ATTN_HEAD_DIM ** -0.5
    cos, sin = _rope_tables(n_lat)
    q = _rms(aq.reshape(b, na, ATTN_KV_HEADS, ATTN_GROUP, ATTN_HEAD_DIM), attn_q_norm_w) * scale
    k = _rms(ak.reshape(b, na, ATTN_KV_HEADS, ATTN_HEAD_DIM), attn_k_norm_w)
    v = av.reshape(b, na, ATTN_KV_HEADS, ATTN_HEAD_DIM)
    q_lat = _rope(q[:, CTX_LEN:], cos[:, None, None, :], sin[:, None, None, :])
    k_all = jnp.concatenate([k[:, :CTX_LEN], _rope(k[:, CTX_LEN:], cos[:, None, :], sin[:, None, :])], axis=1)
    k_all = jnp.transpose(k_all, (0, 2, 1, 3))
    v_all = jnp.transpose(v, (0, 2, 1, 3))

    def group_rows(qh):
        return jnp.transpose(qh, (0, 2, 3, 1, 4)).reshape(b, ATTN_KV_HEADS, -1, ATTN_HEAD_DIM)

    def token_rows(o):
        n = o.shape[2] // ATTN_GROUP
        return jnp.transpose(o.reshape(b, ATTN_KV_HEADS, ATTN_GROUP, n, ATTN_HEAD_DIM),
                             (0, 3, 1, 2, 4)).reshape(b, n, D_ATTN)

    ya = token_rows(_attention(group_rows(q_lat), k_all, v_all))
    if with_ctx_out:
        ya_c = token_rows(_attention(group_rows(q[:, :CTX_LEN]), k_all[:, :, :CTX_LEN], v_all[:, :, :CTX_LEN]))
    else:
        ya_c = jnp.zeros((b, CTX_LEN, D_ATTN), F32)
    ya = jnp.concatenate([ya_c, ya], axis=1)

    hv, hx1, hx2 = _segment_dwconv(hy, hyena_conv_w, hyena_conv_b, groups=3, silu=False)
    yh = _hyena_latent(hv[:, CTX_LEN:], hx1[:, CTX_LEN:], hx2[:, CTX_LEN:],
                       *_hyena_taps(n_lat, *hyena_filter), hyena_skip)
    if with_ctx_out:
        yh_c = _hyena_context(hv[:, :CTX_LEN], hx1[:, :CTX_LEN], hx2[:, :CTX_LEN],
                              *_hyena_taps(CTX_LEN, *hyena_filter), hyena_skip)
    else:
        yh_c = jnp.zeros((b, CTX_LEN, D_HYENA), F32)
    yh = jnp.concatenate([yh_c, yh], axis=1)
    return hm, mo, ya, yh


def kernel(x, c, ctx, c_ctx, w_mod, b_mod, w_in, mlstm_conv_w, mlstm_conv_b, mlstm_gate_b, mlstm_norm_w, attn_q_norm_w, attn_k_norm_w, hyena_conv_w, hyena_conv_b, hyena_f_w1, hyena_f_b1, hyena_f_freq, hyena_f_w2, hyena_f_b2, hyena_f_w3, hyena_skip, w_out, ln_mix_w, ln_mix_b, router_w, router_b, exp_w_gate, exp_w_up, exp_w_down, ln_ffn_w, ln_ffn_b):
    bsz, seq, d = x.shape
    assert d == D_MODEL and ctx.shape[1] == CTX_LEN == TOKEN_TILE and seq == FFT_R * FFT_R // 2
    alpha = (2.0 * DEPTH) ** 0.25
    xa = jnp.concatenate([ctx, x], axis=1)
    crows = jnp.concatenate([c, jnp.broadcast_to(c_ctx, (8 - bsz, d))], axis=0)
    for l in range(DEPTH):
        last = l == DEPTH - 1
        m = _modulation(crows, w_mod, b_mod[l], l).reshape(8, N_MOD, d)
        mod = jnp.stack([jnp.broadcast_to(m[bsz], (bsz, N_MOD, d)), m[:bsz]], axis=1)
        hm, mo, ya, yh = _mixing(
            xa, mod, w_in[l], mlstm_conv_w[l], mlstm_conv_b[l], mlstm_gate_b[l], mlstm_norm_w[l],
            attn_q_norm_w[l], attn_k_norm_w[l], hyena_conv_w[l], hyena_conv_b[l],
            (hyena_f_w1[l], hyena_f_b1[l], hyena_f_freq[l], hyena_f_w2[l], hyena_f_b2[l], hyena_f_w3[l]),
            hyena_skip[l], not last)
        xa = _out_projection(hm, mo, mlstm_norm_w[l], ya, yh, xa, mod, w_out[l].astype(BF16),
                             ln_mix_w[l], ln_mix_b[l], alpha)
        f = _moe(xa, mod, router_w[l], router_b[l], exp_w_gate, exp_w_up, exp_w_down, l)
        xa = _ffn_residual_ln(f, xa, mod, ln_ffn_w[l], ln_ffn_b[l], alpha)
    return xa[:, CTX_LEN:]
```

```python
import functools
import math

import numpy as np
import jax
import jax.numpy as jnp
from jax import lax
from jax.experimental import pallas as pl
from jax.experimental.pallas import tpu as pltpu

F32 = jnp.float32
BF16 = jnp.bfloat16
HI = lax.Precision.HIGHEST

D_MODEL = 1024
DEPTH = 2
GRID_W = 64
CTX_LEN = 256
N_DIR = 2
SHORT_CONV = 3

D_MLSTM = 256
MLSTM_HEAD_DIM = 64
MLSTM_HEADS = 4
MLSTM_GATES = 16
MLSTM_CHUNK = 128

D_ATTN = 512
ATTN_HEAD_DIM = 64
ATTN_HEADS = 8
ATTN_KV_HEADS = 2
ATTN_GROUP = 4
D_KV = 128
ROPE_THETA = 10000.0

D_HYENA = 256
HYENA_ORDER = 2
HYENA_BANDS = 16
HYENA_FAST_DECAY = 0.3
HYENA_SLOW_DECAY = 1.5
HYENA_DECAY_TARGET = 1e-2
HYENA_WINDOW_SHIFT = 0.05

N_IN = 4 * D_MLSTM + MLSTM_GATES + D_ATTN + 2 * D_KV + 3 * D_HYENA
N_EXPERTS = 16
EC_CAPACITY_FACTOR = 2
D_FF_EXPERT = 2816
N_MOD = 6
LN_EPS = 1e-5
RMS_EPS = 1e-6

LANES = 128
TOKEN_TILE = 256
FFT_R = 128
FFT_GROUP = 8
VMEM_LIMIT = 56 * 1024 * 1024


def _cparams(sem, vmem=None):
    return pltpu.CompilerParams(dimension_semantics=sem, vmem_limit_bytes=vmem)


def _mod_kernel(c_ref, w_ref, b_ref, o_ref):
    cs = c_ref[...]
    cs = cs * jax.nn.sigmoid(cs)
    o_ref[...] = jnp.dot(cs, w_ref[0], precision=HI, preferred_element_type=F32) + b_ref[...]


def _modulation(crows, w_mod, b_mod, layer):
    rows, d = crows.shape
    n = w_mod.shape[2]
    tn = 1024
    return pl.pallas_call(
        _mod_kernel,
        out_shape=jax.ShapeDtypeStruct((rows, n), F32),
        grid=(n // tn,),
        in_specs=[pl.BlockSpec((rows, d), lambda j: (0, 0)),
                  pl.BlockSpec((1, d, tn), lambda j: (layer, 0, j)),
                  pl.BlockSpec((1, tn), lambda j: (0, j))],
        out_specs=pl.BlockSpec((rows, tn), lambda j: (0, j)),
        compiler_params=_cparams(("parallel",)),
        name="adaln_mod",
    )(crows, w_mod, b_mod.reshape(1, n))


def _ln_rows(x):
    mu = jnp.mean(x, axis=-1, keepdims=True)
    xc = x - mu
    var = jnp.mean(xc * xc, axis=-1, keepdims=True)
    return xc * lax.rsqrt(var + LN_EPS)


def _mod_spec():
    return pl.BlockSpec((1, 1, N_MOD, D_MODEL), lambda b, i: (b, jnp.minimum(i, 1), 0, 0))


HEAD_DIM = 64
_C_MG = 4 * D_MLSTM
_C_AQ = _C_MG + MLSTM_GATES
IN_GROUPS = (("mqk", 0, 2 * D_MLSTM, True), ("mv", 2 * D_MLSTM, D_MLSTM, True), ("mo", 3 * D_MLSTM, D_MLSTM, False),
             ("aq", _C_AQ, D_ATTN, True), ("ak", _C_AQ + D_ATTN, D_KV, True), ("av", _C_AQ + D_ATTN + D_KV, D_KV, True),
             ("hy", _C_AQ + D_ATTN + 2 * D_KV, 3 * D_HYENA, False), ("mg", _C_MG, MLSTM_GATES, False))


def _group_width(src_width, head_padded):
    return src_width // HEAD_DIM * LANES if head_padded else -(-src_width // LANES) * LANES


def _pad_heads(a, axis=-1):
    axis = axis % a.ndim
    nh = a.shape[axis] // HEAD_DIM
    a = a.reshape(a.shape[:axis] + (nh, HEAD_DIM) + a.shape[axis + 1:])
    pad = [(0, 0)] * a.ndim
    pad[axis + 1] = (0, LANES - HEAD_DIM)
    a = jnp.pad(a, pad)
    return a.reshape(a.shape[:axis] + (nh * LANES,) + a.shape[axis + 2:])


def _inproj_kernel(x_ref, mod_ref, w_ref, *o_refs):
    y = _ln_rows(x_ref[0])
    h = (y * (1.0 + mod_ref[0, 0, 1:2, :]) + mod_ref[0, 0, 0:1, :]).astype(BF16)
    off = 0
    for o_ref in o_refs:
        width = o_ref.shape[2]
        o_ref[0] = jnp.dot(h, w_ref[:, off:off + width], preferred_element_type=F32)
        off += width


def _in_projection(x, mod, w_in):
    b, na, d = x.shape
    cols = []
    for _, start, src, head_padded in IN_GROUPS:
        wg = w_in[:, start:start + src]
        if head_padded:
            wg = _pad_heads(wg)
        elif src % LANES:
            wg = jnp.pad(wg, ((0, 0), (0, LANES - src % LANES)))
        cols.append(wg)
    w = jnp.concatenate(cols, axis=1).astype(BF16)
    n = w.shape[1]
    widths = [_group_width(src, hp) for _, _, src, hp in IN_GROUPS]
    return pl.pallas_call(
        _inproj_kernel,
        out_shape=[jax.ShapeDtypeStruct((b, na, width), F32) for width in widths],
        grid=(b, na // TOKEN_TILE),
        in_specs=[pl.BlockSpec((1, TOKEN_TILE, d), lambda bi, i: (bi, i, 0)),
                  _mod_spec(),
                  pl.BlockSpec((d, n), lambda bi, i: (0, 0))],
        out_specs=[pl.BlockSpec((1, TOKEN_TILE, width), lambda bi, i: (bi, i, 0)) for width in widths],
        compiler_params=_cparams(("parallel", "parallel"), VMEM_LIMIT),
        name="in_proj",
    )(x, mod, w)


def _compact_heads(x):
    tiles = [x[:, j * LANES:(j + 1) * LANES] for j in range(x.shape[1] // LANES)]
    return jnp.concatenate([tiles[j] + pltpu.roll(tiles[j + 1], HEAD_DIM, axis=1) for j in range(0, len(tiles), 2)],
                           axis=1)


def _outproj_kernel(alpha, hm0_ref, hm1_ref, mo_ref, nw_ref, seg_ref, ya_ref, yh_ref, x_ref, mod_ref, w_ref,
                    lw_ref, lb_ref, o_ref):
    hm = _compact_heads(hm0_ref[0, 0] + hm1_ref[0, 0])
    ss = jnp.dot(hm * hm, seg_ref[...], precision=HI, preferred_element_type=F32)
    hn = hm * lax.rsqrt(ss * (1.0 / MLSTM_HEAD_DIM) + RMS_EPS) * nw_ref[...]
    ym = jax.nn.sigmoid(mo_ref[0]) * hn
    mix = jnp.dot(ym.astype(BF16), w_ref[0:D_MLSTM, :], preferred_element_type=F32)
    mix = mix + jnp.dot(_compact_heads(ya_ref[0]).astype(BF16), w_ref[D_MLSTM:D_MLSTM + D_ATTN, :],
                        preferred_element_type=F32)
    mix = mix + jnp.dot(yh_ref[0].astype(BF16), w_ref[D_MLSTM + D_ATTN:, :], preferred_element_type=F32)
    r = alpha * x_ref[0] + mod_ref[0, 0, 2:3, :] * mix
    o_ref[0] = _ln_rows(r) * lw_ref[...] + lb_ref[...]


def _out_projection(hm, mo, norm_w, ya, yh, x, mod, w_bf16, ln_w, ln_b, alpha):
    b, na, d = x.shape
    dm = mo.shape[2]
    head = np.arange(dm) // MLSTM_HEAD_DIM
    seg = jnp.asarray((head[:, None] == head[None, :]).astype(np.float32))

    def tile(width):
        return pl.BlockSpec((1, TOKEN_TILE, width), lambda bi, i: (bi, i, 0))

    def hm_spec(direction):
        return pl.BlockSpec((1, 1, TOKEN_TILE, hm.shape[3]), lambda bi, i: (direction, bi, i, 0))

    vec = pl.BlockSpec((1, d), lambda bi, i: (0, 0))
    return pl.pallas_call(
        functools.partial(_outproj_kernel, alpha),
        out_shape=jax.ShapeDtypeStruct((b, na, d), F32),
        grid=(b, na // TOKEN_TILE),
        in_specs=[hm_spec(0), hm_spec(1), tile(dm), pl.BlockSpec((1, dm), lambda bi, i: (0, 0)),
                  pl.BlockSpec((dm, dm), lambda bi, i: (0, 0)), tile(ya.shape[2]), tile(yh.shape[2]), tile(d),
                  _mod_spec(), pl.BlockSpec((d, d), lambda bi, i: (0, 0)), vec, vec],
        out_specs=tile(d),
        compiler_params=_cparams(("parallel", "parallel"), VMEM_LIMIT),
        name="out_proj_ln",
    )(hm, hm, mo, norm_w.reshape(1, dm), seg, ya, yh, x, mod, w_bf16, ln_w.reshape(1, d), ln_b.reshape(1, d))


def _resid_ln_kernel(alpha, f_ref, x_ref, mod_ref, lw_ref, lb_ref, o_ref):
    f = jnp.concatenate([f_ref[0, :, j, :] for j in range(f_ref.shape[2])], axis=1)
    r = alpha * x_ref[0] + mod_ref[0, 0, 5:6, :] * f
    o_ref[0] = _ln_rows(r) * lw_ref[...] + lb_ref[...]


def _ffn_residual_ln(f, x, mod, ln_w, ln_b, alpha):
    b, na, d = x.shape
    tile = pl.BlockSpec((1, TOKEN_TILE, d), lambda bi, i: (bi, i, 0))
    vec = pl.BlockSpec((1, d), lambda bi, i: (0, 0))
    return pl.pallas_call(
        functools.partial(_resid_ln_kernel, alpha),
        out_shape=jax.ShapeDtypeStruct((b, na, d), F32),
        grid=(b, na // TOKEN_TILE),
        in_specs=[pl.BlockSpec((1, TOKEN_TILE, d // LANES, LANES), lambda bi, i: (bi, i, 0, 0)),
                  tile, _mod_spec(), vec, vec],
        out_specs=tile,
        compiler_params=_cparams(("parallel", "parallel")),
        name="ffn_resid_ln",
    )(f, x, mod, ln_w.reshape(1, d), ln_b.reshape(1, d))


def _mlstm_kernel(qk_ref, v_ref, gr_ref, gc_ref, o_ref, c_sc, n_sc, m_sc):
    t = MLSTM_CHUNK
    nh = MLSTM_HEADS
    d = pl.program_id(0)

    @pl.when(pl.program_id(2) == 0)
    def _():
        c_sc[...] = jnp.zeros_like(c_sc)
        n_sc[...] = jnp.zeros_like(n_sc)
        m_sc[...] = jnp.zeros_like(m_sc)

    row = lax.broadcasted_iota(jnp.int32, (t, t), 0)
    col = lax.broadcasted_iota(jnp.int32, (t, t), 1)
    mask = jnp.where(d == 0, col - row, row - col) <= 0
    maskf = mask.astype(F32)
    grows = gr_ref[0, 0]
    gcols = gc_ref[0, 0]
    cum_cols = jnp.dot(maskf, gcols, precision=HI, preferred_element_type=F32)
    cum_rows = lax.dot_general(grows, maskf, (((1,), (1,)), ((), ())), precision=HI,
                               preferred_element_type=F32)
    lane8 = lax.broadcasted_iota(jnp.int32, (t, 2 * nh), 1)

    def column(a, idx):
        return jnp.sum(jnp.where(lane8 == idx, a, 0.0), axis=1, keepdims=True)

    dh = LANES
    for h in range(nh):
        qc = qk_ref[0, :, h * dh:(h + 1) * dh] * (MLSTM_HEAD_DIM ** -0.5)
        kc = qk_ref[0, :, (nh + h) * dh:(nh + h + 1) * dh]
        vc = v_ref[0, :, h * dh:(h + 1) * dh]
        ic_row = grows[h:h + 1, :]
        ic_col = column(gcols, h)
        bcol = column(cum_cols, nh + h)
        brow = cum_rows[nh + h:nh + h + 1, :]
        tot = jnp.sum(grows[nh + h:nh + h + 1, :], axis=1, keepdims=True)
        m0 = m_sc[h:h + 1, 0:1]
        log_inter = bcol + m0
        log_intra = jnp.where(mask, bcol - brow + ic_row, -jnp.inf)
        mrow = jnp.maximum(log_inter, jnp.max(log_intra, axis=1, keepdims=True))
        w_inter = jnp.exp(log_inter - mrow)
        qb = qc.astype(BF16)
        vb = vc.astype(BF16)
        scores = lax.dot_general(qb, kc.astype(BF16), (((1,), (1,)), ((), ())),
                                 preferred_element_type=F32) * jnp.exp(log_intra - mrow)
        ct = c_sc[h]
        n0 = n_sc[h]
        num = (w_inter * jnp.dot(qb, ct.astype(BF16), preferred_element_type=F32)
               + jnp.dot(scores.astype(BF16), vb, preferred_element_type=F32))
        den = w_inter * jnp.sum(qc * n0, axis=1, keepdims=True) + jnp.sum(scores, axis=1, keepdims=True)
        o_ref[0, 0, :, h * dh:(h + 1) * dh] = num / jnp.maximum(jnp.abs(den), jnp.exp(-mrow))
        log_src = tot - bcol + ic_col
        m_new = jnp.maximum(tot + m0, jnp.max(log_src, axis=0, keepdims=True))
        wk = jnp.exp(log_src - m_new) * kc
        decay = jnp.exp(tot + m0 - m_new)
        c_sc[h] = decay * ct + lax.dot_general(wk.astype(BF16), vb, (((0,), (0,)), ((), ())),
                                               preferred_element_type=F32)
        n_sc[h] = decay * n0 + jnp.sum(wk, axis=0, keepdims=True)
        m_sc[h:h + 1, :] = jnp.broadcast_to(m_new, (1, 128))


def _mlstm_scan(qk, v, grows, gcols):
    b, na, dv = v.shape
    nh, dh = MLSTM_HEADS, LANES
    t = MLSTM_CHUNK
    nc = na // t
    nctx = CTX_LEN // t

    def chunk(d, c):
        rev = jnp.where(c < nctx, nctx - 1 - c, nc + nctx - 1 - c)
        return jnp.where(d == 0, c, rev)

    return pl.pallas_call(
        _mlstm_kernel,
        out_shape=jax.ShapeDtypeStruct((N_DIR, b, na, dv), F32),
        grid=(N_DIR, b, nc),
        in_specs=[pl.BlockSpec((1, t, 2 * dv), lambda d, bi, c: (bi, chunk(d, c), 0)),
                  pl.BlockSpec((1, t, dv), lambda d, bi, c: (bi, chunk(d, c), 0)),
                  pl.BlockSpec((1, 1, 2 * nh, t), lambda d, bi, c: (d, bi, 0, chunk(d, c))),
                  pl.BlockSpec((1, 1, t, 2 * nh), lambda d, bi, c: (d, bi, chunk(d, c), 0))],
        out_specs=pl.BlockSpec((1, 1, t, dv), lambda d, bi, c: (d, bi, chunk(d, c), 0)),
        scratch_shapes=[pltpu.VMEM((nh, dh, dh), F32), pltpu.VMEM((nh, 1, dh), F32), pltpu.VMEM((8, 128), F32)],
        compiler_params=_cparams(("parallel", "parallel", "arbitrary")),
        name="mlstm_scan",
    )(qk, v, grows, gcols)


ATTN_TQ = 256
LOG2E = 1.4426950408889634


def _qkv_prep_kernel(aq_ref, ak_ref, av_ref, cos_ref, sin_ref, qw_ref, kw_ref, q_ref, k_ref, v_ref):
    lane = lax.broadcasted_iota(jnp.int32, (aq_ref.shape[1], LANES), 1)
    head_lane = lane < ATTN_HEAD_DIM
    first_half = (lane % (ATTN_HEAD_DIM // 2)) < (ATTN_HEAD_DIM // 4)
    cos = cos_ref[...]
    sin = sin_ref[...]
    quarter = ATTN_HEAD_DIM // 4

    def norm_rope(x, w):
        y = x * lax.rsqrt(jnp.sum(x * x, axis=1, keepdims=True) * (1.0 / ATTN_HEAD_DIM) + RMS_EPS) * w
        rot = jnp.where(first_half, -pltpu.roll(y, LANES - quarter, axis=1), pltpu.roll(y, quarter, axis=1))
        return jnp.where(head_lane, y * cos + rot * sin, 0.0)

    for h in range(ATTN_HEADS):
        q_ref[0, h] = norm_rope(aq_ref[0, :, h * LANES:(h + 1) * LANES], qw_ref[...]).astype(BF16)
    for h in range(ATTN_KV_HEADS):
        k_ref[0, h] = norm_rope(ak_ref[0, :, h * LANES:(h + 1) * LANES], kw_ref[...]).astype(BF16)
        v_ref[0, h] = jnp.where(lane == ATTN_HEAD_DIM, 1.0, av_ref[0, :, h * LANES:(h + 1) * LANES]).astype(BF16)


def _qkv_prep(aq, ak, av, q_norm_w, k_norm_w):
    b, na, _ = aq.shape
    cos, sin = _rope_tables(na - CTX_LEN)
    pad = ((CTX_LEN, 0), (0, LANES - ATTN_HEAD_DIM))
    cos = jnp.pad(cos, pad, constant_values=1.0)
    sin = jnp.pad(sin, pad)
    qw = jnp.pad(q_norm_w * (ATTN_HEAD_DIM ** -0.5 * LOG2E), (0, LANES - ATTN_HEAD_DIM)).reshape(1, LANES)
    kw = jnp.pad(k_norm_w, (0, LANES - ATTN_HEAD_DIM)).reshape(1, LANES)
    tok = lambda width: pl.BlockSpec((1, TOKEN_TILE, width), lambda bi, i: (bi, i, 0))
    tab = pl.BlockSpec((TOKEN_TILE, LANES), lambda bi, i: (i, 0))
    vec = pl.BlockSpec((1, LANES), lambda bi, i: (0, 0))
    heads = lambda nh: pl.BlockSpec((1, nh, TOKEN_TILE, LANES), lambda bi, i: (bi, 0, i, 0))
    return pl.pallas_call(
        _qkv_prep_kernel,
        out_shape=(jax.ShapeDtypeStruct((b, ATTN_HEADS, na, LANES), BF16),
                   jax.ShapeDtypeStruct((b, ATTN_KV_HEADS, na, LANES), BF16),
                   jax.ShapeDtypeStruct((b, ATTN_KV_HEADS, na, LANES), BF16)),
        grid=(b, na // TOKEN_TILE),
        in_specs=[tok(aq.shape[2]), tok(ak.shape[2]), tok(av.shape[2]), tab, tab, vec, vec],
        out_specs=(heads(ATTN_HEADS), heads(ATTN_KV_HEADS), heads(ATTN_KV_HEADS)),
        compiler_params=_cparams(("parallel", "parallel")),
        name="qkv_prep",
    )(aq, ak, av, cos, sin, qw, kw)


def _attn_kernel(q_ref, k_ref, v_ref, o_ref, s_ref, p_ref):
    def attend(nk):
        s_ref[:, 0:nk] = lax.dot_general(q_ref[0, 0], k_ref[0, 0, 0:nk, :], (((1,), (1,)), ((), ())),
                                         preferred_element_type=F32)
        m = jnp.max(s_ref[:, 0:nk], axis=1, keepdims=True)
        p_ref[:, 0:nk] = jnp.exp2(s_ref[:, 0:nk] - m).astype(BF16)
        acc = jnp.dot(p_ref[:, 0:nk], v_ref[0, 0, 0:nk, :], preferred_element_type=F32)
        lane = lax.broadcasted_iota(jnp.int32, acc.shape, 1)
        o_ref[0] = jnp.where(lane < ATTN_HEAD_DIM, acc / acc[:, ATTN_HEAD_DIM:ATTN_HEAD_DIM + 1], 0.0)

    is_ctx = pl.program_id(2) < CTX_LEN // ATTN_TQ
    pl.when(is_ctx)(lambda: attend(CTX_LEN))
    pl.when(jnp.logical_not(is_ctx))(lambda: attend(k_ref.shape[2]))


def _attention(q, k, v):
    b, nh, na, _ = q.shape
    group = nh // k.shape[1]
    kv_spec = pl.BlockSpec((1, 1, na, LANES), lambda bi, h, i: (bi, h // group, 0, 0))
    return pl.pallas_call(
        _attn_kernel,
        out_shape=jax.ShapeDtypeStruct((b, na, nh * LANES), F32),
        grid=(b, nh, na // ATTN_TQ),
        in_specs=[pl.BlockSpec((1, 1, ATTN_TQ, LANES), lambda bi, h, i: (bi, h, i, 0)), kv_spec, kv_spec],
        out_specs=pl.BlockSpec((1, ATTN_TQ, LANES), lambda bi, h, i: (bi, i, h)),
        scratch_shapes=[pltpu.VMEM((ATTN_TQ, na), F32), pltpu.VMEM((ATTN_TQ, na), BF16)],
        compiler_params=_cparams(("parallel", "parallel", "parallel"), VMEM_LIMIT),
        name="attention",
    )(q, k, v)


def _dft_tables(t2_len):
    r = FFT_R
    n = r * r
    idx = np.arange(r, dtype=np.float64)
    kb = idx[None, :, None]
    t1 = idx[:, None, None]
    t2 = np.arange(t2_len, dtype=np.float64)[None, None, :]
    ang = -2.0 * np.pi * (t2 * kb / r + t1 * kb / n)
    f1 = np.concatenate([np.cos(ang), np.sin(ang)], axis=1)
    ang2 = -2.0 * np.pi * np.outer(idx, idx) / r
    f2 = np.stack([np.cos(ang2), np.sin(ang2)])
    return f1.astype(np.float32), f2.astype(np.float32)


def _idft_table(t2_len):
    r = FFT_R
    n = r * r
    t1 = np.arange(r, dtype=np.float64)[:, None, None]
    t2 = np.arange(t2_len, dtype=np.float64)[None, :, None]
    kb = np.arange(r, dtype=np.float64)[None, None, :]
    ang = 2.0 * np.pi * (t2 * kb / r + t1 * kb / n)
    return (np.stack([np.cos(ang), np.sin(ang)], axis=1) / n).astype(np.float32)


def _dot_hi(table, x):
    return jnp.dot(table, x.astype(BF16), preferred_element_type=F32)


def _fft1_kernel(f_ref, z_ref, g_ref):
    for i in range(FFT_GROUP):
        g_ref[0, i] = _dot_hi(f_ref[i], z_ref[0, :, i, :])


def _fft_stage1(z4, f1):
    b, t2_len, r, c = z4.shape
    g = FFT_GROUP
    return pl.pallas_call(
        _fft1_kernel,
        out_shape=jax.ShapeDtypeStruct((b, r, 2 * r, c), F32),
        grid=(b, r // g),
        in_specs=[pl.BlockSpec((g, 2 * r, t2_len), lambda bi, j: (j, 0, 0)),
                  pl.BlockSpec((1, t2_len, g, c), lambda bi, j: (bi, 0, j, 0))],
        out_specs=pl.BlockSpec((1, g, 2 * r, c), lambda bi, j: (bi, j, 0, 0)),
        compiler_params=_cparams(("parallel", "parallel")),
        name="fft_stage1",
    )(f1, z4)


def _fft2_kernel(with_filter, f_ref, gr_ref, gi_ref, aux_ref, h_ref):
    fr = f_ref[0]
    fi = f_ref[1]
    for i in range(FFT_GROUP):
        gr = gr_ref[0, :, i, :]
        gi = gi_ref[0, :, i, :]
        xr = _dot_hi(fr, gr) - _dot_hi(fi, gi)
        xi = _dot_hi(fr, gi) + _dot_hi(fi, gr)
        if with_filter:
            tr = aux_ref[0, 0, i]
            ti = aux_ref[0, 1, i]
            yr = xr * tr - xi * ti
            yi = xr * ti + xi * tr
            xr = _dot_hi(fr, yr) + _dot_hi(fi, yi)
            xi = _dot_hi(fr, yi) - _dot_hi(fi, yr)
        else:
            xr = xr * aux_ref[0]
            xi = xi * aux_ref[0]
        h_ref[0, 0, i] = xr
        h_ref[0, 1, i] = xi


def _fft_stage2(g, f2, tf=None, scale=None):
    b, r, _, c = g.shape
    grp = FFT_GROUP
    nblk = r // grp
    in_specs = [pl.BlockSpec((2, r, r), lambda bi, j: (0, 0, 0)),
                pl.BlockSpec((1, r, grp, c), lambda bi, j: (bi, 0, j, 0)),
                pl.BlockSpec((1, r, grp, c), lambda bi, j: (bi, 0, nblk + j, 0))]
    args = [f2, g, g]
    if tf is not None:
        in_specs.append(pl.BlockSpec((1, 2, grp, r, c), lambda bi, j: (0, 0, j, 0, 0)))
        args.append(tf)
    else:
        in_specs.append(pl.BlockSpec((1, 1, c), lambda bi, j: (bi, 0, 0)))
        args.append(scale)
    return pl.pallas_call(
        functools.partial(_fft2_kernel, tf is not None),
        out_shape=jax.ShapeDtypeStruct((b, 2, r, r, c), F32),
        grid=(b, nblk),
        in_specs=in_specs,
        out_specs=pl.BlockSpec((1, 2, grp, r, c), lambda bi, j: (bi, 0, j, 0, 0)),
        compiler_params=_cparams(("parallel", "parallel"), VMEM_LIMIT),
        name="fft_stage2",
    )(*args)


def _ifft_kernel(e_ref, h_ref, z_ref, x_ref, skip_ref, o_ref):
    for i in range(FFT_GROUP):
        y = _dot_hi(e_ref[i, 0], h_ref[0, 0, :, i, :]) - _dot_hi(e_ref[i, 1], h_ref[0, 1, :, i, :])
        y = y * (1.0 / (FFT_R * FFT_R))
        o_ref[0, :, i, :] = x_ref[0, :, i, :] * (y + skip_ref[...] * z_ref[0, :, i, :])


def _ifft_gate(h, e, z4, x4, skip):
    b, t2_len, r, c = z4.shape
    g = FFT_GROUP
    tok = pl.BlockSpec((1, t2_len, g, c), lambda bi, j: (bi, 0, j, 0))
    return pl.pallas_call(
        _ifft_kernel,
        out_shape=jax.ShapeDtypeStruct(z4.shape, F32),
        grid=(b, r // g),
        in_specs=[pl.BlockSpec((g, 2, t2_len, r), lambda bi, j: (j, 0, 0, 0)),
                  pl.BlockSpec((1, 2, r, g, c), lambda bi, j: (bi, 0, 0, j, 0)),
                  tok, tok, pl.BlockSpec((1, c), lambda bi, j: (0, 0))],
        out_specs=tok,
        compiler_params=_cparams(("parallel", "parallel"), VMEM_LIMIT),
        name="ifft_gate",
    )(e, h, z4, x4, skip.reshape(1, c))


TAPS_ROWS = 1024
FEAT_PAD = 128


def _taps_kernel(n, feat_ref, w1_ref, b1_ref, freq_ref, w2_ref, b2_ref, w3_ref, delta_ref, taps_ref, l1_ref):
    c = D_HYENA
    rows = feat_ref.shape[0]
    step = pl.program_id(0)

    @pl.when(step == 0)
    def _():
        l1_ref[...] = jnp.zeros_like(l1_ref)

    f = feat_ref[...]
    hid = jnp.sin(freq_ref[0:1, :] * (jnp.dot(f, w1_ref[...], precision=HI, preferred_element_type=F32) + b1_ref[...]))
    hid = jnp.sin(freq_ref[1:2, :] * (jnp.dot(hid, w2_ref[...], precision=HI, preferred_element_type=F32) + b2_ref[...]))
    filt = jnp.dot(hid, w3_ref[...], precision=HI, preferred_element_type=F32)
    window = jnp.exp(-f[:, 0:1] * delta_ref[...]) + HYENA_WINDOW_SHIFT
    i = step * rows + lax.broadcasted_iota(jnp.int32, (rows, c), 0)
    for o in range(HYENA_ORDER):
        fwd = filt[:, o * c:(o + 1) * c]
        bwd = filt[:, (HYENA_ORDER + o) * c:(HYENA_ORDER + o + 1) * c]
        tap = jnp.where(i < n, fwd, jnp.where(i > n, bwd, 0.0)) * window
        taps_ref[o] = tap
        l1_ref[o:o + 1, :] += jnp.sum(jnp.abs(tap), axis=0, keepdims=True)


def _hyena_taps(n, f_w1, f_b1, f_freq, f_w2, f_b2, f_w3):
    i = np.arange(2 * n)
    t = (np.where(i < n, i, 2 * n - i).astype(np.float32) / np.float32(n)).astype(np.float32)
    ang = (np.float32(2.0 * math.pi) * t[:, None]) * np.arange(1, HYENA_BANDS + 1, dtype=np.float32)
    feats = np.zeros((2 * n, FEAT_PAD), np.float32)
    feats[:, 0] = t
    feats[:, 1:1 + HYENA_BANDS] = np.cos(ang.astype(np.float64))
    feats[:, 1 + HYENA_BANDS:1 + 2 * HYENA_BANDS] = np.sin(ang.astype(np.float64))
    log_target = abs(math.log(HYENA_DECAY_TARGET))
    deltas = jnp.linspace(log_target / HYENA_SLOW_DECAY, log_target / HYENA_FAST_DECAY, D_HYENA, dtype=F32)
    hid = f_w2.shape[0]
    w1 = jnp.concatenate([f_w1, jnp.zeros((FEAT_PAD - f_w1.shape[0], hid), F32)], axis=0)
    rows = min(TAPS_ROWS, 2 * n)
    full = lambda shape: pl.BlockSpec(shape, lambda j: (0,) * len(shape))
    return pl.pallas_call(
        functools.partial(_taps_kernel, n),
        out_shape=(jax.ShapeDtypeStruct((HYENA_ORDER, 2 * n, D_HYENA), F32),
                   jax.ShapeDtypeStruct((HYENA_ORDER, D_HYENA), F32)),
        grid=(2 * n // rows,),
        in_specs=[pl.BlockSpec((rows, FEAT_PAD), lambda j: (j, 0)), full((FEAT_PAD, hid)), full((1, hid)),
                  full((2, hid)), full((hid, hid)), full((1, hid)), full(f_w3.shape), full((1, D_HYENA))],
        out_specs=(pl.BlockSpec((HYENA_ORDER, rows, D_HYENA), lambda j: (0, j, 0)), full((HYENA_ORDER, D_HYENA))),
        compiler_params=_cparams(("arbitrary",)),
        name="hyena_taps",
    )(jnp.asarray(feats), w1, f_b1.reshape(1, hid), f_freq, f_w2, f_b2.reshape(1, hid), f_w3, deltas.reshape(1, -1))


def _hyena_latent(v, x1, x2, taps, l1, skip):
    b, n, c = v.shape
    r = FFT_R
    t2_len = n // r
    f1_full, f2 = _dft_tables(r)
    f1_half = jnp.asarray(f1_full[:, :, :t2_len]).astype(BF16)
    f1_full = jnp.asarray(f1_full).astype(BF16)
    f2 = jnp.asarray(f2).astype(BF16)
    e = jnp.asarray(_idft_table(t2_len) * (r * r)).astype(BF16)
    taps4 = taps.reshape(HYENA_ORDER, r, r, c)
    tf = _fft_stage2(_fft_stage1(taps4, f1_full), f2, scale=(1.0 / l1)[:, None, :])
    z = v.reshape(b, t2_len, r, c)
    for o, gate in enumerate((x1, x2)):
        g = _fft_stage1(z, f1_half)
        h = _fft_stage2(g, f2, tf[o:o + 1])
        z = _ifft_gate(h, e, z, gate.reshape(b, t2_len, r, c), skip[o])
    return z.reshape(b, n, c)


def _hyena_ctx_kernel(n, v_ref, x1_ref, x2_ref, k_ref, skip_ref, o_ref, z_sc):
    z_sc[...] = v_ref[0]
    for o, gate_ref in enumerate((x1_ref, x2_ref)):
        def body(s, acc):
            return acc + k_ref[o, pl.ds(n - 1 - s, n), :] * z_sc[pl.ds(s, 1), :]
        conv = lax.fori_loop(0, n, body, jnp.zeros(z_sc.shape, F32))
        z_sc[...] = gate_ref[0] * (conv + skip_ref[o:o + 1, :] * z_sc[...])
    o_ref[0] = z_sc[...]


def _hyena_context(v, x1, x2, taps, l1, skip):
    b, n, c = v.shape
    cb = 128
    k2 = jnp.roll(taps, n - 1, axis=1) / l1[:, None, :]
    tok = pl.BlockSpec((1, n, cb), lambda bi, j: (bi, 0, j))
    return pl.pallas_call(
        functools.partial(_hyena_ctx_kernel, n),
        out_shape=jax.ShapeDtypeStruct((b, n, c), F32),
        grid=(b, c // cb),
        in_specs=[tok, tok, tok,
                  pl.BlockSpec((HYENA_ORDER, 2 * n, cb), lambda bi, j: (0, 0, j)),
                  pl.BlockSpec((HYENA_ORDER, cb), lambda bi, j: (0, j))],
        out_specs=tok,
        scratch_shapes=[pltpu.VMEM((n, cb), F32)],
        compiler_params=_cparams(("parallel", "parallel")),
        name="hyena_ctx",
    )(v, x1, x2, k2, skip)


def _router_kernel(x_ref, mod_ref, rw_ref, rb_ref, h_ref, aff_ref):
    y = _ln_rows(x_ref[0])
    h = y * (1.0 + mod_ref[0, 0, 4:5, :]) + mod_ref[0, 0, 3:4, :]
    for j in range(h_ref.shape[2]):
        h_ref[0, :, j, :] = h[:, j * LANES:(j + 1) * LANES]
    logits = lax.dot_general(rw_ref[...], h, (((1,), (1,)), ((), ())), precision=HI,
                             preferred_element_type=F32) + rb_ref[...]
    z = jnp.exp(logits - jnp.max(logits, axis=0, keepdims=True))
    aff_ref[0] = z / jnp.sum(z, axis=0, keepdims=True)


def _router(x, mod, router_w, router_b):
    b, na, d = x.shape
    e = router_w.shape[1]
    return pl.pallas_call(
        _router_kernel,
        out_shape=(jax.ShapeDtypeStruct((b, na, d // LANES, LANES), F32), jax.ShapeDtypeStruct((b, e, na), F32)),
        grid=(b, na // TOKEN_TILE),
        in_specs=[pl.BlockSpec((1, TOKEN_TILE, d), lambda bi, i: (bi, i, 0)),
                  _mod_spec(),
                  pl.BlockSpec((e, d), lambda bi, i: (0, 0)),
                  pl.BlockSpec((e, 1), lambda bi, i: (0, 0))],
        out_specs=(pl.BlockSpec((1, TOKEN_TILE, d // LANES, LANES), lambda bi, i: (bi, i, 0, 0)),
                   pl.BlockSpec((1, e, TOKEN_TILE), lambda bi, i: (bi, 0, i))),
        compiler_params=_cparams(("parallel", "parallel")),
        name="router",
    )(x, mod, router_w.T, router_b.reshape(e, 1))


def _prefix_count(x):
    n = x.shape[1]
    lane = lax.broadcasted_iota(jnp.int32, x.shape, 1)
    sh = 1
    while sh < n:
        x = x + jnp.where(lane >= sh, pltpu.roll(x, sh, axis=1), 0)
        sh *= 2
    return x


SELECT_BISECTIONS = 160


def _select_kernel(segments, aff_ref, sel_ref, pos_ref):
    nb, ne, _ = aff_ref.shape
    parts = [(b, s0, s1, cap) for b in range(nb) for (s0, s1, cap) in segments]

    def body(_, bounds):
        out = []
        for (b, s0, s1, cap), (lo, hi) in zip(parts, bounds):
            mid = 0.5 * (lo + hi)
            cnt = jnp.sum((aff_ref[b, :, s0:s1] >= mid).astype(jnp.int32), axis=1, keepdims=True)
            ok = cnt >= cap
            out.append((jnp.where(ok, mid, lo), jnp.where(ok, hi, mid)))
        return tuple(out)

    init = tuple((jnp.zeros((ne, 1), F32), jnp.full((ne, 1), 2.0, F32)) for _ in parts)
    bounds = lax.fori_loop(0, SELECT_BISECTIONS, body, init)
    for (b, s0, s1, cap), (lo, hi) in zip(parts, bounds):
        a = aff_ref[b, :, s0:s1]
        above = (a >= hi).astype(jnp.int32)
        tied = jnp.where(a >= lo, 1, 0) - above
        need = cap - jnp.sum(above, axis=1, keepdims=True)
        tie_rank = _prefix_count(tied) - tied
        sel = above + tied * (tie_rank < need).astype(jnp.int32)
        sel_ref[b, :, s0:s1] = sel
        pos_ref[b, :, s0:s1] = _prefix_count(sel) - sel


def _select(aff, segments):
    b, e, na = aff.shape
    blk = pl.BlockSpec((b, e, na), lambda i: (0, 0, 0))
    return pl.pallas_call(
        functools.partial(_select_kernel, segments),
        out_shape=(jax.ShapeDtypeStruct((b, e, na), jnp.int32), jax.ShapeDtypeStruct((b, e, na), jnp.int32)),
        grid=(1,),
        in_specs=[blk],
        out_specs=(blk, blk),
        compiler_params=_cparams(("arbitrary",)),
        name="expert_select",
    )(aff)


MOE_ROW_TILE = 264


def _row_tile(r):
    return next(t for t in range(MOE_ROW_TILE, 7, -8) if r % t == 0)


DMA_UNROLL = 8


def _for_each_row(tr, fn):
    def body(i, c):
        fn(i)
        return c
    lax.fori_loop(0, tr, body, 0, unroll=DMA_UNROLL)


def _gather_kernel(tr, idx_ref, h_hbm, o_ref, buf, sem):
    base = (pl.program_id(0) * pl.num_programs(1) + pl.program_id(1)) * tr
    _for_each_row(tr, lambda i: pltpu.make_async_copy(
        h_hbm.at[pl.ds(idx_ref[base + i], 1)], buf.at[pl.ds(i, 1)], sem).start())
    pltpu.make_async_copy(h_hbm.at[pl.ds(0, tr)], buf, sem).wait()
    for j in range(buf.shape[1]):
        o_ref[0, :, j * LANES:(j + 1) * LANES] = buf[:, j, :].astype(BF16)


def _gather_rows(h3, idx):
    e, r = idx.shape
    _, s, lanes = h3.shape
    tr = _row_tile(r)
    return pl.pallas_call(
        functools.partial(_gather_kernel, tr),
        out_shape=jax.ShapeDtypeStruct((e, r, s * lanes), BF16),
        grid_spec=pltpu.PrefetchScalarGridSpec(
            num_scalar_prefetch=1, grid=(e, r // tr),
            in_specs=[pl.BlockSpec(memory_space=pl.ANY)],
            out_specs=pl.BlockSpec((1, tr, s * lanes), lambda ei, j, idx_ref: (ei, j, 0)),
            scratch_shapes=[pltpu.VMEM((tr, s, lanes), F32), pltpu.SemaphoreType.DMA(())]),
        compiler_params=_cparams(("arbitrary", "arbitrary")),
        name="moe_gather",
    )(idx.reshape(-1), h3)


MOE_TF = 256
MOE_ROW_CHUNKS = 4


def _expert_ffn_kernel(xs_ref, wg_ref, wu_ref, wd_ref, tv_ref, o_ref):
    j = pl.program_id(1)
    wg = wg_ref[0, 0].astype(BF16)
    wu = wu_ref[0, 0].astype(BF16)
    wd = wd_ref[0, 0].astype(BF16)
    rows = xs_ref.shape[1]
    rc = rows // MOE_ROW_CHUNKS

    @pl.when(j == 0)
    def _():
        o_ref[...] = jnp.zeros_like(o_ref)

    for ci in range(MOE_ROW_CHUNKS):
        sl = slice(ci * rc, (ci + 1) * rc)
        x = xs_ref[0, sl, :]
        g = jnp.dot(x, wg, preferred_element_type=F32)
        u = jnp.dot(x, wu, preferred_element_type=F32)
        a = (g * jax.nn.sigmoid(g) * u).astype(BF16)
        o_ref[0, sl, :] += jnp.dot(a, wd, preferred_element_type=F32)

    @pl.when(j == pl.num_programs(1) - 1)
    def _():
        o_ref[0] = o_ref[0] * tv_ref[0]


def _expert_ffn(xs, w_gate, w_up, w_down, tv, layer):
    e, r, d = xs.shape
    f = w_gate.shape[3]
    tf = MOE_TF
    rows = pl.BlockSpec((1, r, d), lambda ei, j: (ei, 0, 0))
    return pl.pallas_call(
        _expert_ffn_kernel,
        out_shape=jax.ShapeDtypeStruct((e, r, d), F32),
        grid=(e, f // tf),
        in_specs=[rows,
                  pl.BlockSpec((1, 1, d, tf), lambda ei, j: (layer, ei, 0, j)),
                  pl.BlockSpec((1, 1, d, tf), lambda ei, j: (layer, ei, 0, j)),
                  pl.BlockSpec((1, 1, tf, d), lambda ei, j: (layer, ei, j, 0)),
                  pl.BlockSpec((1, r, 1), lambda ei, j: (ei, 0, 0))],
        out_specs=rows,
        compiler_params=_cparams(("parallel", "arbitrary"), VMEM_LIMIT),
        name="expert_ffn",
    )(xs, w_gate, w_up, w_down, tv)


def _scatter_kernel(tr, idx_ref, y_ref, acc_in, acc_out, buf, sem_in, sem_out):
    del acc_in
    base = (pl.program_id(0) * pl.num_programs(1) + pl.program_id(1)) * tr

    _for_each_row(tr, lambda i: pltpu.make_async_copy(
        acc_out.at[pl.ds(idx_ref[base + i], 1)], buf.at[pl.ds(i, 1)], sem_in).start())
    pltpu.make_async_copy(acc_out.at[pl.ds(0, tr)], buf, sem_in).wait()
    for j in range(buf.shape[1]):
        buf[:, j, :] = buf[:, j, :] + y_ref[0, :, j * LANES:(j + 1) * LANES]
    _for_each_row(tr, lambda i: pltpu.make_async_copy(
        buf.at[pl.ds(i, 1)], acc_out.at[pl.ds(idx_ref[base + i], 1)], sem_out).start())
    pltpu.make_async_copy(buf, acc_out.at[pl.ds(0, tr)], sem_out).wait()


def _scatter_add_rows(y, idx, n_rows):
    e, r, d = y.shape
    tr = _row_tile(r)
    acc0 = jnp.zeros((n_rows, d // LANES, LANES), F32)
    return pl.pallas_call(
        functools.partial(_scatter_kernel, tr),
        out_shape=jax.ShapeDtypeStruct(acc0.shape, F32),
        grid_spec=pltpu.PrefetchScalarGridSpec(
            num_scalar_prefetch=1, grid=(e, r // tr),
            in_specs=[pl.BlockSpec((1, tr, d), lambda ei, j, idx_ref: (ei, j, 0)),
                      pl.BlockSpec(memory_space=pl.ANY)],
            out_specs=pl.BlockSpec(memory_space=pl.ANY),
            scratch_shapes=[pltpu.VMEM((tr, d // LANES, LANES), F32), pltpu.SemaphoreType.DMA(()),
                            pltpu.SemaphoreType.DMA(())]),
        input_output_aliases={2: 0},
        compiler_params=_cparams(("arbitrary", "arbitrary")),
        name="moe_scatter_add",
    )(idx.reshape(-1), y, acc0)


def _moe(x, mod, router_w, router_b, w_gate, w_up, w_down, layer):
    b, na, d = x.shape
    n_lat = na - CTX_LEN
    cap_c = EC_CAPACITY_FACTOR * CTX_LEN // N_EXPERTS
    cap_l = EC_CAPACITY_FACTOR * n_lat // N_EXPERTS
    h, aff = _router(x, mod, router_w, router_b)
    sel, _ = _select(aff, ((0, CTX_LEN, cap_c), (CTX_LEN, na, cap_l)))
    idx_c = jnp.argsort(1 - sel[:, :, :CTX_LEN], axis=-1, stable=True)[..., :cap_c]
    idx_l = jnp.argsort(1 - sel[:, :, CTX_LEN:], axis=-1, stable=True)[..., :cap_l] + CTX_LEN
    idx = jnp.concatenate([idx_l, idx_c], axis=-1).astype(jnp.int32)
    tv = jnp.take_along_axis(aff, idx, axis=-1)
    flat = idx + (jnp.arange(b, dtype=jnp.int32) * na)[:, None, None]
    flat = jnp.transpose(flat, (1, 0, 2)).reshape(N_EXPERTS, -1)
    tv = jnp.transpose(tv, (1, 0, 2)).reshape(N_EXPERTS, -1, 1)
    xs = _gather_rows(h.reshape(b * na, d // LANES, LANES), flat)
    y = _expert_ffn(xs, w_gate, w_up, w_down, tv, layer)
    return _scatter_add_rows(y, flat, b * na).reshape(b, na, d // LANES, LANES)


def _dwconv_kernel(silu, x_ref, w_ref, b_ref, o_ref):
    x = x_ref[0]
    na = x.shape[0]
    row = lax.broadcasted_iota(jnp.int32, x.shape, 0)
    first = (row == 0) | (row == CTX_LEN)
    last = (row == CTX_LEN - 1) | (row == na - 1)
    prev = jnp.where(first, 0.0, pltpu.roll(x, 1, axis=0))
    nxt = jnp.where(last, 0.0, pltpu.roll(x, na - 1, axis=0))
    y = w_ref[0:1, :] * prev + w_ref[1:2, :] * x + w_ref[2:3, :] * nxt + b_ref[...]
    if silu:
        y = y * jax.nn.sigmoid(y)
    o_ref[0, 0] = y


def _segment_dwconv(p, w, bias, groups, silu):
    b, na, c = p.shape
    per = c // groups // LANES
    return pl.pallas_call(
        functools.partial(_dwconv_kernel, silu),
        out_shape=jax.ShapeDtypeStruct((groups, b, na, c // groups), F32),
        grid=(b, c // LANES),
        in_specs=[pl.BlockSpec((1, na, LANES), lambda bi, j: (bi, 0, j)),
                  pl.BlockSpec((SHORT_CONV, LANES), lambda bi, j: (0, j)),
                  pl.BlockSpec((1, LANES), lambda bi, j: (0, j))],
        out_specs=pl.BlockSpec((1, 1, na, LANES), lambda bi, j: (j // per, bi, 0, j % per)),
        compiler_params=_cparams(("parallel", "parallel"), VMEM_LIMIT),
        name="short_conv",
    )(p, w, bias.reshape(1, c))


def _rope_tables(n_lat):
    rows = n_lat // GRID_W
    row = jnp.repeat(jnp.arange(rows, dtype=F32), GRID_W)
    col = (jnp.arange(n_lat) % GRID_W).astype(F32)
    nf = ATTN_HEAD_DIM // 4
    inv = ROPE_THETA ** (-jnp.arange(nf, dtype=F32) / nf)
    ar = row[:, None] * inv
    ac = col[:, None] * inv
    ang = jnp.concatenate([ar, ar, ac, ac], axis=-1)
    return jnp.cos(ang), jnp.sin(ang)


def _mixing(x, mod, w_in, mlstm_conv_w, mlstm_conv_b, mlstm_gate_b, mlstm_norm_w, attn_q_norm_w,
            attn_k_norm_w, hyena_conv_w, hyena_conv_b, hyena_filter, hyena_skip, with_ctx_out):
    b, na, _ = x.shape
    n_lat = na - CTX_LEN
    mqk, mv, mo, aq, ak, av, hy, mg = _in_projection(x, mod, w_in)
    mg = mg[..., :MLSTM_GATES]

    qk = _segment_dwconv(mqk, _pad_heads(mlstm_conv_w), _pad_heads(mlstm_conv_b), groups=1,
                         silu=True)[0]
    g = mg.reshape(b, na, N_DIR, 2, MLSTM_HEADS) + mlstm_gate_b
    g = jnp.stack([g[:, :, :, 0], jax.nn.log_sigmoid(g[:, :, :, 1])], axis=3)
    gcols = jnp.transpose(g, (2, 0, 1, 3, 4)).reshape(N_DIR, b, na, 2 * MLSTM_HEADS)
    grows = jnp.transpose(gcols, (0, 1, 3, 2))
    hm = _mlstm_scan(qk, mv, grows, gcols)

    q, k, v = _qkv_prep(aq, ak, av, attn_q_norm_w, attn_k_norm_w)
    ya = _attention(q, k, v)

    hv, hx1, hx2 = _segment_dwconv(hy, hyena_conv_w, hyena_conv_b, groups=3, silu=False)
    yh = _hyena_latent(hv[:, CTX_LEN:], hx1[:, CTX_LEN:], hx2[:, CTX_LEN:],
                       *_hyena_taps(n_lat, *hyena_filter), hyena_skip)
    if with_ctx_out:
        yh_c = _hyena_context(hv[:, :CTX_LEN], hx1[:, :CTX_LEN], hx2[:, :CTX_LEN],
                              *_hyena_taps(CTX_LEN, *hyena_filter), hyena_skip)
    else:
        yh_c = jnp.zeros((b, CTX_LEN, D_HYENA), F32)
    yh = jnp.concatenate([yh_c, yh], axis=1)
    return hm, mo, ya, yh


def kernel(x, c, ctx, c_ctx, w_mod, b_mod, w_in, mlstm_conv_w, mlstm_conv_b, mlstm_gate_b, mlstm_norm_w, attn_q_norm_w, attn_k_norm_w, hyena_conv_w, hyena_conv_b, hyena_f_w1, hyena_f_b1, hyena_f_freq, hyena_f_w2, hyena_f_b2, hyena_f_w3, hyena_skip, w_out, ln_mix_w, ln_mix_b, router_w, router_b, exp_w_gate, exp_w_up, exp_w_down, ln_ffn_w, ln_ffn_b):
    bsz, seq, d = x.shape
    assert d == D_MODEL and ctx.shape[1] == CTX_LEN == TOKEN_TILE and seq == FFT_R * FFT_R // 2
    alpha = (2.0 * DEPTH) ** 0.25
    xa = jnp.concatenate([ctx, x], axis=1)
    crows = jnp.concatenate([c, jnp.broadcast_to(c_ctx, (8 - bsz, d))], axis=0)
    for l in range(DEPTH):
        last = l == DEPTH - 1
        m = _modulation(crows, w_mod, b_mod[l], l).reshape(8, N_MOD, d)
        mod = jnp.stack([jnp.broadcast_to(m[bsz], (bsz, N_MOD, d)), m[:bsz]], axis=1)
        hm, mo, ya, yh = _mixing(
            xa, mod, w_in[l], mlstm_conv_w[l], mlstm_conv_b[l], mlstm_gate_b[l], mlstm_norm_w[l],
            attn_q_norm_w[l], attn_k_norm_w[l], hyena_conv_w[l], hyena_conv_b[l],
            (hyena_f_w1[l], hyena_f_b1[l], hyena_f_freq[l], hyena_f_w2[l], hyena_f_b2[l], hyena_f_w3[l]),
            hyena_skip[l], not last)
        xa = _out_projection(hm, mo, mlstm_norm_w[l], ya, yh, xa, mod, w_out[l].astype(BF16),
                             ln_mix_w[l], ln_mix_b[l], alpha)
        f = _moe(xa, mod, router_w[l], router_b[l], exp_w_gate, exp_w_up, exp_w_down, l)
        xa = _ffn_residual_ln(f, xa, mod, ln_ffn_w[l], ln_ffn_b[l], alpha)
    return xa[:, CTX_LEN:]
```

```python
import functools
import math

import numpy as np
import jax
import jax.numpy as jnp
from jax import lax
from jax.experimental import pallas as pl
from jax.experimental.pallas import tpu as pltpu

F32 = jnp.float32
BF16 = jnp.bfloat16
HI = lax.Precision.HIGHEST

D_MODEL = 1024
DEPTH = 2
GRID_W = 64
CTX_LEN = 256
N_DIR = 2
SHORT_CONV = 3

D_MLSTM = 256
MLSTM_HEAD_DIM = 64
MLSTM_HEADS = 4
MLSTM_GATES = 16
MLSTM_CHUNK = 128

D_ATTN = 512
ATTN_HEAD_DIM = 64
ATTN_HEADS = 8
ATTN_KV_HEADS = 2
ATTN_GROUP = 4
D_KV = 128
ROPE_THETA = 10000.0

D_HYENA = 256
HYENA_ORDER = 2
HYENA_BANDS = 16
HYENA_FAST_DECAY = 0.3
HYENA_SLOW_DECAY = 1.5
HYENA_DECAY_TARGET = 1e-2
HYENA_WINDOW_SHIFT = 0.05

N_IN = 4 * D_MLSTM + MLSTM_GATES + D_ATTN + 2 * D_KV + 3 * D_HYENA
N_EXPERTS = 16
EC_CAPACITY_FACTOR = 2
D_FF_EXPERT = 2816
N_MOD = 6
LN_EPS = 1e-5
RMS_EPS = 1e-6

LANES = 128
TOKEN_TILE = 256
FFT_R = 128
FFT_GROUP = 8
VMEM_LIMIT = 56 * 1024 * 1024


def _cparams(sem, vmem=None):
    return pltpu.CompilerParams(dimension_semantics=sem, vmem_limit_bytes=vmem)


def _mod_kernel(c_ref, w_ref, b_ref, o_ref):
    cs = c_ref[...]
    cs = cs * jax.nn.sigmoid(cs)
    o_ref[...] = jnp.dot(cs, w_ref[0], precision=HI, preferred_element_type=F32) + b_ref[...]


def _modulation(crows, w_mod, b_mod, layer):
    rows, d = crows.shape
    n = w_mod.shape[2]
    tn = 1024
    return pl.pallas_call(
        _mod_kernel,
        out_shape=jax.ShapeDtypeStruct((rows, n), F32),
        grid=(n // tn,),
        in_specs=[pl.BlockSpec((rows, d), lambda j: (0, 0)),
                  pl.BlockSpec((1, d, tn), lambda j: (layer, 0, j)),
                  pl.BlockSpec((1, tn), lambda j: (0, j))],
        out_specs=pl.BlockSpec((rows, tn), lambda j: (0, j)),
        compiler_params=_cparams(("parallel",)),
        name="adaln_mod",
    )(crows, w_mod, b_mod.reshape(1, n))


def _ln_rows(x):
    mu = jnp.mean(x, axis=-1, keepdims=True)
    xc = x - mu
    var = jnp.mean(xc * xc, axis=-1, keepdims=True)
    return xc * lax.rsqrt(var + LN_EPS)


def _mod_spec():
    return pl.BlockSpec((1, 1, N_MOD, D_MODEL), lambda b, i: (b, jnp.minimum(i, 1), 0, 0))


HEAD_DIM = 64
_C_MG = 4 * D_MLSTM
_C_AQ = _C_MG + MLSTM_GATES
IN_GROUPS = (("mqk", 0, 2 * D_MLSTM, True), ("mv", 2 * D_MLSTM, D_MLSTM, True), ("mo", 3 * D_MLSTM, D_MLSTM, False),
             ("aq", _C_AQ, D_ATTN, True), ("ak", _C_AQ + D_ATTN, D_KV, True), ("av", _C_AQ + D_ATTN + D_KV, D_KV, True),
             ("hy", _C_AQ + D_ATTN + 2 * D_KV, 3 * D_HYENA, False), ("mg", _C_MG, MLSTM_GATES, False))


def _group_width(src_width, head_padded):
    return src_width // HEAD_DIM * LANES if head_padded else -(-src_width // LANES) * LANES


def _pad_heads(a, axis=-1):
    axis = axis % a.ndim
    nh = a.shape[axis] // HEAD_DIM
    a = a.reshape(a.shape[:axis] + (nh, HEAD_DIM) + a.shape[axis + 1:])
    pad = [(0, 0)] * a.ndim
    pad[axis + 1] = (0, LANES - HEAD_DIM)
    a = jnp.pad(a, pad)
    return a.reshape(a.shape[:axis] + (nh * LANES,) + a.shape[axis + 2:])


def _inproj_kernel(x_ref, mod_ref, w_ref, *o_refs):
    y = _ln_rows(x_ref[0])
    h = (y * (1.0 + mod_ref[0, 0, 1:2, :]) + mod_ref[0, 0, 0:1, :]).astype(BF16)
    off = 0
    for o_ref in o_refs:
        width = o_ref.shape[2]
        o_ref[0] = jnp.dot(h, w_ref[:, off:off + width], preferred_element_type=F32)
        off += width


def _in_projection(x, mod, w_in):
    b, na, d = x.shape
    cols = []
    for _, start, src, head_padded in IN_GROUPS:
        wg = w_in[:, start:start + src]
        if head_padded:
            wg = _pad_heads(wg)
        elif src % LANES:
            wg = jnp.pad(wg, ((0, 0), (0, LANES - src % LANES)))
        cols.append(wg)
    w = jnp.concatenate(cols, axis=1).astype(BF16)
    n = w.shape[1]
    widths = [_group_width(src, hp) for _, _, src, hp in IN_GROUPS]
    return pl.pallas_call(
        _inproj_kernel,
        out_shape=[jax.ShapeDtypeStruct((b, na, width), F32) for width in widths],
        grid=(b, na // TOKEN_TILE),
        in_specs=[pl.BlockSpec((1, TOKEN_TILE, d), lambda bi, i: (bi, i, 0)),
                  _mod_spec(),
                  pl.BlockSpec((d, n), lambda bi, i: (0, 0))],
        out_specs=[pl.BlockSpec((1, TOKEN_TILE, width), lambda bi, i: (bi, i, 0)) for width in widths],
        compiler_params=_cparams(("parallel", "parallel"), VMEM_LIMIT),
        name="in_proj",
    )(x, mod, w)


def _compact_heads(x):
    tiles = [x[:, j * LANES:(j + 1) * LANES] for j in range(x.shape[1] // LANES)]
    return jnp.concatenate([tiles[j] + pltpu.roll(tiles[j + 1], HEAD_DIM, axis=1) for j in range(0, len(tiles), 2)],
                           axis=1)


def _outproj_kernel(alpha, hm0_ref, hm1_ref, mo_ref, nw_ref, seg_ref, ya_ref, yh_ref, x_ref, mod_ref, w_ref,
                    lw_ref, lb_ref, o_ref):
    hm = _compact_heads(hm0_ref[0, 0] + hm1_ref[0, 0])
    ss = jnp.dot(hm * hm, seg_ref[...], precision=HI, preferred_element_type=F32)
    hn = hm * lax.rsqrt(ss * (1.0 / MLSTM_HEAD_DIM) + RMS_EPS) * nw_ref[...]
    ym = jax.nn.sigmoid(mo_ref[0]) * hn
    mix = jnp.dot(ym.astype(BF16), w_ref[0:D_MLSTM, :], preferred_element_type=F32)
    mix = mix + jnp.dot(_compact_heads(ya_ref[0]).astype(BF16), w_ref[D_MLSTM:D_MLSTM + D_ATTN, :],
                        preferred_element_type=F32)
    mix = mix + jnp.dot(yh_ref[0].astype(BF16), w_ref[D_MLSTM + D_ATTN:, :], preferred_element_type=F32)
    r = alpha * x_ref[0] + mod_ref[0, 0, 2:3, :] * mix
    o_ref[0] = _ln_rows(r) * lw_ref[...] + lb_ref[...]


def _out_projection(hm, mo, norm_w, ya, yh, x, mod, w_bf16, ln_w, ln_b, alpha):
    b, na, d = x.shape
    dm = mo.shape[2]
    head = np.arange(dm) // MLSTM_HEAD_DIM
    seg = jnp.asarray((head[:, None] == head[None, :]).astype(np.float32))

    def tile(width):
        return pl.BlockSpec((1, TOKEN_TILE, width), lambda bi, i: (bi, i, 0))

    def hm_spec(direction):
        return pl.BlockSpec((1, 1, TOKEN_TILE, hm.shape[3]), lambda bi, i: (direction, bi, i, 0))

    vec = pl.BlockSpec((1, d), lambda bi, i: (0, 0))
    return pl.pallas_call(
        functools.partial(_outproj_kernel, alpha),
        out_shape=jax.ShapeDtypeStruct((b, na, d), F32),
        grid=(b, na // TOKEN_TILE),
        in_specs=[hm_spec(0), hm_spec(1), tile(dm), pl.BlockSpec((1, dm), lambda bi, i: (0, 0)),
                  pl.BlockSpec((dm, dm), lambda bi, i: (0, 0)), tile(ya.shape[2]), tile(yh.shape[2]), tile(d),
                  _mod_spec(), pl.BlockSpec((d, d), lambda bi, i: (0, 0)), vec, vec],
        out_specs=tile(d),
        compiler_params=_cparams(("parallel", "parallel"), VMEM_LIMIT),
        name="out_proj_ln",
    )(hm, hm, mo, norm_w.reshape(1, dm), seg, ya, yh, x, mod, w_bf16, ln_w.reshape(1, d), ln_b.reshape(1, d))


def _mlstm_kernel(qk_ref, v_ref, gr_ref, gc_ref, o_ref, c_sc, n_sc, m_sc):
    t = MLSTM_CHUNK
    nh = MLSTM_HEADS
    d = pl.program_id(0)

    @pl.when(pl.program_id(2) == 0)
    def _():
        c_sc[...] = jnp.zeros_like(c_sc)
        n_sc[...] = jnp.zeros_like(n_sc)
        m_sc[...] = jnp.zeros_like(m_sc)

    row = lax.broadcasted_iota(jnp.int32, (t, t), 0)
    col = lax.broadcasted_iota(jnp.int32, (t, t), 1)
    mask = jnp.where(d == 0, col - row, row - col) <= 0
    maskf = mask.astype(F32)
    grows = gr_ref[0, 0]
    gcols = gc_ref[0, 0]
    cum_cols = jnp.dot(maskf, gcols, precision=HI, preferred_element_type=F32)
    cum_rows = lax.dot_general(grows, maskf, (((1,), (1,)), ((), ())), precision=HI,
                               preferred_element_type=F32)
    lane8 = lax.broadcasted_iota(jnp.int32, (t, 2 * nh), 1)

    def column(a, idx):
        return jnp.sum(jnp.where(lane8 == idx, a, 0.0), axis=1, keepdims=True)

    dh = LANES
    for h in range(nh):
        qc = qk_ref[0, :, h * dh:(h + 1) * dh] * (MLSTM_HEAD_DIM ** -0.5)
        kc = qk_ref[0, :, (nh + h) * dh:(nh + h + 1) * dh]
        vc = v_ref[0, :, h * dh:(h + 1) * dh]
        ic_row = grows[h:h + 1, :]
        ic_col = column(gcols, h)
        bcol = column(cum_cols, nh + h)
        brow = cum_rows[nh + h:nh + h + 1, :]
        tot = jnp.sum(grows[nh + h:nh + h + 1, :], axis=1, keepdims=True)
        m0 = m_sc[h:h + 1, 0:1]
        log_inter = bcol + m0
        log_intra = jnp.where(mask, bcol - brow + ic_row, -jnp.inf)
        mrow = jnp.maximum(log_inter, jnp.max(log_intra, axis=1, keepdims=True))
        w_inter = jnp.exp(log_inter - mrow)
        qb = qc.astype(BF16)
        vb = vc.astype(BF16)
        scores = lax.dot_general(qb, kc.astype(BF16), (((1,), (1,)), ((), ())),
                                 preferred_element_type=F32) * jnp.exp(log_intra - mrow)
        ct = c_sc[h]
        n0 = n_sc[h]
        num = (w_inter * jnp.dot(qb, ct.astype(BF16), preferred_element_type=F32)
               + jnp.dot(scores.astype(BF16), vb, preferred_element_type=F32))
        den = w_inter * jnp.sum(qc * n0, axis=1, keepdims=True) + jnp.sum(scores, axis=1, keepdims=True)
        o_ref[0, 0, :, h * dh:(h + 1) * dh] = num / jnp.maximum(jnp.abs(den), jnp.exp(-mrow))
        log_src = tot - bcol + ic_col
        m_new = jnp.maximum(tot + m0, jnp.max(log_src, axis=0, keepdims=True))
        wk = jnp.exp(log_src - m_new) * kc
        decay = jnp.exp(tot + m0 - m_new)
        c_sc[h] = decay * ct + lax.dot_general(wk.astype(BF16), vb, (((0,), (0,)), ((), ())),
                                               preferred_element_type=F32)
        n_sc[h] = decay * n0 + jnp.sum(wk, axis=0, keepdims=True)
        m_sc[h:h + 1, :] = jnp.broadcast_to(m_new, (1, 128))


def _mlstm_scan(qk, v, grows, gcols):
    b, na, dv = v.shape
    nh, dh = MLSTM_HEADS, LANES
    t = MLSTM_CHUNK
    nc = na // t
    nctx = CTX_LEN // t

    def chunk(d, c):
        rev = jnp.where(c < nctx, nctx - 1 - c, nc + nctx - 1 - c)
        return jnp.where(d == 0, c, rev)

    return pl.pallas_call(
        _mlstm_kernel,
        out_shape=jax.ShapeDtypeStruct((N_DIR, b, na, dv), F32),
        grid=(N_DIR, b, nc),
        in_specs=[pl.BlockSpec((1, t, 2 * dv), lambda d, bi, c: (bi, chunk(d, c), 0)),
                  pl.BlockSpec((1, t, dv), lambda d, bi, c: (bi, chunk(d, c), 0)),
                  pl.BlockSpec((1, 1, 2 * nh, t), lambda d, bi, c: (d, bi, 0, chunk(d, c))),
                  pl.BlockSpec((1, 1, t, 2 * nh), lambda d, bi, c: (d, bi, chunk(d, c), 0))],
        out_specs=pl.BlockSpec((1, 1, t, dv), lambda d, bi, c: (d, bi, chunk(d, c), 0)),
        scratch_shapes=[pltpu.VMEM((nh, dh, dh), F32), pltpu.VMEM((nh, 1, dh), F32), pltpu.VMEM((8, 128), F32)],
        compiler_params=_cparams(("parallel", "parallel", "arbitrary")),
        name="mlstm_scan",
    )(qk, v, grows, gcols)


ATTN_TQ = 256
LOG2E = 1.4426950408889634


def _qkv_prep_kernel(aq_ref, ak_ref, av_ref, cos_ref, sin_ref, qw_ref, kw_ref, q_ref, k_ref, v_ref):
    lane = lax.broadcasted_iota(jnp.int32, (aq_ref.shape[1], LANES), 1)
    head_lane = lane < ATTN_HEAD_DIM
    first_half = (lane % (ATTN_HEAD_DIM // 2)) < (ATTN_HEAD_DIM // 4)
    cos = cos_ref[...]
    sin = sin_ref[...]
    quarter = ATTN_HEAD_DIM // 4

    def norm_rope(x, w):
        y = x * lax.rsqrt(jnp.sum(x * x, axis=1, keepdims=True) * (1.0 / ATTN_HEAD_DIM) + RMS_EPS) * w
        rot = jnp.where(first_half, -pltpu.roll(y, LANES - quarter, axis=1), pltpu.roll(y, quarter, axis=1))
        return jnp.where(head_lane, y * cos + rot * sin, 0.0)

    for h in range(ATTN_HEADS):
        q_ref[0, h] = norm_rope(aq_ref[0, :, h * LANES:(h + 1) * LANES], qw_ref[...]).astype(BF16)
    for h in range(ATTN_KV_HEADS):
        k_ref[0, h] = norm_rope(ak_ref[0, :, h * LANES:(h + 1) * LANES], kw_ref[...]).astype(BF16)
        v_ref[0, h] = jnp.where(lane == ATTN_HEAD_DIM, 1.0, av_ref[0, :, h * LANES:(h + 1) * LANES]).astype(BF16)


def _qkv_prep(aq, ak, av, q_norm_w, k_norm_w):
    b, na, _ = aq.shape
    cos, sin = _rope_tables(na - CTX_LEN)
    pad = ((CTX_LEN, 0), (0, LANES - ATTN_HEAD_DIM))
    cos = jnp.pad(cos, pad, constant_values=1.0)
    sin = jnp.pad(sin, pad)
    qw = jnp.pad(q_norm_w * (ATTN_HEAD_DIM ** -0.5 * LOG2E), (0, LANES - ATTN_HEAD_DIM)).reshape(1, LANES)
    kw = jnp.pad(k_norm_w, (0, LANES - ATTN_HEAD_DIM)).reshape(1, LANES)
    tok = lambda width: pl.BlockSpec((1, TOKEN_TILE, width), lambda bi, i: (bi, i, 0))
    tab = pl.BlockSpec((TOKEN_TILE, LANES), lambda bi, i: (i, 0))
    vec = pl.BlockSpec((1, LANES), lambda bi, i: (0, 0))
    heads = lambda nh: pl.BlockSpec((1, nh, TOKEN_TILE, LANES), lambda bi, i: (bi, 0, i, 0))
    return pl.pallas_call(
        _qkv_prep_kernel,
        out_shape=(jax.ShapeDtypeStruct((b, ATTN_HEADS, na, LANES), BF16),
                   jax.ShapeDtypeStruct((b, ATTN_KV_HEADS, na, LANES), BF16),
                   jax.ShapeDtypeStruct((b, ATTN_KV_HEADS, na, LANES), BF16)),
        grid=(b, na // TOKEN_TILE),
        in_specs=[tok(aq.shape[2]), tok(ak.shape[2]), tok(av.shape[2]), tab, tab, vec, vec],
        out_specs=(heads(ATTN_HEADS), heads(ATTN_KV_HEADS), heads(ATTN_KV_HEADS)),
        compiler_params=_cparams(("parallel", "parallel")),
        name="qkv_prep",
    )(aq, ak, av, cos, sin, qw, kw)


def _attn_kernel(q_ref, k_ref, v_ref, o_ref, s_ref, p_ref):
    def attend(nk):
        s_ref[:, 0:nk] = lax.dot_general(q_ref[0, 0], k_ref[0, 0, 0:nk, :], (((1,), (1,)), ((), ())),
                                         preferred_element_type=F32)
        m = jnp.max(s_ref[:, 0:nk], axis=1, keepdims=True)
        p_ref[:, 0:nk] = jnp.exp2(s_ref[:, 0:nk] - m).astype(BF16)
        acc = jnp.dot(p_ref[:, 0:nk], v_ref[0, 0, 0:nk, :], preferred_element_type=F32)
        lane = lax.broadcasted_iota(jnp.int32, acc.shape, 1)
        o_ref[0] = jnp.where(lane < ATTN_HEAD_DIM, acc / acc[:, ATTN_HEAD_DIM:ATTN_HEAD_DIM + 1], 0.0)

    is_ctx = pl.program_id(2) < CTX_LEN // ATTN_TQ
    pl.when(is_ctx)(lambda: attend(CTX_LEN))
    pl.when(jnp.logical_not(is_ctx))(lambda: attend(k_ref.shape[2]))


def _attention(q, k, v):
    b, nh, na, _ = q.shape
    group = nh // k.shape[1]
    kv_spec = pl.BlockSpec((1, 1, na, LANES), lambda bi, h, i: (bi, h // group, 0, 0))
    return pl.pallas_call(
        _attn_kernel,
        out_shape=jax.ShapeDtypeStruct((b, na, nh * LANES), F32),
        grid=(b, nh, na // ATTN_TQ),
        in_specs=[pl.BlockSpec((1, 1, ATTN_TQ, LANES), lambda bi, h, i: (bi, h, i, 0)), kv_spec, kv_spec],
        out_specs=pl.BlockSpec((1, ATTN_TQ, LANES), lambda bi, h, i: (bi, i, h)),
        scratch_shapes=[pltpu.VMEM((ATTN_TQ, na), F32), pltpu.VMEM((ATTN_TQ, na), BF16)],
        compiler_params=_cparams(("parallel", "parallel", "parallel"), VMEM_LIMIT),
        name="attention",
    )(q, k, v)


def _dft_tables(t2_len):
    r = FFT_R
    n = r * r
    idx = np.arange(r, dtype=np.float64)
    kb = idx[None, :, None]
    t1 = idx[:, None, None]
    t2 = np.arange(t2_len, dtype=np.float64)[None, None, :]
    ang = -2.0 * np.pi * (t2 * kb / r + t1 * kb / n)
    f1 = np.concatenate([np.cos(ang), np.sin(ang)], axis=1)
    ang2 = -2.0 * np.pi * np.outer(idx, idx) / r
    f2 = np.stack([np.cos(ang2), np.sin(ang2)])
    return f1.astype(np.float32), f2.astype(np.float32)


def _idft_table(t2_len):
    r = FFT_R
    n = r * r
    t1 = np.arange(r, dtype=np.float64)[:, None, None]
    t2 = np.arange(t2_len, dtype=np.float64)[None, :, None]
    kb = np.arange(r, dtype=np.float64)[None, None, :]
    ang = 2.0 * np.pi * (t2 * kb / r + t1 * kb / n)
    return (np.stack([np.cos(ang), np.sin(ang)], axis=1) / n).astype(np.float32)


def _dot_hi(table, x):
    return jnp.dot(table, x.astype(BF16), preferred_element_type=F32)


def _fft1_kernel(f_ref, z_ref, g_ref):
    for i in range(FFT_GROUP):
        g_ref[0, i] = _dot_hi(f_ref[i], z_ref[0, :, i, :])


def _fft_stage1(z4, f1):
    b, t2_len, r, c = z4.shape
    g = FFT_GROUP
    return pl.pallas_call(
        _fft1_kernel,
        out_shape=jax.ShapeDtypeStruct((b, r, 2 * r, c), F32),
        grid=(b, r // g),
        in_specs=[pl.BlockSpec((g, 2 * r, t2_len), lambda bi, j: (j, 0, 0)),
                  pl.BlockSpec((1, t2_len, g, c), lambda bi, j: (bi, 0, j, 0))],
        out_specs=pl.BlockSpec((1, g, 2 * r, c), lambda bi, j: (bi, j, 0, 0)),
        compiler_params=_cparams(("parallel", "parallel")),
        name="fft_stage1",
    )(f1, z4)


def _fft2_kernel(with_filter, f_ref, gr_ref, gi_ref, aux_ref, h_ref):
    fr = f_ref[0]
    fi = f_ref[1]
    for i in range(FFT_GROUP):
        gr = gr_ref[0, :, i, :]
        gi = gi_ref[0, :, i, :]
        xr = _dot_hi(fr, gr) - _dot_hi(fi, gi)
        xi = _dot_hi(fr, gi) + _dot_hi(fi, gr)
        if with_filter:
            tr = aux_ref[0, 0, i]
            ti = aux_ref[0, 1, i]
            yr = xr * tr - xi * ti
            yi = xr * ti + xi * tr
            xr = _dot_hi(fr, yr) + _dot_hi(fi, yi)
            xi = _dot_hi(fr, yi) - _dot_hi(fi, yr)
        else:
            xr = xr * aux_ref[0]
            xi = xi * aux_ref[0]
        h_ref[0, 0, i] = xr
        h_ref[0, 1, i] = xi


def _fft_stage2(g, f2, tf=None, scale=None):
    b, r, _, c = g.shape
    grp = FFT_GROUP
    nblk = r // grp
    in_specs = [pl.BlockSpec((2, r, r), lambda bi, j: (0, 0, 0)),
                pl.BlockSpec((1, r, grp, c), lambda bi, j: (bi, 0, j, 0)),
                pl.BlockSpec((1, r, grp, c), lambda bi, j: (bi, 0, nblk + j, 0))]
    args = [f2, g, g]
    if tf is not None:
        in_specs.append(pl.BlockSpec((1, 2, grp, r, c), lambda bi, j: (0, 0, j, 0, 0)))
        args.append(tf)
    else:
        in_specs.append(pl.BlockSpec((1, 1, c), lambda bi, j: (bi, 0, 0)))
        args.append(scale)
    return pl.pallas_call(
        functools.partial(_fft2_kernel, tf is not None),
        out_shape=jax.ShapeDtypeStruct((b, 2, r, r, c), F32),
        grid=(b, nblk),
        in_specs=in_specs,
        out_specs=pl.BlockSpec((1, 2, grp, r, c), lambda bi, j: (bi, 0, j, 0, 0)),
        compiler_params=_cparams(("parallel", "parallel"), VMEM_LIMIT),
        name="fft_stage2",
    )(*args)


def _ifft_kernel(e_ref, h_ref, z_ref, x_ref, skip_ref, o_ref):
    for i in range(FFT_GROUP):
        y = _dot_hi(e_ref[i, 0], h_ref[0, 0, :, i, :]) - _dot_hi(e_ref[i, 1], h_ref[0, 1, :, i, :])
        y = y * (1.0 / (FFT_R * FFT_R))
        o_ref[0, :, i, :] = x_ref[0, :, i, :] * (y + skip_ref[...] * z_ref[0, :, i, :])


def _ifft_gate(h, e, z4, x4, skip):
    b, t2_len, r, c = z4.shape
    g = FFT_GROUP
    tok = pl.BlockSpec((1, t2_len, g, c), lambda bi, j: (bi, 0, j, 0))
    return pl.pallas_call(
        _ifft_kernel,
        out_shape=jax.ShapeDtypeStruct(z4.shape, F32),
        grid=(b, r // g),
        in_specs=[pl.BlockSpec((g, 2, t2_len, r), lambda bi, j: (j, 0, 0, 0)),
                  pl.BlockSpec((1, 2, r, g, c), lambda bi, j: (bi, 0, 0, j, 0)),
                  tok, tok, pl.BlockSpec((1, c), lambda bi, j: (0, 0))],
        out_specs=tok,
        compiler_params=_cparams(("parallel", "parallel"), VMEM_LIMIT),
        name="ifft_gate",
    )(e, h, z4, x4, skip.reshape(1, c))


TAPS_ROWS = 1024
FEAT_PAD = 128


def _taps_kernel(n, feat_ref, w1_ref, b1_ref, freq_ref, w2_ref, b2_ref, w3_ref, delta_ref, taps_ref, l1_ref):
    c = D_HYENA
    rows = feat_ref.shape[0]
    step = pl.program_id(0)

    @pl.when(step == 0)
    def _():
        l1_ref[...] = jnp.zeros_like(l1_ref)

    f = feat_ref[...]
    hid = jnp.sin(freq_ref[0:1, :] * (jnp.dot(f, w1_ref[...], precision=HI, preferred_element_type=F32) + b1_ref[...]))
    hid = jnp.sin(freq_ref[1:2, :] * (jnp.dot(hid, w2_ref[...], precision=HI, preferred_element_type=F32) + b2_ref[...]))
    filt = jnp.dot(hid, w3_ref[...], precision=HI, preferred_element_type=F32)
    window = jnp.exp(-f[:, 0:1] * delta_ref[...]) + HYENA_WINDOW_SHIFT
    i = step * rows + lax.broadcasted_iota(jnp.int32, (rows, c), 0)
    for o in range(HYENA_ORDER):
        fwd = filt[:, o * c:(o + 1) * c]
        bwd = filt[:, (HYENA_ORDER + o) * c:(HYENA_ORDER + o + 1) * c]
        tap = jnp.where(i < n, fwd, jnp.where(i > n, bwd, 0.0)) * window
        taps_ref[o] = tap
        l1_ref[o:o + 1, :] += jnp.sum(jnp.abs(tap), axis=0, keepdims=True)


def _hyena_taps(n, f_w1, f_b1, f_freq, f_w2, f_b2, f_w3):
    i = np.arange(2 * n)
    t = (np.where(i < n, i, 2 * n - i).astype(np.float32) / np.float32(n)).astype(np.float32)
    ang = (np.float32(2.0 * math.pi) * t[:, None]) * np.arange(1, HYENA_BANDS + 1, dtype=np.float32)
    feats = np.zeros((2 * n, FEAT_PAD), np.float32)
    feats[:, 0] = t
    feats[:, 1:1 + HYENA_BANDS] = np.cos(ang.astype(np.float64))
    feats[:, 1 + HYENA_BANDS:1 + 2 * HYENA_BANDS] = np.sin(ang.astype(np.float64))
    log_target = abs(math.log(HYENA_DECAY_TARGET))
    deltas = jnp.linspace(log_target / HYENA_SLOW_DECAY, log_target / HYENA_FAST_DECAY, D_HYENA, dtype=F32)
    hid = f_w2.shape[0]
    w1 = jnp.concatenate([f_w1, jnp.zeros((FEAT_PAD - f_w1.shape[0], hid), F32)], axis=0)
    rows = min(TAPS_ROWS, 2 * n)
    full = lambda shape: pl.BlockSpec(shape, lambda j: (0,) * len(shape))
    return pl.pallas_call(
        functools.partial(_taps_kernel, n),
        out_shape=(jax.ShapeDtypeStruct((HYENA_ORDER, 2 * n, D_HYENA), F32),
                   jax.ShapeDtypeStruct((HYENA_ORDER, D_HYENA), F32)),
        grid=(2 * n // rows,),
        in_specs=[pl.BlockSpec((rows, FEAT_PAD), lambda j: (j, 0)), full((FEAT_PAD, hid)), full((1, hid)),
                  full((2, hid)), full((hid, hid)), full((1, hid)), full(f_w3.shape), full((1, D_HYENA))],
        out_specs=(pl.BlockSpec((HYENA_ORDER, rows, D_HYENA), lambda j: (0, j, 0)), full((HYENA_ORDER, D_HYENA))),
        compiler_params=_cparams(("arbitrary",)),
        name="hyena_taps",
    )(jnp.asarray(feats), w1, f_b1.reshape(1, hid), f_freq, f_w2, f_b2.reshape(1, hid), f_w3, deltas.reshape(1, -1))


def _hyena_latent(v, x1, x2, taps, l1, skip):
    b, n, c = v.shape
    r = FFT_R
    t2_len = n // r
    f1_full, f2 = _dft_tables(r)
    f1_half = jnp.asarray(f1_full[:, :, :t2_len]).astype(BF16)
    f1_full = jnp.asarray(f1_full).astype(BF16)
    f2 = jnp.asarray(f2).astype(BF16)
    e = jnp.asarray(_idft_table(t2_len) * (r * r)).astype(BF16)
    taps4 = taps.reshape(HYENA_ORDER, r, r, c)
    tf = _fft_stage2(_fft_stage1(taps4, f1_full), f2, scale=(1.0 / l1)[:, None, :])
    z = v.reshape(b, t2_len, r, c)
    for o, gate in enumerate((x1, x2)):
        g = _fft_stage1(z, f1_half)
        h = _fft_stage2(g, f2, tf[o:o + 1])
        z = _ifft_gate(h, e, z, gate.reshape(b, t2_len, r, c), skip[o])
    return z.reshape(b, n, c)


def _hyena_ctx_kernel(n, v_ref, x1_ref, x2_ref, k_ref, skip_ref, o_ref, z_sc):
    z_sc[...] = v_ref[0]
    for o, gate_ref in enumerate((x1_ref, x2_ref)):
        def body(s, acc):
            return acc + k_ref[o, pl.ds(n - 1 - s, n), :] * z_sc[pl.ds(s, 1), :]
        conv = lax.fori_loop(0, n, body, jnp.zeros(z_sc.shape, F32))
        z_sc[...] = gate_ref[0] * (conv + skip_ref[o:o + 1, :] * z_sc[...])
    o_ref[0] = z_sc[...]


def _hyena_context(v, x1, x2, taps, l1, skip):
    b, n, c = v.shape
    cb = 128
    k2 = jnp.roll(taps, n - 1, axis=1) / l1[:, None, :]
    tok = pl.BlockSpec((1, n, cb), lambda bi, j: (bi, 0, j))
    return pl.pallas_call(
        functools.partial(_hyena_ctx_kernel, n),
        out_shape=jax.ShapeDtypeStruct((b, n, c), F32),
        grid=(b, c // cb),
        in_specs=[tok, tok, tok,
                  pl.BlockSpec((HYENA_ORDER, 2 * n, cb), lambda bi, j: (0, 0, j)),
                  pl.BlockSpec((HYENA_ORDER, cb), lambda bi, j: (0, j))],
        out_specs=tok,
        scratch_shapes=[pltpu.VMEM((n, cb), F32)],
        compiler_params=_cparams(("parallel", "parallel")),
        name="hyena_ctx",
    )(v, x1, x2, k2, skip)


def _router_kernel(x_ref, mod_ref, rw_ref, rb_ref, h_ref, aff_ref):
    y = _ln_rows(x_ref[0])
    h = y * (1.0 + mod_ref[0, 0, 4:5, :]) + mod_ref[0, 0, 3:4, :]
    for j in range(h_ref.shape[2]):
        h_ref[0, :, j, :] = h[:, j * LANES:(j + 1) * LANES]
    logits = lax.dot_general(rw_ref[...], h, (((1,), (1,)), ((), ())), precision=HI,
                             preferred_element_type=F32) + rb_ref[...]
    z = jnp.exp(logits - jnp.max(logits, axis=0, keepdims=True))
    aff_ref[0] = z / jnp.sum(z, axis=0, keepdims=True)


def _router(x, mod, router_w, router_b):
    b, na, d = x.shape
    e = router_w.shape[1]
    return pl.pallas_call(
        _router_kernel,
        out_shape=(jax.ShapeDtypeStruct((b, na, d // LANES, LANES), F32), jax.ShapeDtypeStruct((b, e, na), F32)),
        grid=(b, na // TOKEN_TILE),
        in_specs=[pl.BlockSpec((1, TOKEN_TILE, d), lambda bi, i: (bi, i, 0)),
                  _mod_spec(),
                  pl.BlockSpec((e, d), lambda bi, i: (0, 0)),
                  pl.BlockSpec((e, 1), lambda bi, i: (0, 0))],
        out_specs=(pl.BlockSpec((1, TOKEN_TILE, d // LANES, LANES), lambda bi, i: (bi, i, 0, 0)),
                   pl.BlockSpec((1, e, TOKEN_TILE), lambda bi, i: (bi, 0, i))),
        compiler_params=_cparams(("parallel", "parallel")),
        name="router",
    )(x, mod, router_w.T, router_b.reshape(e, 1))


def _prefix_count(x):
    n = x.shape[1]
    lane = lax.broadcasted_iota(jnp.int32, x.shape, 1)
    sh = 1
    while sh < n:
        x = x + jnp.where(lane >= sh, pltpu.roll(x, sh, axis=1), 0)
        sh *= 2
    return x


SELECT_BISECTIONS = 160


def _select_kernel(segments, aff_ref, sel_ref, pos_ref):
    nb, ne, _ = aff_ref.shape
    parts = [(b, s0, s1, cap) for b in range(nb) for (s0, s1, cap) in segments]

    def body(_, bounds):
        out = []
        for (b, s0, s1, cap), (lo, hi) in zip(parts, bounds):
            mid = 0.5 * (lo + hi)
            cnt = jnp.sum((aff_ref[b, :, s0:s1] >= mid).astype(jnp.int32), axis=1, keepdims=True)
            ok = cnt >= cap
            out.append((jnp.where(ok, mid, lo), jnp.where(ok, hi, mid)))
        return tuple(out)

    init = tuple((jnp.zeros((ne, 1), F32), jnp.full((ne, 1), 2.0, F32)) for _ in parts)
    bounds = lax.fori_loop(0, SELECT_BISECTIONS, body, init)
    for (b, s0, s1, cap), (lo, hi) in zip(parts, bounds):
        a = aff_ref[b, :, s0:s1]
        above = (a >= hi).astype(jnp.int32)
        tied = jnp.where(a >= lo, 1, 0) - above
        need = cap - jnp.sum(above, axis=1, keepdims=True)
        tie_rank = _prefix_count(tied) - tied
        sel = above + tied * (tie_rank < need).astype(jnp.int32)
        sel_ref[b, :, s0:s1] = sel
        pos_ref[b, :, s0:s1] = _prefix_count(sel) - sel


def _select(aff, segments):
    b, e, na = aff.shape
    blk = pl.BlockSpec((b, e, na), lambda i: (0, 0, 0))
    return pl.pallas_call(
        functools.partial(_select_kernel, segments),
        out_shape=(jax.ShapeDtypeStruct((b, e, na), jnp.int32), jax.ShapeDtypeStruct((b, e, na), jnp.int32)),
        grid=(1,),
        in_specs=[blk],
        out_specs=(blk, blk),
        compiler_params=_cparams(("arbitrary",)),
        name="expert_select",
    )(aff)


MOE_ROW_TILE = 264


def _row_tile(r):
    return next(t for t in range(MOE_ROW_TILE, 7, -8) if r % t == 0)


DMA_UNROLL = 8


def _for_each_row(tr, fn):
    def body(i, c):
        fn(i)
        return c
    lax.fori_loop(0, tr, body, 0, unroll=DMA_UNROLL)


def _gather_kernel(tr, idx_ref, h_hbm, o_ref, buf, sem):
    step = pl.program_id(0) * pl.num_programs(1) + pl.program_id(1)
    n_steps = pl.num_programs(0) * pl.num_programs(1)
    cur = step % 2

    def start_tile(t, slot):
        _for_each_row(tr, lambda i: pltpu.make_async_copy(
            h_hbm.at[pl.ds(idx_ref[t * tr + i], 1)], buf.at[slot, pl.ds(i, 1)], sem.at[slot]).start())

    @pl.when(step == 0)
    def _():
        start_tile(0, 0)

    @pl.when(step + 1 < n_steps)
    def _():
        start_tile(step + 1, 1 - cur)

    pltpu.make_async_copy(h_hbm.at[pl.ds(0, tr)], buf.at[cur], sem.at[cur]).wait()
    for j in range(buf.shape[2]):
        o_ref[0, :, j * LANES:(j + 1) * LANES] = buf[cur, :, j, :].astype(BF16)


def _gather_rows(h3, idx):
    e, r = idx.shape
    _, s, lanes = h3.shape
    tr = _row_tile(r)
    return pl.pallas_call(
        functools.partial(_gather_kernel, tr),
        out_shape=jax.ShapeDtypeStruct((e, r, s * lanes), BF16),
        grid_spec=pltpu.PrefetchScalarGridSpec(
            num_scalar_prefetch=1, grid=(e, r // tr),
            in_specs=[pl.BlockSpec(memory_space=pl.ANY)],
            out_specs=pl.BlockSpec((1, tr, s * lanes), lambda ei, j, idx_ref: (ei, j, 0)),
            scratch_shapes=[pltpu.VMEM((2, tr, s, lanes), F32), pltpu.SemaphoreType.DMA((2,))]),
        compiler_params=_cparams(("arbitrary", "arbitrary")),
        name="moe_gather",
    )(idx.reshape(-1), h3)


MOE_TF = 256
MOE_ROW_CHUNKS = 4


COMBINE_CHUNK = 64


def _expert_ffn_kernel(xs_ref, wg_ref, wu_ref, wd_ref, tv_ref, o_ref, acc_ref):
    j = pl.program_id(1)
    wg = wg_ref[0, 0].astype(BF16)
    wu = wu_ref[0, 0].astype(BF16)
    wd = wd_ref[0, 0].astype(BF16)
    rows = xs_ref.shape[1]
    rc = rows // MOE_ROW_CHUNKS

    @pl.when(j == 0)
    def _():
        acc_ref[...] = jnp.zeros_like(acc_ref)

    for ci in range(MOE_ROW_CHUNKS):
        sl = slice(ci * rc, (ci + 1) * rc)
        x = xs_ref[0, sl, :]
        g = jnp.dot(x, wg, preferred_element_type=F32)
        u = jnp.dot(x, wu, preferred_element_type=F32)
        a = (g * jax.nn.sigmoid(g) * u).astype(BF16)
        acc_ref[sl, :] += jnp.dot(a, wd, preferred_element_type=F32)

    @pl.when(j == pl.num_programs(1) - 1)
    def _():
        y = acc_ref[...] * tv_ref[0]
        for t in range(o_ref.shape[2]):
            o_ref[0, 0:rows, t, :] = y[:, t * LANES:(t + 1) * LANES]
        o_ref[0, rows:, :, :] = jnp.zeros((o_ref.shape[1] - rows,) + o_ref.shape[2:], F32)


def _expert_ffn(xs, w_gate, w_up, w_down, tv, layer):
    e, r, d = xs.shape
    f = w_gate.shape[3]
    tf = MOE_TF
    return pl.pallas_call(
        _expert_ffn_kernel,
        out_shape=jax.ShapeDtypeStruct((e, r + COMBINE_CHUNK, d // LANES, LANES), F32),
        grid=(e, f // tf),
        in_specs=[pl.BlockSpec((1, r, d), lambda ei, j: (ei, 0, 0)),
                  pl.BlockSpec((1, 1, d, tf), lambda ei, j: (layer, ei, 0, j)),
                  pl.BlockSpec((1, 1, d, tf), lambda ei, j: (layer, ei, 0, j)),
                  pl.BlockSpec((1, 1, tf, d), lambda ei, j: (layer, ei, j, 0)),
                  pl.BlockSpec((1, r, 1), lambda ei, j: (ei, 0, 0))],
        out_specs=pl.BlockSpec((1, r + COMBINE_CHUNK, d // LANES, LANES), lambda ei, j: (ei, 0, 0, 0)),
        scratch_shapes=[pltpu.VMEM((r, d), F32)],
        compiler_params=_cparams(("parallel", "arbitrary"), VMEM_LIMIT),
        name="expert_ffn",
    )(xs, w_gate, w_up, w_down, tv)


COMBINE_UNROLL = 4
COMBINE_PAD = 8


def _combine_kernel(alpha, n_exp, r, n_tiles, idx_ref, st_ref, y_hbm, x_ref, mod_ref, lw_ref, lb_ref, o_ref,
                    acc, buf, sem):
    g = pl.program_id(0) * pl.num_programs(1) + pl.program_id(1)
    tile_base = g * TOKEN_TILE
    ch = COMBINE_CHUNK

    @pl.when(g == 0)
    def _():
        buf[...] = jnp.zeros_like(buf)

    acc[...] = jnp.zeros_like(acc)

    def run(e):
        s0 = st_ref[e * (n_tiles + 1) + g]
        return s0, st_ref[e * (n_tiles + 1) + g + 1] - s0

    def each_chunk(e, slot, act):
        s0, cnt = run(e)
        for c in range(TOKEN_TILE // ch):
            copy = pltpu.make_async_copy(y_hbm.at[e, pl.ds(s0 + c * ch, ch)], buf.at[slot, pl.ds(c * ch, ch)],
                                         sem.at[slot])
            pl.when(cnt > c * ch)(functools.partial(act, copy))

    each_chunk(0, 0, lambda copy: copy.start())
    for e in range(n_exp):
        slot = e % 2
        if e + 1 < n_exp:
            each_chunk(e + 1, 1 - slot, lambda copy: copy.start())
        each_chunk(e, slot, lambda copy: copy.wait())
        s0, cnt = run(e)

        def add_rows(it, carry, e=e, slot=slot, s0=s0, cnt=cnt):
            for u in range(COMBINE_UNROLL):
                k = it * COMBINE_UNROLL + u
                t = idx_ref[e * r + s0 + k] - tile_base
                t = jnp.where(k < cnt, t, TOKEN_TILE)
                acc[t] = acc[t] + buf[slot, k]
            return carry

        lax.fori_loop(0, (cnt + COMBINE_UNROLL - 1) // COMBINE_UNROLL, add_rows, 0)

    f = jnp.concatenate([acc[0:TOKEN_TILE, j, :] for j in range(acc.shape[1])], axis=1)
    res = alpha * x_ref[0] + mod_ref[0, 0, 5:6, :] * f
    o_ref[0] = _ln_rows(res) * lw_ref[...] + lb_ref[...]


def _combine_residual_ln(y, idx, starts, x, mod, ln_w, ln_b, alpha):
    b, na, d = x.shape
    e, r = idx.shape
    n_tiles = b * na // TOKEN_TILE
    tile = pl.BlockSpec((1, TOKEN_TILE, d), lambda bi, i, *_: (bi, i, 0))
    vec = pl.BlockSpec((1, d), lambda bi, i, *_: (0, 0))
    rows = TOKEN_TILE + COMBINE_PAD
    return pl.pallas_call(
        functools.partial(_combine_kernel, alpha, e, r, n_tiles),
        out_shape=jax.ShapeDtypeStruct((b, na, d), F32),
        grid_spec=pltpu.PrefetchScalarGridSpec(
            num_scalar_prefetch=2, grid=(b, na // TOKEN_TILE),
            in_specs=[pl.BlockSpec(memory_space=pl.ANY), tile,
                      pl.BlockSpec((1, 1, N_MOD, d), lambda bi, i, *_: (bi, jnp.minimum(i, 1), 0, 0)), vec, vec],
            out_specs=tile,
            scratch_shapes=[pltpu.VMEM((rows, d // LANES, LANES), F32), pltpu.VMEM((2, rows, d // LANES, LANES), F32),
                            pltpu.SemaphoreType.DMA((2,))]),
        compiler_params=_cparams(("arbitrary", "arbitrary")),
        name="moe_combine_ln",
    )(jnp.pad(idx.reshape(-1), (0, COMBINE_PAD)), starts.reshape(-1), y, x, mod, ln_w.reshape(1, d), ln_b.reshape(1, d))


def _moe_residual_ln(x, mod, router_w, router_b, w_gate, w_up, w_down, layer, ln_w, ln_b, alpha):
    b, na, d = x.shape
    n_lat = na - CTX_LEN
    cap_c = EC_CAPACITY_FACTOR * CTX_LEN // N_EXPERTS
    cap_l = EC_CAPACITY_FACTOR * n_lat // N_EXPERTS
    h, aff = _router(x, mod, router_w, router_b)
    sel, _ = _select(aff, ((0, CTX_LEN, cap_c), (CTX_LEN, na, cap_l)))
    idx_c = jnp.argsort(1 - sel[:, :, :CTX_LEN], axis=-1, stable=True)[..., :cap_c]
    idx_l = jnp.argsort(1 - sel[:, :, CTX_LEN:], axis=-1, stable=True)[..., :cap_l] + CTX_LEN
    idx = jnp.concatenate([idx_c, idx_l], axis=-1).astype(jnp.int32)
    tv = jnp.take_along_axis(aff, idx, axis=-1)
    flat = idx + (jnp.arange(b, dtype=jnp.int32) * na)[:, None, None]
    flat = jnp.transpose(flat, (1, 0, 2)).reshape(N_EXPERTS, -1)
    tv = jnp.transpose(tv, (1, 0, 2)).reshape(N_EXPERTS, -1, 1)
    per_tile = jnp.sum(sel.reshape(b, N_EXPERTS, na // TOKEN_TILE, TOKEN_TILE), axis=-1)
    per_tile = jnp.transpose(per_tile, (1, 0, 2)).reshape(N_EXPERTS, -1)
    starts = jnp.concatenate([jnp.zeros((N_EXPERTS, 1), jnp.int32), jnp.cumsum(per_tile, axis=1, dtype=jnp.int32)], axis=1)
    xs = _gather_rows(h.reshape(b * na, d // LANES, LANES), flat)
    y = _expert_ffn(xs, w_gate, w_up, w_down, tv, layer)
    return _combine_residual_ln(y, flat, starts, x, mod, ln_w, ln_b, alpha)


def _dwconv_kernel(silu, x_ref, w_ref, b_ref, o_ref):
    x = x_ref[0]
    na = x.shape[0]
    row = lax.broadcasted_iota(jnp.int32, x.shape, 0)
    first = (row == 0) | (row == CTX_LEN)
    last = (row == CTX_LEN - 1) | (row == na - 1)
    prev = jnp.where(first, 0.0, pltpu.roll(x, 1, axis=0))
    nxt = jnp.where(last, 0.0, pltpu.roll(x, na - 1, axis=0))
    y = w_ref[0:1, :] * prev + w_ref[1:2, :] * x + w_ref[2:3, :] * nxt + b_ref[...]
    if silu:
        y = y * jax.nn.sigmoid(y)
    o_ref[0, 0] = y


def _segment_dwconv(p, w, bias, groups, silu):
    b, na, c = p.shape
    per = c // groups // LANES
    return pl.pallas_call(
        functools.partial(_dwconv_kernel, silu),
        out_shape=jax.ShapeDtypeStruct((groups, b, na, c // groups), F32),
        grid=(b, c // LANES),
        in_specs=[pl.BlockSpec((1, na, LANES), lambda bi, j: (bi, 0, j)),
                  pl.BlockSpec((SHORT_CONV, LANES), lambda bi, j: (0, j)),
                  pl.BlockSpec((1, LANES), lambda bi, j: (0, j))],
        out_specs=pl.BlockSpec((1, 1, na, LANES), lambda bi, j: (j // per, bi, 0, j % per)),
        compiler_params=_cparams(("parallel", "parallel"), VMEM_LIMIT),
        name="short_conv",
    )(p, w, bias.reshape(1, c))


def _rope_tables(n_lat):
    rows = n_lat // GRID_W
    row = jnp.repeat(jnp.arange(rows, dtype=F32), GRID_W)
    col = (jnp.arange(n_lat) % GRID_W).astype(F32)
    nf = ATTN_HEAD_DIM // 4
    inv = ROPE_THETA ** (-jnp.arange(nf, dtype=F32) / nf)
    ar = row[:, None] * inv
    ac = col[:, None] * inv
    ang = jnp.concatenate([ar, ar, ac, ac], axis=-1)
    return jnp.cos(ang), jnp.sin(ang)


def _mixing(x, mod, w_in, mlstm_conv_w, mlstm_conv_b, mlstm_gate_b, mlstm_norm_w, attn_q_norm_w,
            attn_k_norm_w, hyena_conv_w, hyena_conv_b, hyena_filter, hyena_skip, with_ctx_out):
    b, na, _ = x.shape
    n_lat = na - CTX_LEN
    mqk, mv, mo, aq, ak, av, hy, mg = _in_projection(x, mod, w_in)
    mg = mg[..., :MLSTM_GATES]

    qk = _segment_dwconv(mqk, _pad_heads(mlstm_conv_w), _pad_heads(mlstm_conv_b), groups=1,
                         silu=True)[0]
    g = mg.reshape(b, na, N_DIR, 2, MLSTM_HEADS) + mlstm_gate_b
    g = jnp.stack([g[:, :, :, 0], jax.nn.log_sigmoid(g[:, :, :, 1])], axis=3)
    gcols = jnp.transpose(g, (2, 0, 1, 3, 4)).reshape(N_DIR, b, na, 2 * MLSTM_HEADS)
    grows = jnp.transpose(gcols, (0, 1, 3, 2))
    hm = _mlstm_scan(qk, mv, grows, gcols)

    q, k, v = _qkv_prep(aq, ak, av, attn_q_norm_w, attn_k_norm_w)
    ya = _attention(q, k, v)

    hv, hx1, hx2 = _segment_dwconv(hy, hyena_conv_w, hyena_conv_b, groups=3, silu=False)
    yh = _hyena_latent(hv[:, CTX_LEN:], hx1[:, CTX_LEN:], hx2[:, CTX_LEN:],
                       *_hyena_taps(n_lat, *hyena_filter), hyena_skip)
    if with_ctx_out:
        yh_c = _hyena_context(hv[:, :CTX_LEN], hx1[:, :CTX_LEN], hx2[:, :CTX_LEN],
                              *_hyena_taps(CTX_LEN, *hyena_filter), hyena_skip)
    else:
        yh_c = jnp.zeros((b, CTX_LEN, D_HYENA), F32)
    yh = jnp.concatenate([yh_c, yh], axis=1)
    return hm, mo, ya, yh


def kernel(x, c, ctx, c_ctx, w_mod, b_mod, w_in, mlstm_conv_w, mlstm_conv_b, mlstm_gate_b, mlstm_norm_w, attn_q_norm_w, attn_k_norm_w, hyena_conv_w, hyena_conv_b, hyena_f_w1, hyena_f_b1, hyena_f_freq, hyena_f_w2, hyena_f_b2, hyena_f_w3, hyena_skip, w_out, ln_mix_w, ln_mix_b, router_w, router_b, exp_w_gate, exp_w_up, exp_w_down, ln_ffn_w, ln_ffn_b):
    bsz, seq, d = x.shape
    assert d == D_MODEL and ctx.shape[1] == CTX_LEN == TOKEN_TILE and seq == FFT_R * FFT_R // 2
    alpha = (2.0 * DEPTH) ** 0.25
    xa = jnp.concatenate([ctx, x], axis=1)
    crows = jnp.concatenate([c, jnp.broadcast_to(c_ctx, (8 - bsz, d))], axis=0)
    for l in range(DEPTH):
        last = l == DEPTH - 1
        m = _modulation(crows, w_mod, b_mod[l], l).reshape(8, N_MOD, d)
        mod = jnp.stack([jnp.broadcast_to(m[bsz], (bsz, N_MOD, d)), m[:bsz]], axis=1)
        hm, mo, ya, yh = _mixing(
            xa, mod, w_in[l], mlstm_conv_w[l], mlstm_conv_b[l], mlstm_gate_b[l], mlstm_norm_w[l],
            attn_q_norm_w[l], attn_k_norm_w[l], hyena_conv_w[l], hyena_conv_b[l],
            (hyena_f_w1[l], hyena_f_b1[l], hyena_f_freq[l], hyena_f_w2[l], hyena_f_b2[l], hyena_f_w3[l]),
            hyena_skip[l], not last)
        xa = _out_projection(hm, mo, mlstm_norm_w[l], ya, yh, xa, mod, w_out[l].astype(BF16),
                             ln_mix_w[l], ln_mix_b[l], alpha)
        xa = _moe_residual_ln(xa, mod, router_w[l], router_b[l], exp_w_gate, exp_w_up, exp_w_down, l,
                              ln_ffn_w[l], ln_ffn_b[l], alpha)
    return xa[:, CTX_LEN:]
```

```python
import functools
import math

import numpy as np
import jax
import jax.numpy as jnp
from jax import lax
from jax.experimental import pallas as pl
from jax.experimental.pallas import tpu as pltpu

F32 = jnp.float32
BF16 = jnp.bfloat16
HI = lax.Precision.HIGHEST

D_MODEL = 1024
DEPTH = 2
GRID_W = 64
CTX_LEN = 256
N_DIR = 2
SHORT_CONV = 3

D_MLSTM = 256
MLSTM_HEAD_DIM = 64
MLSTM_HEADS = 4
MLSTM_GATES = 16
MLSTM_CHUNK = 128

D_ATTN = 512
ATTN_HEAD_DIM = 64
ATTN_HEADS = 8
ATTN_KV_HEADS = 2
ATTN_GROUP = 4
D_KV = 128
ROPE_THETA = 10000.0

D_HYENA = 256
HYENA_ORDER = 2
HYENA_BANDS = 16
HYENA_FAST_DECAY = 0.3
HYENA_SLOW_DECAY = 1.5
HYENA_DECAY_TARGET = 1e-2
HYENA_WINDOW_SHIFT = 0.05

N_IN = 4 * D_MLSTM + MLSTM_GATES + D_ATTN + 2 * D_KV + 3 * D_HYENA
N_EXPERTS = 16
EC_CAPACITY_FACTOR = 2
D_FF_EXPERT = 2816
N_MOD = 6
LN_EPS = 1e-5
RMS_EPS = 1e-6

LANES = 128
TOKEN_TILE = 256
FFT_R = 128
FFT_GROUP = 8
VMEM_LIMIT = 56 * 1024 * 1024


def _cparams(sem, vmem=None):
    return pltpu.CompilerParams(dimension_semantics=sem, vmem_limit_bytes=vmem)


def _mod_kernel(c_ref, w_ref, b_ref, o_ref):
    cs = c_ref[...]
    cs = cs * jax.nn.sigmoid(cs)
    o_ref[...] = jnp.dot(cs, w_ref[0], precision=HI, preferred_element_type=F32) + b_ref[...]


def _modulation(crows, w_mod, b_mod, layer):
    rows, d = crows.shape
    n = w_mod.shape[2]
    tn = 1024
    return pl.pallas_call(
        _mod_kernel,
        out_shape=jax.ShapeDtypeStruct((rows, n), F32),
        grid=(n // tn,),
        in_specs=[pl.BlockSpec((rows, d), lambda j: (0, 0)),
                  pl.BlockSpec((1, d, tn), lambda j: (layer, 0, j)),
                  pl.BlockSpec((1, tn), lambda j: (0, j))],
        out_specs=pl.BlockSpec((rows, tn), lambda j: (0, j)),
        compiler_params=_cparams(("parallel",)),
        name="adaln_mod",
    )(crows, w_mod, b_mod.reshape(1, n))


def _ln_rows(x):
    mu = jnp.mean(x, axis=-1, keepdims=True)
    xc = x - mu
    var = jnp.mean(xc * xc, axis=-1, keepdims=True)
    return xc * lax.rsqrt(var + LN_EPS)


def _mod_spec():
    return pl.BlockSpec((1, 1, N_MOD, D_MODEL), lambda b, i: (b, jnp.minimum(i, 1), 0, 0))


HEAD_DIM = 64
_C_MG = 4 * D_MLSTM
_C_AQ = _C_MG + MLSTM_GATES
IN_GROUPS = (("mqk", 0, 2 * D_MLSTM, True), ("mv", 2 * D_MLSTM, D_MLSTM, True), ("mo", 3 * D_MLSTM, D_MLSTM, False),
             ("aq", _C_AQ, D_ATTN, True), ("ak", _C_AQ + D_ATTN, D_KV, True), ("av", _C_AQ + D_ATTN + D_KV, D_KV, True),
             ("hy", _C_AQ + D_ATTN + 2 * D_KV, 3 * D_HYENA, False), ("mg", _C_MG, MLSTM_GATES, False))


def _group_width(src_width, head_padded):
    return src_width // HEAD_DIM * LANES if head_padded else -(-src_width // LANES) * LANES


def _pad_heads(a, axis=-1):
    axis = axis % a.ndim
    nh = a.shape[axis] // HEAD_DIM
    a = a.reshape(a.shape[:axis] + (nh, HEAD_DIM) + a.shape[axis + 1:])
    pad = [(0, 0)] * a.ndim
    pad[axis + 1] = (0, LANES - HEAD_DIM)
    a = jnp.pad(a, pad)
    return a.reshape(a.shape[:axis] + (nh * LANES,) + a.shape[axis + 2:])


def _inproj_kernel(x_ref, mod_ref, w_ref, *o_refs):
    y = _ln_rows(x_ref[0])
    h = (y * (1.0 + mod_ref[0, 0, 1:2, :]) + mod_ref[0, 0, 0:1, :]).astype(BF16)
    off = 0
    for o_ref in o_refs:
        width = o_ref.shape[2]
        o_ref[0] = jnp.dot(h, w_ref[:, off:off + width], preferred_element_type=F32)
        off += width


def _in_projection(x, mod, w_in):
    b, na, d = x.shape
    cols = []
    for _, start, src, head_padded in IN_GROUPS:
        wg = w_in[:, start:start + src]
        if head_padded:
            wg = _pad_heads(wg)
        elif src % LANES:
            wg = jnp.pad(wg, ((0, 0), (0, LANES - src % LANES)))
        cols.append(wg)
    w = jnp.concatenate(cols, axis=1).astype(BF16)
    n = w.shape[1]
    widths = [_group_width(src, hp) for _, _, src, hp in IN_GROUPS]
    return pl.pallas_call(
        _inproj_kernel,
        out_shape=[jax.ShapeDtypeStruct((b, na, width), F32) for width in widths],
        grid=(b, na // TOKEN_TILE),
        in_specs=[pl.BlockSpec((1, TOKEN_TILE, d), lambda bi, i: (bi, i, 0)),
                  _mod_spec(),
                  pl.BlockSpec((d, n), lambda bi, i: (0, 0))],
        out_specs=[pl.BlockSpec((1, TOKEN_TILE, width), lambda bi, i: (bi, i, 0)) for width in widths],
        compiler_params=_cparams(("parallel", "parallel"), VMEM_LIMIT),
        name="in_proj",
    )(x, mod, w)


def _compact_heads(x):
    tiles = [x[:, j * LANES:(j + 1) * LANES] for j in range(x.shape[1] // LANES)]
    return jnp.concatenate([tiles[j] + pltpu.roll(tiles[j + 1], HEAD_DIM, axis=1) for j in range(0, len(tiles), 2)],
                           axis=1)


def _outproj_kernel(alpha, hm0_ref, hm1_ref, mo_ref, nw_ref, seg_ref, ya_ref, yh_ref, x_ref, mod_ref, w_ref,
                    lw_ref, lb_ref, o_ref):
    hm = _compact_heads(hm0_ref[0, 0] + hm1_ref[0, 0])
    ss = jnp.dot(hm * hm, seg_ref[...], precision=HI, preferred_element_type=F32)
    hn = hm * lax.rsqrt(ss * (1.0 / MLSTM_HEAD_DIM) + RMS_EPS) * nw_ref[...]
    ym = jax.nn.sigmoid(mo_ref[0]) * hn
    mix = jnp.dot(ym.astype(BF16), w_ref[0:D_MLSTM, :], preferred_element_type=F32)
    mix = mix + jnp.dot(_compact_heads(ya_ref[0]).astype(BF16), w_ref[D_MLSTM:D_MLSTM + D_ATTN, :],
                        preferred_element_type=F32)
    mix = mix + jnp.dot(yh_ref[0].astype(BF16), w_ref[D_MLSTM + D_ATTN:, :], preferred_element_type=F32)
    r = alpha * x_ref[0] + mod_ref[0, 0, 2:3, :] * mix
    o_ref[0] = _ln_rows(r) * lw_ref[...] + lb_ref[...]


def _out_projection(hm, mo, norm_w, ya, yh, x, mod, w_bf16, ln_w, ln_b, alpha):
    b, na, d = x.shape
    dm = mo.shape[2]
    head = np.arange(dm) // MLSTM_HEAD_DIM
    seg = jnp.asarray((head[:, None] == head[None, :]).astype(np.float32))

    def tile(width):
        return pl.BlockSpec((1, TOKEN_TILE, width), lambda bi, i: (bi, i, 0))

    def hm_spec(direction):
        return pl.BlockSpec((1, 1, TOKEN_TILE, hm.shape[3]), lambda bi, i: (direction, bi, i, 0))

    vec = pl.BlockSpec((1, d), lambda bi, i: (0, 0))
    return pl.pallas_call(
        functools.partial(_outproj_kernel, alpha),
        out_shape=jax.ShapeDtypeStruct((b, na, d), F32),
        grid=(b, na // TOKEN_TILE),
        in_specs=[hm_spec(0), hm_spec(1), tile(dm), pl.BlockSpec((1, dm), lambda bi, i: (0, 0)),
                  pl.BlockSpec((dm, dm), lambda bi, i: (0, 0)), tile(ya.shape[2]), tile(yh.shape[2]), tile(d),
                  _mod_spec(), pl.BlockSpec((d, d), lambda bi, i: (0, 0)), vec, vec],
        out_specs=tile(d),
        compiler_params=_cparams(("parallel", "parallel"), VMEM_LIMIT),
        name="out_proj_ln",
    )(hm, hm, mo, norm_w.reshape(1, dm), seg, ya, yh, x, mod, w_bf16, ln_w.reshape(1, d), ln_b.reshape(1, d))


def _mlstm_kernel(qk_ref, v_ref, gr_ref, gc_ref, o_ref, c_sc, n_sc, m_sc):
    t = MLSTM_CHUNK
    nh = MLSTM_HEADS
    d = pl.program_id(0)

    @pl.when(pl.program_id(2) == 0)
    def _():
        c_sc[...] = jnp.zeros_like(c_sc)
        n_sc[...] = jnp.zeros_like(n_sc)
        m_sc[...] = jnp.zeros_like(m_sc)

    row = lax.broadcasted_iota(jnp.int32, (t, t), 0)
    col = lax.broadcasted_iota(jnp.int32, (t, t), 1)
    mask = jnp.where(d == 0, col - row, row - col) <= 0
    maskf = mask.astype(F32)
    grows = gr_ref[0, 0]
    gcols = gc_ref[0, 0]
    cum_cols = jnp.dot(maskf, gcols, precision=HI, preferred_element_type=F32)
    cum_rows = lax.dot_general(grows, maskf, (((1,), (1,)), ((), ())), precision=HI,
                               preferred_element_type=F32)
    lane8 = lax.broadcasted_iota(jnp.int32, (t, 2 * nh), 1)

    def column(a, idx):
        return jnp.sum(jnp.where(lane8 == idx, a, 0.0), axis=1, keepdims=True)

    dh = LANES
    for h in range(nh):
        qc = qk_ref[0, :, h * dh:(h + 1) * dh] * (MLSTM_HEAD_DIM ** -0.5)
        kc = qk_ref[0, :, (nh + h) * dh:(nh + h + 1) * dh]
        vc = v_ref[0, :, h * dh:(h + 1) * dh]
        ic_row = grows[h:h + 1, :]
        ic_col = column(gcols, h)
        bcol = column(cum_cols, nh + h)
        brow = cum_rows[nh + h:nh + h + 1, :]
        tot = jnp.sum(grows[nh + h:nh + h + 1, :], axis=1, keepdims=True)
        m0 = m_sc[h:h + 1, 0:1]
        log_inter = bcol + m0
        log_intra = jnp.where(mask, bcol - brow + ic_row, -jnp.inf)
        mrow = jnp.maximum(log_inter, jnp.max(log_intra, axis=1, keepdims=True))
        w_inter = jnp.exp(log_inter - mrow)
        qb = qc.astype(BF16)
        vb = vc.astype(BF16)
        scores = lax.dot_general(qb, kc.astype(BF16), (((1,), (1,)), ((), ())),
                                 preferred_element_type=F32) * jnp.exp(log_intra - mrow)
        ct = c_sc[h]
        n0 = n_sc[h]
        num = (w_inter * jnp.dot(qb, ct.astype(BF16), preferred_element_type=F32)
               + jnp.dot(scores.astype(BF16), vb, preferred_element_type=F32))
        den = w_inter * jnp.sum(qc * n0, axis=1, keepdims=True) + jnp.sum(scores, axis=1, keepdims=True)
        o_ref[0, 0, :, h * dh:(h + 1) * dh] = num / jnp.maximum(jnp.abs(den), jnp.exp(-mrow))
        log_src = tot - bcol + ic_col
        m_new = jnp.maximum(tot + m0, jnp.max(log_src, axis=0, keepdims=True))
        wk = jnp.exp(log_src - m_new) * kc
        decay = jnp.exp(tot + m0 - m_new)
        c_sc[h] = decay * ct + lax.dot_general(wk.astype(BF16), vb, (((0,), (0,)), ((), ())),
                                               preferred_element_type=F32)
        n_sc[h] = decay * n0 + jnp.sum(wk, axis=0, keepdims=True)
        m_sc[h:h + 1, :] = jnp.broadcast_to(m_new, (1, 128))


def _mlstm_scan(qk, v, grows, gcols):
    b, na, dv = v.shape
    nh, dh = MLSTM_HEADS, LANES
    t = MLSTM_CHUNK
    nc = na // t
    nctx = CTX_LEN // t

    def chunk(d, c):
        rev = jnp.where(c < nctx, nctx - 1 - c, nc + nctx - 1 - c)
        return jnp.where(d == 0, c, rev)

    return pl.pallas_call(
        _mlstm_kernel,
        out_shape=jax.ShapeDtypeStruct((N_DIR, b, na, dv), F32),
        grid=(N_DIR, b, nc),
        in_specs=[pl.BlockSpec((1, t, 2 * dv), lambda d, bi, c: (bi, chunk(d, c), 0)),
                  pl.BlockSpec((1, t, dv), lambda d, bi, c: (bi, chunk(d, c), 0)),
                  pl.BlockSpec((1, 1, 2 * nh, t), lambda d, bi, c: (d, bi, 0, chunk(d, c))),
                  pl.BlockSpec((1, 1, t, 2 * nh), lambda d, bi, c: (d, bi, chunk(d, c), 0))],
        out_specs=pl.BlockSpec((1, 1, t, dv), lambda d, bi, c: (d, bi, chunk(d, c), 0)),
        scratch_shapes=[pltpu.VMEM((nh, dh, dh), F32), pltpu.VMEM((nh, 1, dh), F32), pltpu.VMEM((8, 128), F32)],
        compiler_params=_cparams(("parallel", "parallel", "arbitrary")),
        name="mlstm_scan",
    )(qk, v, grows, gcols)


ATTN_TQ = 256
LOG2E = 1.4426950408889634


def _qkv_prep_kernel(aq_ref, ak_ref, av_ref, cos_ref, sin_ref, qw_ref, kw_ref, q_ref, k_ref, v_ref):
    lane = lax.broadcasted_iota(jnp.int32, (aq_ref.shape[1], LANES), 1)
    head_lane = lane < ATTN_HEAD_DIM
    first_half = (lane % (ATTN_HEAD_DIM // 2)) < (ATTN_HEAD_DIM // 4)
    cos = cos_ref[...]
    sin = sin_ref[...]
    quarter = ATTN_HEAD_DIM // 4

    def norm_rope(x, w):
        y = x * lax.rsqrt(jnp.sum(x * x, axis=1, keepdims=True) * (1.0 / ATTN_HEAD_DIM) + RMS_EPS) * w
        rot = jnp.where(first_half, -pltpu.roll(y, LANES - quarter, axis=1), pltpu.roll(y, quarter, axis=1))
        return jnp.where(head_lane, y * cos + rot * sin, 0.0)

    for h in range(ATTN_HEADS):
        q_ref[0, h] = norm_rope(aq_ref[0, :, h * LANES:(h + 1) * LANES], qw_ref[...]).astype(BF16)
    for h in range(ATTN_KV_HEADS):
        k_ref[0, h] = norm_rope(ak_ref[0, :, h * LANES:(h + 1) * LANES], kw_ref[...]).astype(BF16)
        v_ref[0, h] = jnp.where(lane == ATTN_HEAD_DIM, 1.0, av_ref[0, :, h * LANES:(h + 1) * LANES]).astype(BF16)


def _qkv_prep(aq, ak, av, q_norm_w, k_norm_w):
    b, na, _ = aq.shape
    cos, sin = _rope_tables(na - CTX_LEN)
    pad = ((CTX_LEN, 0), (0, LANES - ATTN_HEAD_DIM))
    cos = jnp.pad(cos, pad, constant_values=1.0)
    sin = jnp.pad(sin, pad)
    qw = jnp.pad(q_norm_w * (ATTN_HEAD_DIM ** -0.5 * LOG2E), (0, LANES - ATTN_HEAD_DIM)).reshape(1, LANES)
    kw = jnp.pad(k_norm_w, (0, LANES - ATTN_HEAD_DIM)).reshape(1, LANES)
    tok = lambda width: pl.BlockSpec((1, TOKEN_TILE, width), lambda bi, i: (bi, i, 0))
    tab = pl.BlockSpec((TOKEN_TILE, LANES), lambda bi, i: (i, 0))
    vec = pl.BlockSpec((1, LANES), lambda bi, i: (0, 0))
    heads = lambda nh: pl.BlockSpec((1, nh, TOKEN_TILE, LANES), lambda bi, i: (bi, 0, i, 0))
    return pl.pallas_call(
        _qkv_prep_kernel,
        out_shape=(jax.ShapeDtypeStruct((b, ATTN_HEADS, na, LANES), BF16),
                   jax.ShapeDtypeStruct((b, ATTN_KV_HEADS, na, LANES), BF16),
                   jax.ShapeDtypeStruct((b, ATTN_KV_HEADS, na, LANES), BF16)),
        grid=(b, na // TOKEN_TILE),
        in_specs=[tok(aq.shape[2]), tok(ak.shape[2]), tok(av.shape[2]), tab, tab, vec, vec],
        out_specs=(heads(ATTN_HEADS), heads(ATTN_KV_HEADS), heads(ATTN_KV_HEADS)),
        compiler_params=_cparams(("parallel", "parallel")),
        name="qkv_prep",
    )(aq, ak, av, cos, sin, qw, kw)


def _attn_kernel(q_ref, k_ref, v_ref, o_ref, s_ref, p_ref):
    def attend(nk):
        s_ref[:, 0:nk] = lax.dot_general(q_ref[0, 0], k_ref[0, 0, 0:nk, :], (((1,), (1,)), ((), ())),
                                         preferred_element_type=F32)
        m = jnp.max(s_ref[:, 0:nk], axis=1, keepdims=True)
        p_ref[:, 0:nk] = jnp.exp2(s_ref[:, 0:nk] - m).astype(BF16)
        acc = jnp.dot(p_ref[:, 0:nk], v_ref[0, 0, 0:nk, :], preferred_element_type=F32)
        lane = lax.broadcasted_iota(jnp.int32, acc.shape, 1)
        o_ref[0] = jnp.where(lane < ATTN_HEAD_DIM, acc / acc[:, ATTN_HEAD_DIM:ATTN_HEAD_DIM + 1], 0.0)

    is_ctx = pl.program_id(2) < CTX_LEN // ATTN_TQ
    pl.when(is_ctx)(lambda: attend(CTX_LEN))
    pl.when(jnp.logical_not(is_ctx))(lambda: attend(k_ref.shape[2]))


def _attention(q, k, v):
    b, nh, na, _ = q.shape
    group = nh // k.shape[1]
    kv_spec = pl.BlockSpec((1, 1, na, LANES), lambda bi, h, i: (bi, h // group, 0, 0))
    return pl.pallas_call(
        _attn_kernel,
        out_shape=jax.ShapeDtypeStruct((b, na, nh * LANES), F32),
        grid=(b, nh, na // ATTN_TQ),
        in_specs=[pl.BlockSpec((1, 1, ATTN_TQ, LANES), lambda bi, h, i: (bi, h, i, 0)), kv_spec, kv_spec],
        out_specs=pl.BlockSpec((1, ATTN_TQ, LANES), lambda bi, h, i: (bi, i, h)),
        scratch_shapes=[pltpu.VMEM((ATTN_TQ, na), F32), pltpu.VMEM((ATTN_TQ, na), BF16)],
        compiler_params=_cparams(("parallel", "parallel", "parallel"), VMEM_LIMIT),
        name="attention",
    )(q, k, v)


def _dft_tables(t2_len):
    r = FFT_R
    n = r * r
    idx = np.arange(r, dtype=np.float64)
    kb = idx[None, :, None]
    t1 = idx[:, None, None]
    t2 = np.arange(t2_len, dtype=np.float64)[None, None, :]
    ang = -2.0 * np.pi * (t2 * kb / r + t1 * kb / n)
    f1 = np.concatenate([np.cos(ang), np.sin(ang)], axis=1)
    ang2 = -2.0 * np.pi * np.outer(idx, idx) / r
    f2 = np.stack([np.cos(ang2), np.sin(ang2)])
    return f1.astype(np.float32), f2.astype(np.float32)


def _idft_table(t2_len):
    r = FFT_R
    n = r * r
    t1 = np.arange(r, dtype=np.float64)[:, None, None]
    t2 = np.arange(t2_len, dtype=np.float64)[None, :, None]
    kb = np.arange(r, dtype=np.float64)[None, None, :]
    ang = 2.0 * np.pi * (t2 * kb / r + t1 * kb / n)
    return (np.stack([np.cos(ang), np.sin(ang)], axis=1) / n).astype(np.float32)


def _dot_hi(table, x):
    return jnp.dot(table, x.astype(BF16), preferred_element_type=F32)


def _fft1_kernel(f_ref, z_ref, g_ref):
    for i in range(FFT_GROUP):
        g_ref[0, i] = _dot_hi(f_ref[i], z_ref[0, :, i, :])


def _fft_stage1(z4, f1):
    b, t2_len, r, c = z4.shape
    g = FFT_GROUP
    return pl.pallas_call(
        _fft1_kernel,
        out_shape=jax.ShapeDtypeStruct((b, r, 2 * r, c), F32),
        grid=(b, r // g),
        in_specs=[pl.BlockSpec((g, 2 * r, t2_len), lambda bi, j: (j, 0, 0)),
                  pl.BlockSpec((1, t2_len, g, c), lambda bi, j: (bi, 0, j, 0))],
        out_specs=pl.BlockSpec((1, g, 2 * r, c), lambda bi, j: (bi, j, 0, 0)),
        compiler_params=_cparams(("parallel", "parallel")),
        name="fft_stage1",
    )(f1, z4)


def _fft2_kernel(with_filter, f_ref, gr_ref, gi_ref, aux_ref, h_ref):
    fr = f_ref[0]
    fi = f_ref[1]
    for i in range(FFT_GROUP):
        gr = gr_ref[0, :, i, :]
        gi = gi_ref[0, :, i, :]
        xr = _dot_hi(fr, gr) - _dot_hi(fi, gi)
        xi = _dot_hi(fr, gi) + _dot_hi(fi, gr)
        if with_filter:
            tr = aux_ref[0, 0, i]
            ti = aux_ref[0, 1, i]
            yr = xr * tr - xi * ti
            yi = xr * ti + xi * tr
            xr = _dot_hi(fr, yr) + _dot_hi(fi, yi)
            xi = _dot_hi(fr, yi) - _dot_hi(fi, yr)
        else:
            xr = xr * aux_ref[0]
            xi = xi * aux_ref[0]
        h_ref[0, 0, i] = xr
        h_ref[0, 1, i] = xi


def _fft_stage2(g, f2, tf=None, scale=None):
    b, r, _, c = g.shape
    grp = FFT_GROUP
    nblk = r // grp
    in_specs = [pl.BlockSpec((2, r, r), lambda bi, j: (0, 0, 0)),
                pl.BlockSpec((1, r, grp, c), lambda bi, j: (bi, 0, j, 0)),
                pl.BlockSpec((1, r, grp, c), lambda bi, j: (bi, 0, nblk + j, 0))]
    args = [f2, g, g]
    if tf is not None:
        in_specs.append(pl.BlockSpec((1, 2, grp, r, c), lambda bi, j: (0, 0, j, 0, 0)))
        args.append(tf)
    else:
        in_specs.append(pl.BlockSpec((1, 1, c), lambda bi, j: (bi, 0, 0)))
        args.append(scale)
    return pl.pallas_call(
        functools.partial(_fft2_kernel, tf is not None),
        out_shape=jax.ShapeDtypeStruct((b, 2, r, r, c), F32),
        grid=(b, nblk),
        in_specs=in_specs,
        out_specs=pl.BlockSpec((1, 2, grp, r, c), lambda bi, j: (bi, 0, j, 0, 0)),
        compiler_params=_cparams(("parallel", "parallel"), VMEM_LIMIT),
        name="fft_stage2",
    )(*args)


def _ifft_kernel(e_ref, h_ref, z_ref, x_ref, skip_ref, o_ref):
    for i in range(FFT_GROUP):
        y = _dot_hi(e_ref[i, 0], h_ref[0, 0, :, i, :]) - _dot_hi(e_ref[i, 1], h_ref[0, 1, :, i, :])
        y = y * (1.0 / (FFT_R * FFT_R))
        o_ref[0, :, i, :] = x_ref[0, :, i, :] * (y + skip_ref[...] * z_ref[0, :, i, :])


def _ifft_gate(h, e, z4, x4, skip):
    b, t2_len, r, c = z4.shape
    g = FFT_GROUP
    tok = pl.BlockSpec((1, t2_len, g, c), lambda bi, j: (bi, 0, j, 0))
    return pl.pallas_call(
        _ifft_kernel,
        out_shape=jax.ShapeDtypeStruct(z4.shape, F32),
        grid=(b, r // g),
        in_specs=[pl.BlockSpec((g, 2, t2_len, r), lambda bi, j: (j, 0, 0, 0)),
                  pl.BlockSpec((1, 2, r, g, c), lambda bi, j: (bi, 0, 0, j, 0)),
                  tok, tok, pl.BlockSpec((1, c), lambda bi, j: (0, 0))],
        out_specs=tok,
        compiler_params=_cparams(("parallel", "parallel"), VMEM_LIMIT),
        name="ifft_gate",
    )(e, h, z4, x4, skip.reshape(1, c))


TAPS_ROWS = 1024
FEAT_PAD = 128


def _taps_kernel(n, feat_ref, w1_ref, b1_ref, freq_ref, w2_ref, b2_ref, w3_ref, delta_ref, taps_ref, l1_ref):
    c = D_HYENA
    rows = feat_ref.shape[0]
    step = pl.program_id(0)

    @pl.when(step == 0)
    def _():
        l1_ref[...] = jnp.zeros_like(l1_ref)

    f = feat_ref[...]
    hid = jnp.sin(freq_ref[0:1, :] * (jnp.dot(f, w1_ref[...], precision=HI, preferred_element_type=F32) + b1_ref[...]))
    hid = jnp.sin(freq_ref[1:2, :] * (jnp.dot(hid, w2_ref[...], precision=HI, preferred_element_type=F32) + b2_ref[...]))
    filt = jnp.dot(hid, w3_ref[...], precision=HI, preferred_element_type=F32)
    window = jnp.exp(-f[:, 0:1] * delta_ref[...]) + HYENA_WINDOW_SHIFT
    i = step * rows + lax.broadcasted_iota(jnp.int32, (rows, c), 0)
    for o in range(HYENA_ORDER):
        fwd = filt[:, o * c:(o + 1) * c]
        bwd = filt[:, (HYENA_ORDER + o) * c:(HYENA_ORDER + o + 1) * c]
        tap = jnp.where(i < n, fwd, jnp.where(i > n, bwd, 0.0)) * window
        taps_ref[o] = tap
        l1_ref[o:o + 1, :] += jnp.sum(jnp.abs(tap), axis=0, keepdims=True)


def _hyena_taps(n, f_w1, f_b1, f_freq, f_w2, f_b2, f_w3):
    i = np.arange(2 * n)
    t = (np.where(i < n, i, 2 * n - i).astype(np.float32) / np.float32(n)).astype(np.float32)
    ang = (np.float32(2.0 * math.pi) * t[:, None]) * np.arange(1, HYENA_BANDS + 1, dtype=np.float32)
    feats = np.zeros((2 * n, FEAT_PAD), np.float32)
    feats[:, 0] = t
    feats[:, 1:1 + HYENA_BANDS] = np.cos(ang.astype(np.float64))
    feats[:, 1 + HYENA_BANDS:1 + 2 * HYENA_BANDS] = np.sin(ang.astype(np.float64))
    log_target = abs(math.log(HYENA_DECAY_TARGET))
    deltas = jnp.linspace(log_target / HYENA_SLOW_DECAY, log_target / HYENA_FAST_DECAY, D_HYENA, dtype=F32)
    hid = f_w2.shape[0]
    w1 = jnp.concatenate([f_w1, jnp.zeros((FEAT_PAD - f_w1.shape[0], hid), F32)], axis=0)
    rows = min(TAPS_ROWS, 2 * n)
    full = lambda shape: pl.BlockSpec(shape, lambda j: (0,) * len(shape))
    return pl.pallas_call(
        functools.partial(_taps_kernel, n),
        out_shape=(jax.ShapeDtypeStruct((HYENA_ORDER, 2 * n, D_HYENA), F32),
                   jax.ShapeDtypeStruct((HYENA_ORDER, D_HYENA), F32)),
        grid=(2 * n // rows,),
        in_specs=[pl.BlockSpec((rows, FEAT_PAD), lambda j: (j, 0)), full((FEAT_PAD, hid)), full((1, hid)),
                  full((2, hid)), full((hid, hid)), full((1, hid)), full(f_w3.shape), full((1, D_HYENA))],
        out_specs=(pl.BlockSpec((HYENA_ORDER, rows, D_HYENA), lambda j: (0, j, 0)), full((HYENA_ORDER, D_HYENA))),
        compiler_params=_cparams(("arbitrary",)),
        name="hyena_taps",
    )(jnp.asarray(feats), w1, f_b1.reshape(1, hid), f_freq, f_w2, f_b2.reshape(1, hid), f_w3, deltas.reshape(1, -1))


def _hyena_latent(v, x1, x2, taps, l1, skip):
    b, n, c = v.shape
    r = FFT_R
    t2_len = n // r
    f1_full, f2 = _dft_tables(r)
    f1_half = jnp.asarray(f1_full[:, :, :t2_len]).astype(BF16)
    f1_full = jnp.asarray(f1_full).astype(BF16)
    f2 = jnp.asarray(f2).astype(BF16)
    e = jnp.asarray(_idft_table(t2_len) * (r * r)).astype(BF16)
    taps4 = taps.reshape(HYENA_ORDER, r, r, c)
    tf = _fft_stage2(_fft_stage1(taps4, f1_full), f2, scale=(1.0 / l1)[:, None, :])
    z = v.reshape(b, t2_len, r, c)
    for o, gate in enumerate((x1, x2)):
        g = _fft_stage1(z, f1_half)
        h = _fft_stage2(g, f2, tf[o:o + 1])
        z = _ifft_gate(h, e, z, gate.reshape(b, t2_len, r, c), skip[o])
    return z.reshape(b, n, c)


def _hyena_ctx_kernel(n, v_ref, x1_ref, x2_ref, k_ref, skip_ref, o_ref, z_sc):
    z_sc[...] = v_ref[0]
    for o, gate_ref in enumerate((x1_ref, x2_ref)):
        def body(s, acc):
            return acc + k_ref[o, pl.ds(n - 1 - s, n), :] * z_sc[pl.ds(s, 1), :]
        conv = lax.fori_loop(0, n, body, jnp.zeros(z_sc.shape, F32))
        z_sc[...] = gate_ref[0] * (conv + skip_ref[o:o + 1, :] * z_sc[...])
    o_ref[0] = z_sc[...]


def _hyena_context(v, x1, x2, taps, l1, skip):
    b, n, c = v.shape
    cb = 128
    k2 = jnp.roll(taps, n - 1, axis=1) / l1[:, None, :]
    tok = pl.BlockSpec((1, n, cb), lambda bi, j: (bi, 0, j))
    return pl.pallas_call(
        functools.partial(_hyena_ctx_kernel, n),
        out_shape=jax.ShapeDtypeStruct((b, n, c), F32),
        grid=(b, c // cb),
        in_specs=[tok, tok, tok,
                  pl.BlockSpec((HYENA_ORDER, 2 * n, cb), lambda bi, j: (0, 0, j)),
                  pl.BlockSpec((HYENA_ORDER, cb), lambda bi, j: (0, j))],
        out_specs=tok,
        scratch_shapes=[pltpu.VMEM((n, cb), F32)],
        compiler_params=_cparams(("parallel", "parallel")),
        name="hyena_ctx",
    )(v, x1, x2, k2, skip)


def _router_kernel(x_ref, mod_ref, rw_ref, rb_ref, h_ref, aff_ref):
    y = _ln_rows(x_ref[0])
    h = y * (1.0 + mod_ref[0, 0, 4:5, :]) + mod_ref[0, 0, 3:4, :]
    for j in range(h_ref.shape[2]):
        h_ref[0, :, j, :] = h[:, j * LANES:(j + 1) * LANES]
    logits = lax.dot_general(rw_ref[...], h, (((1,), (1,)), ((), ())), precision=HI,
                             preferred_element_type=F32) + rb_ref[...]
    z = jnp.exp(logits - jnp.max(logits, axis=0, keepdims=True))
    aff_ref[0] = z / jnp.sum(z, axis=0, keepdims=True)


def _router(x, mod, router_w, router_b):
    b, na, d = x.shape
    e = router_w.shape[1]
    return pl.pallas_call(
        _router_kernel,
        out_shape=(jax.ShapeDtypeStruct((b, na, d // LANES, LANES), F32), jax.ShapeDtypeStruct((b, e, na), F32)),
        grid=(b, na // TOKEN_TILE),
        in_specs=[pl.BlockSpec((1, TOKEN_TILE, d), lambda bi, i: (bi, i, 0)),
                  _mod_spec(),
                  pl.BlockSpec((e, d), lambda bi, i: (0, 0)),
                  pl.BlockSpec((e, 1), lambda bi, i: (0, 0))],
        out_specs=(pl.BlockSpec((1, TOKEN_TILE, d // LANES, LANES), lambda bi, i: (bi, i, 0, 0)),
                   pl.BlockSpec((1, e, TOKEN_TILE), lambda bi, i: (bi, 0, i))),
        compiler_params=_cparams(("parallel", "parallel")),
        name="router",
    )(x, mod, router_w.T, router_b.reshape(e, 1))


def _prefix_count(x):
    n = x.shape[1]
    lane = lax.broadcasted_iota(jnp.int32, x.shape, 1)
    sh = 1
    while sh < n:
        x = x + jnp.where(lane >= sh, pltpu.roll(x, sh, axis=1), 0)
        sh *= 2
    return x


SELECT_BISECTIONS = 160


def _select_kernel(segments, aff_ref, sel_ref, pos_ref):
    nb, ne, _ = aff_ref.shape
    parts = [(b, s0, s1, cap) for b in range(nb) for (s0, s1, cap) in segments]

    def body(_, bounds):
        out = []
        for (b, s0, s1, cap), (lo, hi) in zip(parts, bounds):
            mid = 0.5 * (lo + hi)
            cnt = jnp.sum((aff_ref[b, :, s0:s1] >= mid).astype(jnp.int32), axis=1, keepdims=True)
            ok = cnt >= cap
            out.append((jnp.where(ok, mid, lo), jnp.where(ok, hi, mid)))
        return tuple(out)

    init = tuple((jnp.zeros((ne, 1), F32), jnp.full((ne, 1), 2.0, F32)) for _ in parts)
    bounds = lax.fori_loop(0, SELECT_BISECTIONS, body, init)
    for (b, s0, s1, cap), (lo, hi) in zip(parts, bounds):
        a = aff_ref[b, :, s0:s1]
        above = (a >= hi).astype(jnp.int32)
        tied = jnp.where(a >= lo, 1, 0) - above
        need = cap - jnp.sum(above, axis=1, keepdims=True)
        tie_rank = _prefix_count(tied) - tied
        sel = above + tied * (tie_rank < need).astype(jnp.int32)
        sel_ref[b, :, s0:s1] = sel
        pos_ref[b, :, s0:s1] = _prefix_count(sel) - sel


def _select(aff, segments):
    b, e, na = aff.shape
    blk = pl.BlockSpec((b, e, na), lambda i: (0, 0, 0))
    return pl.pallas_call(
        functools.partial(_select_kernel, segments),
        out_shape=(jax.ShapeDtypeStruct((b, e, na), jnp.int32), jax.ShapeDtypeStruct((b, e, na), jnp.int32)),
        grid=(1,),
        in_specs=[blk],
        out_specs=(blk, blk),
        compiler_params=_cparams(("arbitrary",)),
        name="expert_select",
    )(aff)


MOE_ROW_TILE = 264


def _row_tile(r):
    return next(t for t in range(MOE_ROW_TILE, 7, -8) if r % t == 0)


DMA_UNROLL = 8


def _for_each_row(tr, fn):
    def body(i, c):
        fn(i)
        return c
    lax.fori_loop(0, tr, body, 0, unroll=DMA_UNROLL)


def _gather_kernel(tr, idx_ref, h_hbm, o_ref, buf, sem):
    step = pl.program_id(0) * pl.num_programs(1) + pl.program_id(1)
    n_steps = pl.num_programs(0) * pl.num_programs(1)
    cur = step % 2

    def start_tile(t, slot):
        _for_each_row(tr, lambda i: pltpu.make_async_copy(
            h_hbm.at[pl.ds(idx_ref[t * tr + i], 1)], buf.at[slot, pl.ds(i, 1)], sem.at[slot]).start())

    @pl.when(step == 0)
    def _():
        start_tile(0, 0)

    @pl.when(step + 1 < n_steps)
    def _():
        start_tile(step + 1, 1 - cur)

    pltpu.make_async_copy(h_hbm.at[pl.ds(0, tr)], buf.at[cur], sem.at[cur]).wait()
    for j in range(buf.shape[2]):
        o_ref[0, :, j * LANES:(j + 1) * LANES] = buf[cur, :, j, :].astype(BF16)


def _gather_rows(h3, idx):
    e, r = idx.shape
    _, s, lanes = h3.shape
    tr = _row_tile(r)
    return pl.pallas_call(
        functools.partial(_gather_kernel, tr),
        out_shape=jax.ShapeDtypeStruct((e, r, s * lanes), BF16),
        grid_spec=pltpu.PrefetchScalarGridSpec(
            num_scalar_prefetch=1, grid=(e, r // tr),
            in_specs=[pl.BlockSpec(memory_space=pl.ANY)],
            out_specs=pl.BlockSpec((1, tr, s * lanes), lambda ei, j, idx_ref: (ei, j, 0)),
            scratch_shapes=[pltpu.VMEM((2, tr, s, lanes), F32), pltpu.SemaphoreType.DMA((2,))]),
        compiler_params=_cparams(("arbitrary", "arbitrary")),
        name="moe_gather",
    )(idx.reshape(-1), h3)


MOE_TF = 256
MOE_ROW_CHUNKS = 4


COMBINE_CHUNK = 64


def _expert_ffn_kernel(xs_ref, wg_ref, wu_ref, wd_ref, tv_ref, o_ref, acc_ref):
    j = pl.program_id(1)
    wg = wg_ref[0, 0].astype(BF16)
    wu = wu_ref[0, 0].astype(BF16)
    wd = wd_ref[0, 0].astype(BF16)
    rows = xs_ref.shape[1]
    rc = rows // MOE_ROW_CHUNKS

    @pl.when(j == 0)
    def _():
        acc_ref[...] = jnp.zeros_like(acc_ref)

    for ci in range(MOE_ROW_CHUNKS):
        sl = slice(ci * rc, (ci + 1) * rc)
        x = xs_ref[0, sl, :]
        g = jnp.dot(x, wg, preferred_element_type=F32)
        u = jnp.dot(x, wu, preferred_element_type=F32)
        a = (g * jax.nn.sigmoid(g) * u).astype(BF16)
        acc_ref[sl, :] += jnp.dot(a, wd, preferred_element_type=F32)

    @pl.when(j == pl.num_programs(1) - 1)
    def _():
        y = acc_ref[...] * tv_ref[0]
        for t in range(o_ref.shape[2]):
            o_ref[0, 0:rows, t, :] = y[:, t * LANES:(t + 1) * LANES]
        o_ref[0, rows:, :, :] = jnp.zeros((o_ref.shape[1] - rows,) + o_ref.shape[2:], F32)


def _expert_ffn(xs, w_gate, w_up, w_down, tv, layer):
    e, r, d = xs.shape
    f = w_gate.shape[3]
    tf = MOE_TF
    return pl.pallas_call(
        _expert_ffn_kernel,
        out_shape=jax.ShapeDtypeStruct((e, r + COMBINE_CHUNK, d // LANES, LANES), F32),
        grid=(e, f // tf),
        in_specs=[pl.BlockSpec((1, r, d), lambda ei, j: (ei, 0, 0)),
                  pl.BlockSpec((1, 1, d, tf), lambda ei, j: (layer, ei, 0, j)),
                  pl.BlockSpec((1, 1, d, tf), lambda ei, j: (layer, ei, 0, j)),
                  pl.BlockSpec((1, 1, tf, d), lambda ei, j: (layer, ei, j, 0)),
                  pl.BlockSpec((1, r, 1), lambda ei, j: (ei, 0, 0))],
        out_specs=pl.BlockSpec((1, r + COMBINE_CHUNK, d // LANES, LANES), lambda ei, j: (ei, 0, 0, 0)),
        scratch_shapes=[pltpu.VMEM((r, d), F32)],
        compiler_params=_cparams(("parallel", "arbitrary"), VMEM_LIMIT),
        name="expert_ffn",
    )(xs, w_gate, w_up, w_down, tv)


COMBINE_UNROLL = 4
COMBINE_PAD = 8


def _combine_kernel(alpha, n_exp, r, n_tiles, idx_ref, st_ref, y_hbm, x_ref, mod_ref, lw_ref, lb_ref, o_ref,
                    acc, head, tail, head_sem, tail_sem):
    g = pl.program_id(0) * pl.num_programs(1) + pl.program_id(1)
    tile_base = g * TOKEN_TILE
    ch = COMBINE_CHUNK
    cur = g % 2

    def run(e, tile):
        s0 = st_ref[e * (n_tiles + 1) + tile]
        return s0, st_ref[e * (n_tiles + 1) + tile + 1] - s0

    def head_copy(e, tile, slot):
        s0, cnt = run(e, tile)
        return pltpu.make_async_copy(y_hbm.at[e, pl.ds(s0, ch)], head.at[slot, e, pl.ds(0, ch)], head_sem.at[slot, e]), cnt

    def start_heads(tile, slot):
        for e in range(n_exp):
            copy, cnt = head_copy(e, tile, slot)
            pl.when(cnt > 0)(copy.start)

    @pl.when(g == 0)
    def _():
        head[...] = jnp.zeros_like(head)
        tail[...] = jnp.zeros_like(tail)
        start_heads(0, 0)

    @pl.when(g + 1 < n_tiles)
    def _():
        start_heads(g + 1, 1 - cur)

    acc[...] = jnp.zeros_like(acc)

    def add_rows(src, first, count):
        def body(it, carry):
            for u in range(COMBINE_UNROLL):
                j = it * COMBINE_UNROLL + u
                t = jnp.where(j < count, idx_ref[first + j] - tile_base, TOKEN_TILE)
                acc[t] = acc[t] + src[j]
            return carry
        lax.fori_loop(0, (count + COMBINE_UNROLL - 1) // COMBINE_UNROLL, body, 0)

    for e in range(n_exp):
        s0, cnt = run(e, g)
        copy, _ = head_copy(e, g, cur)
        pl.when(cnt > 0)(copy.wait)
        add_rows(head.at[cur, e], e * r + s0, jnp.minimum(cnt, ch))
        for c in range(1, TOKEN_TILE // ch):
            @pl.when(cnt > c * ch)
            def _(e=e, c=c, s0=s0, cnt=cnt):
                more = pltpu.make_async_copy(y_hbm.at[e, pl.ds(s0 + c * ch, ch)], tail.at[pl.ds(0, ch)], tail_sem)
                more.start()
                more.wait()
                add_rows(tail, e * r + s0 + c * ch, jnp.minimum(cnt - c * ch, ch))

    f = jnp.concatenate([acc[0:TOKEN_TILE, j, :] for j in range(acc.shape[1])], axis=1)
    res = alpha * x_ref[0] + mod_ref[0, 0, 5:6, :] * f
    o_ref[0] = _ln_rows(res) * lw_ref[...] + lb_ref[...]


def _combine_residual_ln(y, idx, starts, x, mod, ln_w, ln_b, alpha):
    b, na, d = x.shape
    e, r = idx.shape
    n_tiles = b * na // TOKEN_TILE
    tile = pl.BlockSpec((1, TOKEN_TILE, d), lambda bi, i, *_: (bi, i, 0))
    vec = pl.BlockSpec((1, d), lambda bi, i, *_: (0, 0))
    row_tile = (d // LANES, LANES)
    chunk_rows = COMBINE_CHUNK + COMBINE_PAD
    return pl.pallas_call(
        functools.partial(_combine_kernel, alpha, e, r, n_tiles),
        out_shape=jax.ShapeDtypeStruct((b, na, d), F32),
        grid_spec=pltpu.PrefetchScalarGridSpec(
            num_scalar_prefetch=2, grid=(b, na // TOKEN_TILE),
            in_specs=[pl.BlockSpec(memory_space=pl.ANY), tile,
                      pl.BlockSpec((1, 1, N_MOD, d), lambda bi, i, *_: (bi, jnp.minimum(i, 1), 0, 0)), vec, vec],
            out_specs=tile,
            scratch_shapes=[pltpu.VMEM((TOKEN_TILE + COMBINE_PAD,) + row_tile, F32),
                            pltpu.VMEM((2, e, chunk_rows) + row_tile, F32),
                            pltpu.VMEM((chunk_rows,) + row_tile, F32),
                            pltpu.SemaphoreType.DMA((2, e)), pltpu.SemaphoreType.DMA(())]),
        compiler_params=_cparams(("arbitrary", "arbitrary"), VMEM_LIMIT),
        name="moe_combine_ln",
    )(jnp.pad(idx.reshape(-1), (0, COMBINE_PAD)), starts.reshape(-1), y, x, mod, ln_w.reshape(1, d), ln_b.reshape(1, d))


def _moe_residual_ln(x, mod, router_w, router_b, w_gate, w_up, w_down, layer, ln_w, ln_b, alpha):
    b, na, d = x.shape
    n_lat = na - CTX_LEN
    cap_c = EC_CAPACITY_FACTOR * CTX_LEN // N_EXPERTS
    cap_l = EC_CAPACITY_FACTOR * n_lat // N_EXPERTS
    h, aff = _router(x, mod, router_w, router_b)
    sel, _ = _select(aff, ((0, CTX_LEN, cap_c), (CTX_LEN, na, cap_l)))
    idx_c = jnp.argsort(1 - sel[:, :, :CTX_LEN], axis=-1, stable=True)[..., :cap_c]
    idx_l = jnp.argsort(1 - sel[:, :, CTX_LEN:], axis=-1, stable=True)[..., :cap_l] + CTX_LEN
    idx = jnp.concatenate([idx_c, idx_l], axis=-1).astype(jnp.int32)
    tv = jnp.take_along_axis(aff, idx, axis=-1)
    flat = idx + (jnp.arange(b, dtype=jnp.int32) * na)[:, None, None]
    flat = jnp.transpose(flat, (1, 0, 2)).reshape(N_EXPERTS, -1)
    tv = jnp.transpose(tv, (1, 0, 2)).reshape(N_EXPERTS, -1, 1)
    per_tile = jnp.sum(sel.reshape(b, N_EXPERTS, na // TOKEN_TILE, TOKEN_TILE), axis=-1)
    per_tile = jnp.transpose(per_tile, (1, 0, 2)).reshape(N_EXPERTS, -1)
    starts = jnp.concatenate([jnp.zeros((N_EXPERTS, 1), jnp.int32), jnp.cumsum(per_tile, axis=1, dtype=jnp.int32)], axis=1)
    xs = _gather_rows(h.reshape(b * na, d // LANES, LANES), flat)
    y = _expert_ffn(xs, w_gate, w_up, w_down, tv, layer)
    return _combine_residual_ln(y, flat, starts, x, mod, ln_w, ln_b, alpha)


def _dwconv_kernel(silu, x_ref, w_ref, b_ref, o_ref):
    x = x_ref[0]
    na = x.shape[0]
    row = lax.broadcasted_iota(jnp.int32, x.shape, 0)
    first = (row == 0) | (row == CTX_LEN)
    last = (row == CTX_LEN - 1) | (row == na - 1)
    prev = jnp.where(first, 0.0, pltpu.roll(x, 1, axis=0))
    nxt = jnp.where(last, 0.0, pltpu.roll(x, na - 1, axis=0))
    y = w_ref[0:1, :] * prev + w_ref[1:2, :] * x + w_ref[2:3, :] * nxt + b_ref[...]
    if silu:
        y = y * jax.nn.sigmoid(y)
    o_ref[0, 0] = y


def _segment_dwconv(p, w, bias, groups, silu):
    b, na, c = p.shape
    per = c // groups // LANES
    return pl.pallas_call(
        functools.partial(_dwconv_kernel, silu),
        out_shape=jax.ShapeDtypeStruct((groups, b, na, c // groups), F32),
        grid=(b, c // LANES),
        in_specs=[pl.BlockSpec((1, na, LANES), lambda bi, j: (bi, 0, j)),
                  pl.BlockSpec((SHORT_CONV, LANES), lambda bi, j: (0, j)),
                  pl.BlockSpec((1, LANES), lambda bi, j: (0, j))],
        out_specs=pl.BlockSpec((1, 1, na, LANES), lambda bi, j: (j // per, bi, 0, j % per)),
        compiler_params=_cparams(("parallel", "parallel"), VMEM_LIMIT),
        name="short_conv",
    )(p, w, bias.reshape(1, c))


def _rope_tables(n_lat):
    rows = n_lat // GRID_W
    row = jnp.repeat(jnp.arange(rows, dtype=F32), GRID_W)
    col = (jnp.arange(n_lat) % GRID_W).astype(F32)
    nf = ATTN_HEAD_DIM // 4
    inv = ROPE_THETA ** (-jnp.arange(nf, dtype=F32) / nf)
    ar = row[:, None] * inv
    ac = col[:, None] * inv
    ang = jnp.concatenate([ar, ar, ac, ac], axis=-1)
    return jnp.cos(ang), jnp.sin(ang)


def _mixing(x, mod, w_in, mlstm_conv_w, mlstm_conv_b, mlstm_gate_b, mlstm_norm_w, attn_q_norm_w,
            attn_k_norm_w, hyena_conv_w, hyena_conv_b, hyena_filter, hyena_skip, with_ctx_out):
    b, na, _ = x.shape
    n_lat = na - CTX_LEN
    mqk, mv, mo, aq, ak, av, hy, mg = _in_projection(x, mod, w_in)
    mg = mg[..., :MLSTM_GATES]

    qk = _segment_dwconv(mqk, _pad_heads(mlstm_conv_w), _pad_heads(mlstm_conv_b), groups=1,
                         silu=True)[0]
    g = mg.reshape(b, na, N_DIR, 2, MLSTM_HEADS) + mlstm_gate_b
    g = jnp.stack([g[:, :, :, 0], jax.nn.log_sigmoid(g[:, :, :, 1])], axis=3)
    gcols = jnp.transpose(g, (2, 0, 1, 3, 4)).reshape(N_DIR, b, na, 2 * MLSTM_HEADS)
    grows = jnp.transpose(gcols, (0, 1, 3, 2))
    hm = _mlstm_scan(qk, mv, grows, gcols)

    q, k, v = _qkv_prep(aq, ak, av, attn_q_norm_w, attn_k_norm_w)
    ya = _attention(q, k, v)

    hv, hx1, hx2 = _segment_dwconv(hy, hyena_conv_w, hyena_conv_b, groups=3, silu=False)
    yh = _hyena_latent(hv[:, CTX_LEN:], hx1[:, CTX_LEN:], hx2[:, CTX_LEN:],
                       *_hyena_taps(n_lat, *hyena_filter), hyena_skip)
    if with_ctx_out:
        yh_c = _hyena_context(hv[:, :CTX_LEN], hx1[:, :CTX_LEN], hx2[:, :CTX_LEN],
                              *_hyena_taps(CTX_LEN, *hyena_filter), hyena_skip)
    else:
        yh_c = jnp.zeros((b, CTX_LEN, D_HYENA), F32)
    yh = jnp.concatenate([yh_c, yh], axis=1)
    return hm, mo, ya, yh


def kernel(x, c, ctx, c_ctx, w_mod, b_mod, w_in, mlstm_conv_w, mlstm_conv_b, mlstm_gate_b, mlstm_norm_w, attn_q_norm_w, attn_k_norm_w, hyena_conv_w, hyena_conv_b, hyena_f_w1, hyena_f_b1, hyena_f_freq, hyena_f_w2, hyena_f_b2, hyena_f_w3, hyena_skip, w_out, ln_mix_w, ln_mix_b, router_w, router_b, exp_w_gate, exp_w_up, exp_w_down, ln_ffn_w, ln_ffn_b):
    bsz, seq, d = x.shape
    assert d == D_MODEL and ctx.shape[1] == CTX_LEN == TOKEN_TILE and seq == FFT_R * FFT_R // 2
    alpha = (2.0 * DEPTH) ** 0.25
    xa = jnp.concatenate([ctx, x], axis=1)
    crows = jnp.concatenate([c, jnp.broadcast_to(c_ctx, (8 - bsz, d))], axis=0)
    for l in range(DEPTH):
        last = l == DEPTH - 1
        m = _modulation(crows, w_mod, b_mod[l], l).reshape(8, N_MOD, d)
        mod = jnp.stack([jnp.broadcast_to(m[bsz], (bsz, N_MOD, d)), m[:bsz]], axis=1)
        hm, mo, ya, yh = _mixing(
            xa, mod, w_in[l], mlstm_conv_w[l], mlstm_conv_b[l], mlstm_gate_b[l], mlstm_norm_w[l],
            attn_q_norm_w[l], attn_k_norm_w[l], hyena_conv_w[l], hyena_conv_b[l],
            (hyena_f_w1[l], hyena_f_b1[l], hyena_f_freq[l], hyena_f_w2[l], hyena_f_b2[l], hyena_f_w3[l]),
            hyena_skip[l], not last)
        xa = _out_projection(hm, mo, mlstm_norm_w[l], ya, yh, xa, mod, w_out[l].astype(BF16),
                             ln_mix_w[l], ln_mix_b[l], alpha)
        xa = _moe_residual_ln(xa, mod, router_w[l], router_b[l], exp_w_gate, exp_w_up, exp_w_down, l,
                              ln_ffn_w[l], ln_ffn_b[l], alpha)
    return xa[:, CTX_LEN:]
```

```python
import functools
import math

import numpy as np
import jax
import jax.numpy as jnp
from jax import lax
from jax.experimental import pallas as pl
from jax.experimental.pallas import tpu as pltpu

F32 = jnp.float32
BF16 = jnp.bfloat16
HI = lax.Precision.HIGHEST

D_MODEL = 1024
DEPTH = 2
GRID_W = 64
CTX_LEN = 256
N_DIR = 2
SHORT_CONV = 3

D_MLSTM = 256
MLSTM_HEAD_DIM = 64
MLSTM_HEADS = 4
MLSTM_GATES = 16
MLSTM_CHUNK = 128

D_ATTN = 512
ATTN_HEAD_DIM = 64
ATTN_HEADS = 8
ATTN_KV_HEADS = 2
ATTN_GROUP = 4
D_KV = 128
ROPE_THETA = 10000.0

D_HYENA = 256
HYENA_ORDER = 2
HYENA_BANDS = 16
HYENA_FAST_DECAY = 0.3
HYENA_SLOW_DECAY = 1.5
HYENA_DECAY_TARGET = 1e-2
HYENA_WINDOW_SHIFT = 0.05

N_IN = 4 * D_MLSTM + MLSTM_GATES + D_ATTN + 2 * D_KV + 3 * D_HYENA
N_EXPERTS = 16
EC_CAPACITY_FACTOR = 2
D_FF_EXPERT = 2816
N_MOD = 6
LN_EPS = 1e-5
RMS_EPS = 1e-6

LANES = 128
TOKEN_TILE = 256
FFT_R = 128
FFT_GROUP = 8
VMEM_LIMIT = 56 * 1024 * 1024


def _cparams(sem, vmem=None):
    return pltpu.CompilerParams(dimension_semantics=sem, vmem_limit_bytes=vmem)


def _mod_kernel(c_ref, w_ref, b_ref, o_ref):
    cs = c_ref[...]
    cs = cs * jax.nn.sigmoid(cs)
    o_ref[...] = jnp.dot(cs, w_ref[0], precision=HI, preferred_element_type=F32) + b_ref[...]


def _modulation(crows, w_mod, b_mod, layer):
    rows, d = crows.shape
    n = w_mod.shape[2]
    tn = 1024
    return pl.pallas_call(
        _mod_kernel,
        out_shape=jax.ShapeDtypeStruct((rows, n), F32),
        grid=(n // tn,),
        in_specs=[pl.BlockSpec((rows, d), lambda j: (0, 0)),
                  pl.BlockSpec((1, d, tn), lambda j: (layer, 0, j)),
                  pl.BlockSpec((1, tn), lambda j: (0, j))],
        out_specs=pl.BlockSpec((rows, tn), lambda j: (0, j)),
        compiler_params=_cparams(("parallel",)),
        name="adaln_mod",
    )(crows, w_mod, b_mod.reshape(1, n))


def _ln_rows(x):
    mu = jnp.mean(x, axis=-1, keepdims=True)
    xc = x - mu
    var = jnp.mean(xc * xc, axis=-1, keepdims=True)
    return xc * lax.rsqrt(var + LN_EPS)


def _mod_spec():
    return pl.BlockSpec((1, 1, N_MOD, D_MODEL), lambda b, i: (b, jnp.minimum(i, 1), 0, 0))


HEAD_DIM = 64
_C_MG = 4 * D_MLSTM
_C_AQ = _C_MG + MLSTM_GATES
IN_GROUPS = (("mqk", 0, 2 * D_MLSTM, True), ("mv", 2 * D_MLSTM, D_MLSTM, True), ("mo", 3 * D_MLSTM, D_MLSTM, False),
             ("aq", _C_AQ, D_ATTN, True), ("ak", _C_AQ + D_ATTN, D_KV, True), ("av", _C_AQ + D_ATTN + D_KV, D_KV, True),
             ("hy", _C_AQ + D_ATTN + 2 * D_KV, 3 * D_HYENA, False), ("mg", _C_MG, MLSTM_GATES, False))


def _group_width(src_width, head_padded):
    return src_width // HEAD_DIM * LANES if head_padded else -(-src_width // LANES) * LANES


def _pad_heads(a, axis=-1):
    axis = axis % a.ndim
    nh = a.shape[axis] // HEAD_DIM
    a = a.reshape(a.shape[:axis] + (nh, HEAD_DIM) + a.shape[axis + 1:])
    pad = [(0, 0)] * a.ndim
    pad[axis + 1] = (0, LANES - HEAD_DIM)
    a = jnp.pad(a, pad)
    return a.reshape(a.shape[:axis] + (nh * LANES,) + a.shape[axis + 2:])


def _inproj_kernel(x_ref, mod_ref, w_ref, *o_refs):
    y = _ln_rows(x_ref[0])
    h = (y * (1.0 + mod_ref[0, 0, 1:2, :]) + mod_ref[0, 0, 0:1, :]).astype(BF16)
    off = 0
    for o_ref in o_refs:
        width = o_ref.shape[2]
        o_ref[0] = jnp.dot(h, w_ref[:, off:off + width], preferred_element_type=F32)
        off += width


def _in_projection(x, mod, w_in):
    b, na, d = x.shape
    cols = []
    for _, start, src, head_padded in IN_GROUPS:
        wg = w_in[:, start:start + src]
        if head_padded:
            wg = _pad_heads(wg)
        elif src % LANES:
            wg = jnp.pad(wg, ((0, 0), (0, LANES - src % LANES)))
        cols.append(wg)
    w = jnp.concatenate(cols, axis=1).astype(BF16)
    n = w.shape[1]
    widths = [_group_width(src, hp) for _, _, src, hp in IN_GROUPS]
    return pl.pallas_call(
        _inproj_kernel,
        out_shape=[jax.ShapeDtypeStruct((b, na, width), F32) for width in widths],
        grid=(b, na // TOKEN_TILE),
        in_specs=[pl.BlockSpec((1, TOKEN_TILE, d), lambda bi, i: (bi, i, 0)),
                  _mod_spec(),
                  pl.BlockSpec((d, n), lambda bi, i: (0, 0))],
        out_specs=[pl.BlockSpec((1, TOKEN_TILE, width), lambda bi, i: (bi, i, 0)) for width in widths],
        compiler_params=_cparams(("parallel", "parallel"), VMEM_LIMIT),
        name="in_proj",
    )(x, mod, w)


def _compact_heads(x):
    tiles = [x[:, j * LANES:(j + 1) * LANES] for j in range(x.shape[1] // LANES)]
    return jnp.concatenate([tiles[j] + pltpu.roll(tiles[j + 1], HEAD_DIM, axis=1) for j in range(0, len(tiles), 2)],
                           axis=1)


def _outproj_kernel(alpha, hm0_ref, hm1_ref, mo_ref, nw_ref, seg_ref, ya_ref, yh_ref, x_ref, mod_ref, w_ref,
                    lw_ref, lb_ref, o_ref):
    hm = _compact_heads(hm0_ref[0, 0] + hm1_ref[0, 0])
    ss = jnp.dot(hm * hm, seg_ref[...], precision=HI, preferred_element_type=F32)
    hn = hm * lax.rsqrt(ss * (1.0 / MLSTM_HEAD_DIM) + RMS_EPS) * nw_ref[...]
    ym = jax.nn.sigmoid(mo_ref[0]) * hn
    mix = jnp.dot(ym.astype(BF16), w_ref[0:D_MLSTM, :], preferred_element_type=F32)
    mix = mix + jnp.dot(_compact_heads(ya_ref[0]).astype(BF16), w_ref[D_MLSTM:D_MLSTM + D_ATTN, :],
                        preferred_element_type=F32)
    mix = mix + jnp.dot(yh_ref[0].astype(BF16), w_ref[D_MLSTM + D_ATTN:, :], preferred_element_type=F32)
    r = alpha * x_ref[0] + mod_ref[0, 0, 2:3, :] * mix
    o_ref[0] = _ln_rows(r) * lw_ref[...] + lb_ref[...]


def _out_projection(hm, mo, norm_w, ya, yh, x, mod, w_bf16, ln_w, ln_b, alpha):
    b, na, d = x.shape
    dm = mo.shape[2]
    head = np.arange(dm) // MLSTM_HEAD_DIM
    seg = jnp.asarray((head[:, None] == head[None, :]).astype(np.float32))

    def tile(width):
        return pl.BlockSpec((1, TOKEN_TILE, width), lambda bi, i: (bi, i, 0))

    def hm_spec(direction):
        return pl.BlockSpec((1, 1, TOKEN_TILE, hm.shape[3]), lambda bi, i: (direction, bi, i, 0))

    vec = pl.BlockSpec((1, d), lambda bi, i: (0, 0))
    return pl.pallas_call(
        functools.partial(_outproj_kernel, alpha),
        out_shape=jax.ShapeDtypeStruct((b, na, d), F32),
        grid=(b, na // TOKEN_TILE),
        in_specs=[hm_spec(0), hm_spec(1), tile(dm), pl.BlockSpec((1, dm), lambda bi, i: (0, 0)),
                  pl.BlockSpec((dm, dm), lambda bi, i: (0, 0)), tile(ya.shape[2]), tile(yh.shape[2]), tile(d),
                  _mod_spec(), pl.BlockSpec((d, d), lambda bi, i: (0, 0)), vec, vec],
        out_specs=tile(d),
        compiler_params=_cparams(("parallel", "parallel"), VMEM_LIMIT),
        name="out_proj_ln",
    )(hm, hm, mo, norm_w.reshape(1, dm), seg, ya, yh, x, mod, w_bf16, ln_w.reshape(1, d), ln_b.reshape(1, d))


def _mlstm_kernel(qk_ref, v_ref, gr_ref, gc_ref, o_ref, c_sc, n_sc, m_sc):
    t = MLSTM_CHUNK
    nh = MLSTM_HEADS
    d = pl.program_id(0)

    @pl.when(pl.program_id(2) == 0)
    def _():
        c_sc[...] = jnp.zeros_like(c_sc)
        n_sc[...] = jnp.zeros_like(n_sc)
        m_sc[...] = jnp.zeros_like(m_sc)

    row = lax.broadcasted_iota(jnp.int32, (t, t), 0)
    col = lax.broadcasted_iota(jnp.int32, (t, t), 1)
    mask = jnp.where(d == 0, col - row, row - col) <= 0
    maskf = mask.astype(F32)
    grows = gr_ref[0, 0]
    gcols = gc_ref[0, 0]
    cum_cols = jnp.dot(maskf, gcols, precision=HI, preferred_element_type=F32)
    cum_rows = lax.dot_general(grows, maskf, (((1,), (1,)), ((), ())), precision=HI,
                               preferred_element_type=F32)
    lane8 = lax.broadcasted_iota(jnp.int32, (t, 2 * nh), 1)

    def column(a, idx):
        return jnp.sum(jnp.where(lane8 == idx, a, 0.0), axis=1, keepdims=True)

    dh = LANES
    for h in range(nh):
        qc = qk_ref[0, :, h * dh:(h + 1) * dh] * (MLSTM_HEAD_DIM ** -0.5)
        kc = qk_ref[0, :, (nh + h) * dh:(nh + h + 1) * dh]
        vc = v_ref[0, :, h * dh:(h + 1) * dh]
        ic_row = grows[h:h + 1, :]
        ic_col = column(gcols, h)
        bcol = column(cum_cols, nh + h)
        brow = cum_rows[nh + h:nh + h + 1, :]
        tot = jnp.sum(grows[nh + h:nh + h + 1, :], axis=1, keepdims=True)
        m0 = m_sc[h:h + 1, 0:1]
        log_inter = bcol + m0
        log_intra = jnp.where(mask, bcol - brow + ic_row, -jnp.inf)
        mrow = jnp.maximum(log_inter, jnp.max(log_intra, axis=1, keepdims=True))
        w_inter = jnp.exp(log_inter - mrow)
        qb = qc.astype(BF16)
        vb = vc.astype(BF16)
        scores = lax.dot_general(qb, kc.astype(BF16), (((1,), (1,)), ((), ())),
                                 preferred_element_type=F32) * jnp.exp(log_intra - mrow)
        ct = c_sc[h]
        n0 = n_sc[h]
        num = (w_inter * jnp.dot(qb, ct.astype(BF16), preferred_element_type=F32)
               + jnp.dot(scores.astype(BF16), vb, preferred_element_type=F32))
        den = w_inter * jnp.sum(qc * n0, axis=1, keepdims=True) + jnp.sum(scores, axis=1, keepdims=True)
        o_ref[0, 0, :, h * dh:(h + 1) * dh] = num / jnp.maximum(jnp.abs(den), jnp.exp(-mrow))
        log_src = tot - bcol + ic_col
        m_new = jnp.maximum(tot + m0, jnp.max(log_src, axis=0, keepdims=True))
        wk = jnp.exp(log_src - m_new) * kc
        decay = jnp.exp(tot + m0 - m_new)
        c_sc[h] = decay * ct + lax.dot_general(wk.astype(BF16), vb, (((0,), (0,)), ((), ())),
                                               preferred_element_type=F32)
        n_sc[h] = decay * n0 + jnp.sum(wk, axis=0, keepdims=True)
        m_sc[h:h + 1, :] = jnp.broadcast_to(m_new, (1, 128))


def _mlstm_scan(qk, v, grows, gcols):
    b, na, dv = v.shape
    nh, dh = MLSTM_HEADS, LANES
    t = MLSTM_CHUNK
    nc = na // t
    nctx = CTX_LEN // t

    def chunk(d, c):
        rev = jnp.where(c < nctx, nctx - 1 - c, nc + nctx - 1 - c)
        return jnp.where(d == 0, c, rev)

    return pl.pallas_call(
        _mlstm_kernel,
        out_shape=jax.ShapeDtypeStruct((N_DIR, b, na, dv), F32),
        grid=(N_DIR, b, nc),
        in_specs=[pl.BlockSpec((1, t, 2 * dv), lambda d, bi, c: (bi, chunk(d, c), 0)),
                  pl.BlockSpec((1, t, dv), lambda d, bi, c: (bi, chunk(d, c), 0)),
                  pl.BlockSpec((1, 1, 2 * nh, t), lambda d, bi, c: (d, bi, 0, chunk(d, c))),
                  pl.BlockSpec((1, 1, t, 2 * nh), lambda d, bi, c: (d, bi, chunk(d, c), 0))],
        out_specs=pl.BlockSpec((1, 1, t, dv), lambda d, bi, c: (d, bi, chunk(d, c), 0)),
        scratch_shapes=[pltpu.VMEM((nh, dh, dh), F32), pltpu.VMEM((nh, 1, dh), F32), pltpu.VMEM((8, 128), F32)],
        compiler_params=_cparams(("parallel", "parallel", "arbitrary")),
        name="mlstm_scan",
    )(qk, v, grows, gcols)


ATTN_TQ = 256
LOG2E = 1.4426950408889634


def _qkv_prep_kernel(aq_ref, ak_ref, av_ref, cos_ref, sin_ref, qw_ref, kw_ref, q_ref, k_ref, v_ref):
    lane = lax.broadcasted_iota(jnp.int32, (aq_ref.shape[1], LANES), 1)
    head_lane = lane < ATTN_HEAD_DIM
    first_half = (lane % (ATTN_HEAD_DIM // 2)) < (ATTN_HEAD_DIM // 4)
    cos = cos_ref[...]
    sin = sin_ref[...]
    quarter = ATTN_HEAD_DIM // 4

    def norm_rope(x, w):
        y = x * lax.rsqrt(jnp.sum(x * x, axis=1, keepdims=True) * (1.0 / ATTN_HEAD_DIM) + RMS_EPS) * w
        rot = jnp.where(first_half, -pltpu.roll(y, LANES - quarter, axis=1), pltpu.roll(y, quarter, axis=1))
        return jnp.where(head_lane, y * cos + rot * sin, 0.0)

    for h in range(ATTN_HEADS):
        q_ref[0, h] = norm_rope(aq_ref[0, :, h * LANES:(h + 1) * LANES], qw_ref[...]).astype(BF16)
    for h in range(ATTN_KV_HEADS):
        k_ref[0, h] = norm_rope(ak_ref[0, :, h * LANES:(h + 1) * LANES], kw_ref[...]).astype(BF16)
        v_ref[0, h] = jnp.where(lane == ATTN_HEAD_DIM, 1.0, av_ref[0, :, h * LANES:(h + 1) * LANES]).astype(BF16)


def _qkv_prep(aq, ak, av, q_norm_w, k_norm_w):
    b, na, _ = aq.shape
    cos, sin = _rope_tables(na - CTX_LEN)
    pad = ((CTX_LEN, 0), (0, LANES - ATTN_HEAD_DIM))
    cos = jnp.pad(cos, pad, constant_values=1.0)
    sin = jnp.pad(sin, pad)
    qw = jnp.pad(q_norm_w * (ATTN_HEAD_DIM ** -0.5 * LOG2E), (0, LANES - ATTN_HEAD_DIM)).reshape(1, LANES)
    kw = jnp.pad(k_norm_w, (0, LANES - ATTN_HEAD_DIM)).reshape(1, LANES)
    tok = lambda width: pl.BlockSpec((1, TOKEN_TILE, width), lambda bi, i: (bi, i, 0))
    tab = pl.BlockSpec((TOKEN_TILE, LANES), lambda bi, i: (i, 0))
    vec = pl.BlockSpec((1, LANES), lambda bi, i: (0, 0))
    heads = lambda nh: pl.BlockSpec((1, nh, TOKEN_TILE, LANES), lambda bi, i: (bi, 0, i, 0))
    return pl.pallas_call(
        _qkv_prep_kernel,
        out_shape=(jax.ShapeDtypeStruct((b, ATTN_HEADS, na, LANES), BF16),
                   jax.ShapeDtypeStruct((b, ATTN_KV_HEADS, na, LANES), BF16),
                   jax.ShapeDtypeStruct((b, ATTN_KV_HEADS, na, LANES), BF16)),
        grid=(b, na // TOKEN_TILE),
        in_specs=[tok(aq.shape[2]), tok(ak.shape[2]), tok(av.shape[2]), tab, tab, vec, vec],
        out_specs=(heads(ATTN_HEADS), heads(ATTN_KV_HEADS), heads(ATTN_KV_HEADS)),
        compiler_params=_cparams(("parallel", "parallel")),
        name="qkv_prep",
    )(aq, ak, av, cos, sin, qw, kw)


def _attn_kernel(q_ref, k_ref, v_ref, o_ref, s_ref, p_ref):
    def attend(nk):
        s_ref[:, 0:nk] = lax.dot_general(q_ref[0, 0], k_ref[0, 0, 0:nk, :], (((1,), (1,)), ((), ())),
                                         preferred_element_type=F32)
        m = jnp.max(s_ref[:, 0:nk], axis=1, keepdims=True)
        p_ref[:, 0:nk] = jnp.exp2(s_ref[:, 0:nk] - m).astype(BF16)
        acc = jnp.dot(p_ref[:, 0:nk], v_ref[0, 0, 0:nk, :], preferred_element_type=F32)
        lane = lax.broadcasted_iota(jnp.int32, acc.shape, 1)
        o_ref[0] = jnp.where(lane < ATTN_HEAD_DIM, acc / acc[:, ATTN_HEAD_DIM:ATTN_HEAD_DIM + 1], 0.0)

    is_ctx = pl.program_id(2) < CTX_LEN // ATTN_TQ
    pl.when(is_ctx)(lambda: attend(CTX_LEN))
    pl.when(jnp.logical_not(is_ctx))(lambda: attend(k_ref.shape[2]))


def _attention(q, k, v):
    b, nh, na, _ = q.shape
    group = nh // k.shape[1]
    kv_spec = pl.BlockSpec((1, 1, na, LANES), lambda bi, h, i: (bi, h // group, 0, 0))
    return pl.pallas_call(
        _attn_kernel,
        out_shape=jax.ShapeDtypeStruct((b, na, nh * LANES), F32),
        grid=(b, nh, na // ATTN_TQ),
        in_specs=[pl.BlockSpec((1, 1, ATTN_TQ, LANES), lambda bi, h, i: (bi, h, i, 0)), kv_spec, kv_spec],
        out_specs=pl.BlockSpec((1, ATTN_TQ, LANES), lambda bi, h, i: (bi, i, h)),
        scratch_shapes=[pltpu.VMEM((ATTN_TQ, na), F32), pltpu.VMEM((ATTN_TQ, na), BF16)],
        compiler_params=_cparams(("parallel", "parallel", "parallel"), VMEM_LIMIT),
        name="attention",
    )(q, k, v)


def _dft_tables(t2_len):
    r = FFT_R
    n = r * r
    idx = np.arange(r, dtype=np.float64)
    kb = idx[None, :, None]
    t1 = idx[:, None, None]
    t2 = np.arange(t2_len, dtype=np.float64)[None, None, :]
    ang = -2.0 * np.pi * (t2 * kb / r + t1 * kb / n)
    f1 = np.concatenate([np.cos(ang), np.sin(ang)], axis=1)
    ang2 = -2.0 * np.pi * np.outer(idx, idx) / r
    f2 = np.stack([np.cos(ang2), np.sin(ang2)])
    return f1.astype(np.float32), f2.astype(np.float32)


def _idft_table(t2_len):
    r = FFT_R
    n = r * r
    t1 = np.arange(r, dtype=np.float64)[:, None, None]
    t2 = np.arange(t2_len, dtype=np.float64)[None, :, None]
    kb = np.arange(r, dtype=np.float64)[None, None, :]
    ang = 2.0 * np.pi * (t2 * kb / r + t1 * kb / n)
    return (np.stack([np.cos(ang), np.sin(ang)], axis=1) / n).astype(np.float32)


def _dot_hi(table, x):
    return jnp.dot(table, x.astype(BF16), preferred_element_type=F32)


def _stage_block(block, sc):
    rows = block.shape[0]
    for t in range(rows):
        for h in range(sc.shape[0]):
            sc[h, pl.ds(t * FFT_GROUP, FFT_GROUP), :] = block[t, :, h * LANES:(h + 1) * LANES]


def _middle_rows(sc, i):
    rows = sc.shape[1] // FFT_GROUP
    return jnp.concatenate([sc[h, pl.ds(i, rows, stride=FFT_GROUP), :] for h in range(sc.shape[0])], axis=1)


def _stage_scratch(rows, c):
    return pltpu.VMEM((c // LANES, rows * FFT_GROUP, LANES), F32)


def _fft1_kernel(f_ref, z_ref, g_ref, z_sc):
    _stage_block(z_ref.at[0], z_sc)
    for i in range(FFT_GROUP):
        g_ref[0, i] = _dot_hi(f_ref[i], _middle_rows(z_sc, i))


def _fft_stage1(z4, f1):
    b, t2_len, r, c = z4.shape
    g = FFT_GROUP
    return pl.pallas_call(
        _fft1_kernel,
        out_shape=jax.ShapeDtypeStruct((b, r, 2 * r, c), F32),
        grid=(b, r // g),
        in_specs=[pl.BlockSpec((g, 2 * r, t2_len), lambda bi, j: (j, 0, 0)),
                  pl.BlockSpec((1, t2_len, g, c), lambda bi, j: (bi, 0, j, 0))],
        out_specs=pl.BlockSpec((1, g, 2 * r, c), lambda bi, j: (bi, j, 0, 0)),
        scratch_shapes=[_stage_scratch(t2_len, c)],
        compiler_params=_cparams(("parallel", "parallel")),
        name="fft_stage1",
    )(f1, z4)


def _fft2_kernel(with_filter, f_ref, gr_ref, gi_ref, aux_ref, h_ref, gr_sc, gi_sc):
    fr = f_ref[0]
    fi = f_ref[1]
    _stage_block(gr_ref.at[0], gr_sc)
    _stage_block(gi_ref.at[0], gi_sc)
    for i in range(FFT_GROUP):
        gr = _middle_rows(gr_sc, i).astype(BF16)
        gi = _middle_rows(gi_sc, i).astype(BF16)
        xr = _dot_hi(fr, gr) - _dot_hi(fi, gi)
        xi = _dot_hi(fr, gi) + _dot_hi(fi, gr)
        if with_filter:
            tr = aux_ref[0, 0, i]
            ti = aux_ref[0, 1, i]
            yr = xr * tr - xi * ti
            yi = xr * ti + xi * tr
            xr = _dot_hi(fr, yr) + _dot_hi(fi, yi)
            xi = _dot_hi(fr, yi) - _dot_hi(fi, yr)
        else:
            xr = xr * aux_ref[0]
            xi = xi * aux_ref[0]
        h_ref[0, 0, i] = xr
        h_ref[0, 1, i] = xi


def _fft_stage2(g, f2, tf=None, scale=None):
    b, r, _, c = g.shape
    grp = FFT_GROUP
    nblk = r // grp
    in_specs = [pl.BlockSpec((2, r, r), lambda bi, j: (0, 0, 0)),
                pl.BlockSpec((1, r, grp, c), lambda bi, j: (bi, 0, j, 0)),
                pl.BlockSpec((1, r, grp, c), lambda bi, j: (bi, 0, nblk + j, 0))]
    args = [f2, g, g]
    if tf is not None:
        in_specs.append(pl.BlockSpec((1, 2, grp, r, c), lambda bi, j: (0, 0, j, 0, 0)))
        args.append(tf)
    else:
        in_specs.append(pl.BlockSpec((1, 1, c), lambda bi, j: (bi, 0, 0)))
        args.append(scale)
    return pl.pallas_call(
        functools.partial(_fft2_kernel, tf is not None),
        out_shape=jax.ShapeDtypeStruct((b, 2, r, r, c), F32),
        grid=(b, nblk),
        in_specs=in_specs,
        out_specs=pl.BlockSpec((1, 2, grp, r, c), lambda bi, j: (bi, 0, j, 0, 0)),
        scratch_shapes=[_stage_scratch(r, c), _stage_scratch(r, c)],
        compiler_params=_cparams(("parallel", "parallel"), VMEM_LIMIT),
        name="fft_stage2",
    )(*args)


def _ifft_kernel(e_ref, h_ref, z_ref, x_ref, skip_ref, o_ref, hr_sc, hi_sc, z_sc, x_sc, o_sc):
    _stage_block(h_ref.at[0, 0], hr_sc)
    _stage_block(h_ref.at[0, 1], hi_sc)
    _stage_block(z_ref.at[0], z_sc)
    _stage_block(x_ref.at[0], x_sc)
    rows = z_ref.shape[1]
    for i in range(FFT_GROUP):
        y = _dot_hi(e_ref[i, 0], _middle_rows(hr_sc, i)) - _dot_hi(e_ref[i, 1], _middle_rows(hi_sc, i))
        y = y * (1.0 / (FFT_R * FFT_R))
        out = _middle_rows(x_sc, i) * (y + skip_ref[...] * _middle_rows(z_sc, i))
        for h in range(o_sc.shape[0]):
            o_sc[h, pl.ds(i, rows, stride=FFT_GROUP), :] = out[:, h * LANES:(h + 1) * LANES]
    for t in range(rows):
        o_ref[0, t] = jnp.concatenate([o_sc[h, pl.ds(t * FFT_GROUP, FFT_GROUP), :] for h in range(o_sc.shape[0])],
                                      axis=1)


def _ifft_gate(h, e, z4, x4, skip):
    b, t2_len, r, c = z4.shape
    g = FFT_GROUP
    tok = pl.BlockSpec((1, t2_len, g, c), lambda bi, j: (bi, 0, j, 0))
    return pl.pallas_call(
        _ifft_kernel,
        out_shape=jax.ShapeDtypeStruct(z4.shape, F32),
        grid=(b, r // g),
        in_specs=[pl.BlockSpec((g, 2, t2_len, r), lambda bi, j: (j, 0, 0, 0)),
                  pl.BlockSpec((1, 2, r, g, c), lambda bi, j: (bi, 0, 0, j, 0)),
                  tok, tok, pl.BlockSpec((1, c), lambda bi, j: (0, 0))],
        out_specs=tok,
        scratch_shapes=[_stage_scratch(r, c), _stage_scratch(r, c), _stage_scratch(t2_len, c),
                        _stage_scratch(t2_len, c), _stage_scratch(t2_len, c)],
        compiler_params=_cparams(("parallel", "parallel"), VMEM_LIMIT),
        name="ifft_gate",
    )(e, h, z4, x4, skip.reshape(1, c))


TAPS_ROWS = 1024
FEAT_PAD = 128


def _taps_kernel(n, feat_ref, w1_ref, b1_ref, freq_ref, w2_ref, b2_ref, w3_ref, delta_ref, taps_ref, l1_ref):
    c = D_HYENA
    rows = feat_ref.shape[0]
    step = pl.program_id(0)

    @pl.when(step == 0)
    def _():
        l1_ref[...] = jnp.zeros_like(l1_ref)

    f = feat_ref[...]
    hid = jnp.sin(freq_ref[0:1, :] * (jnp.dot(f, w1_ref[...], precision=HI, preferred_element_type=F32) + b1_ref[...]))
    hid = jnp.sin(freq_ref[1:2, :] * (jnp.dot(hid, w2_ref[...], precision=HI, preferred_element_type=F32) + b2_ref[...]))
    filt = jnp.dot(hid, w3_ref[...], precision=HI, preferred_element_type=F32)
    window = jnp.exp(-f[:, 0:1] * delta_ref[...]) + HYENA_WINDOW_SHIFT
    i = step * rows + lax.broadcasted_iota(jnp.int32, (rows, c), 0)
    for o in range(HYENA_ORDER):
        fwd = filt[:, o * c:(o + 1) * c]
        bwd = filt[:, (HYENA_ORDER + o) * c:(HYENA_ORDER + o + 1) * c]
        tap = jnp.where(i < n, fwd, jnp.where(i > n, bwd, 0.0)) * window
        taps_ref[o] = tap
        l1_ref[o:o + 1, :] += jnp.sum(jnp.abs(tap), axis=0, keepdims=True)


def _hyena_taps(n, f_w1, f_b1, f_freq, f_w2, f_b2, f_w3):
    i = np.arange(2 * n)
    t = (np.where(i < n, i, 2 * n - i).astype(np.float32) / np.float32(n)).astype(np.float32)
    ang = (np.float32(2.0 * math.pi) * t[:, None]) * np.arange(1, HYENA_BANDS + 1, dtype=np.float32)
    feats = np.zeros((2 * n, FEAT_PAD), np.float32)
    feats[:, 0] = t
    feats[:, 1:1 + HYENA_BANDS] = np.cos(ang.astype(np.float64))
    feats[:, 1 + HYENA_BANDS:1 + 2 * HYENA_BANDS] = np.sin(ang.astype(np.float64))
    log_target = abs(math.log(HYENA_DECAY_TARGET))
    deltas = jnp.linspace(log_target / HYENA_SLOW_DECAY, log_target / HYENA_FAST_DECAY, D_HYENA, dtype=F32)
    hid = f_w2.shape[0]
    w1 = jnp.concatenate([f_w1, jnp.zeros((FEAT_PAD - f_w1.shape[0], hid), F32)], axis=0)
    rows = min(TAPS_ROWS, 2 * n)
    full = lambda shape: pl.BlockSpec(shape, lambda j: (0,) * len(shape))
    return pl.pallas_call(
        functools.partial(_taps_kernel, n),
        out_shape=(jax.ShapeDtypeStruct((HYENA_ORDER, 2 * n, D_HYENA), F32),
                   jax.ShapeDtypeStruct((HYENA_ORDER, D_HYENA), F32)),
        grid=(2 * n // rows,),
        in_specs=[pl.BlockSpec((rows, FEAT_PAD), lambda j: (j, 0)), full((FEAT_PAD, hid)), full((1, hid)),
                  full((2, hid)), full((hid, hid)), full((1, hid)), full(f_w3.shape), full((1, D_HYENA))],
        out_specs=(pl.BlockSpec((HYENA_ORDER, rows, D_HYENA), lambda j: (0, j, 0)), full((HYENA_ORDER, D_HYENA))),
        compiler_params=_cparams(("arbitrary",)),
        name="hyena_taps",
    )(jnp.asarray(feats), w1, f_b1.reshape(1, hid), f_freq, f_w2, f_b2.reshape(1, hid), f_w3, deltas.reshape(1, -1))


def _hyena_latent(v, x1, x2, taps, l1, skip):
    b, n, c = v.shape
    r = FFT_R
    t2_len = n // r
    f1_full, f2 = _dft_tables(r)
    f1_half = jnp.asarray(f1_full[:, :, :t2_len]).astype(BF16)
    f1_full = jnp.asarray(f1_full).astype(BF16)
    f2 = jnp.asarray(f2).astype(BF16)
    e = jnp.asarray(_idft_table(t2_len) * (r * r)).astype(BF16)
    taps4 = taps.reshape(HYENA_ORDER, r, r, c)
    tf = _fft_stage2(_fft_stage1(taps4, f1_full), f2, scale=(1.0 / l1)[:, None, :])
    z = v.reshape(b, t2_len, r, c)
    for o, gate in enumerate((x1, x2)):
        g = _fft_stage1(z, f1_half)
        h = _fft_stage2(g, f2, tf[o:o + 1])
        z = _ifft_gate(h, e, z, gate.reshape(b, t2_len, r, c), skip[o])
    return z.reshape(b, n, c)


def _hyena_ctx_kernel(n, v_ref, x1_ref, x2_ref, k_ref, skip_ref, o_ref, z_sc):
    z_sc[...] = v_ref[0]
    for o, gate_ref in enumerate((x1_ref, x2_ref)):
        def body(s, acc):
            return acc + k_ref[o, pl.ds(n - 1 - s, n), :] * z_sc[pl.ds(s, 1), :]
        conv = lax.fori_loop(0, n, body, jnp.zeros(z_sc.shape, F32))
        z_sc[...] = gate_ref[0] * (conv + skip_ref[o:o + 1, :] * z_sc[...])
    o_ref[0] = z_sc[...]


def _hyena_context(v, x1, x2, taps, l1, skip):
    b, n, c = v.shape
    cb = 128
    k2 = jnp.roll(taps, n - 1, axis=1) / l1[:, None, :]
    tok = pl.BlockSpec((1, n, cb), lambda bi, j: (bi, 0, j))
    return pl.pallas_call(
        functools.partial(_hyena_ctx_kernel, n),
        out_shape=jax.ShapeDtypeStruct((b, n, c), F32),
        grid=(b, c // cb),
        in_specs=[tok, tok, tok,
                  pl.BlockSpec((HYENA_ORDER, 2 * n, cb), lambda bi, j: (0, 0, j)),
                  pl.BlockSpec((HYENA_ORDER, cb), lambda bi, j: (0, j))],
        out_specs=tok,
        scratch_shapes=[pltpu.VMEM((n, cb), F32)],
        compiler_params=_cparams(("parallel", "parallel")),
        name="hyena_ctx",
    )(v, x1, x2, k2, skip)


def _router_kernel(x_ref, mod_ref, rw_ref, rb_ref, h_ref, aff_ref):
    y = _ln_rows(x_ref[0])
    h = y * (1.0 + mod_ref[0, 0, 4:5, :]) + mod_ref[0, 0, 3:4, :]
    for j in range(h_ref.shape[2]):
        h_ref[0, :, j, :] = h[:, j * LANES:(j + 1) * LANES]
    logits = lax.dot_general(rw_ref[...], h, (((1,), (1,)), ((), ())), precision=HI,
                             preferred_element_type=F32) + rb_ref[...]
    z = jnp.exp(logits - jnp.max(logits, axis=0, keepdims=True))
    aff_ref[0] = z / jnp.sum(z, axis=0, keepdims=True)


def _router(x, mod, router_w, router_b):
    b, na, d = x.shape
    e = router_w.shape[1]
    return pl.pallas_call(
        _router_kernel,
        out_shape=(jax.ShapeDtypeStruct((b, na, d // LANES, LANES), F32), jax.ShapeDtypeStruct((b, e, na), F32)),
        grid=(b, na // TOKEN_TILE),
        in_specs=[pl.BlockSpec((1, TOKEN_TILE, d), lambda bi, i: (bi, i, 0)),
                  _mod_spec(),
                  pl.BlockSpec((e, d), lambda bi, i: (0, 0)),
                  pl.BlockSpec((e, 1), lambda bi, i: (0, 0))],
        out_specs=(pl.BlockSpec((1, TOKEN_TILE, d // LANES, LANES), lambda bi, i: (bi, i, 0, 0)),
                   pl.BlockSpec((1, e, TOKEN_TILE), lambda bi, i: (bi, 0, i))),
        compiler_params=_cparams(("parallel", "parallel")),
        name="router",
    )(x, mod, router_w.T, router_b.reshape(e, 1))


def _prefix_count(x):
    n = x.shape[1]
    lane = lax.broadcasted_iota(jnp.int32, x.shape, 1)
    sh = 1
    while sh < n:
        x = x + jnp.where(lane >= sh, pltpu.roll(x, sh, axis=1), 0)
        sh *= 2
    return x


SELECT_BISECTIONS = 160


def _select_kernel(segments, aff_ref, sel_ref, pos_ref):
    nb, ne, _ = aff_ref.shape
    parts = [(b, s0, s1, cap) for b in range(nb) for (s0, s1, cap) in segments]

    def body(_, bounds):
        out = []
        for (b, s0, s1, cap), (lo, hi) in zip(parts, bounds):
            mid = 0.5 * (lo + hi)
            cnt = jnp.sum((aff_ref[b, :, s0:s1] >= mid).astype(jnp.int32), axis=1, keepdims=True)
            ok = cnt >= cap
            out.append((jnp.where(ok, mid, lo), jnp.where(ok, hi, mid)))
        return tuple(out)

    init = tuple((jnp.zeros((ne, 1), F32), jnp.full((ne, 1), 2.0, F32)) for _ in parts)
    bounds = lax.fori_loop(0, SELECT_BISECTIONS, body, init)
    for (b, s0, s1, cap), (lo, hi) in zip(parts, bounds):
        a = aff_ref[b, :, s0:s1]
        above = (a >= hi).astype(jnp.int32)
        tied = jnp.where(a >= lo, 1, 0) - above
        need = cap - jnp.sum(above, axis=1, keepdims=True)
        tie_rank = _prefix_count(tied) - tied
        sel = above + tied * (tie_rank < need).astype(jnp.int32)
        sel_ref[b, :, s0:s1] = sel
        pos_ref[b, :, s0:s1] = _prefix_count(sel) - sel


def _select(aff, segments):
    b, e, na = aff.shape
    blk = pl.BlockSpec((b, e, na), lambda i: (0, 0, 0))
    return pl.pallas_call(
        functools.partial(_select_kernel, segments),
        out_shape=(jax.ShapeDtypeStruct((b, e, na), jnp.int32), jax.ShapeDtypeStruct((b, e, na), jnp.int32)),
        grid=(1,),
        in_specs=[blk],
        out_specs=(blk, blk),
        compiler_params=_cparams(("arbitrary",)),
        name="expert_select",
    )(aff)


MOE_ROW_TILE = 264


def _row_tile(r):
    return next(t for t in range(MOE_ROW_TILE, 7, -8) if r % t == 0)


DMA_UNROLL = 8


def _for_each_row(tr, fn):
    def body(i, c):
        fn(i)
        return c
    lax.fori_loop(0, tr, body, 0, unroll=DMA_UNROLL)


def _gather_kernel(tr, idx_ref, h_hbm, o_ref, buf, sem):
    step = pl.program_id(0) * pl.num_programs(1) + pl.program_id(1)
    n_steps = pl.num_programs(0) * pl.num_programs(1)
    cur = step % 2

    def start_tile(t, slot):
        _for_each_row(tr, lambda i: pltpu.make_async_copy(
            h_hbm.at[pl.ds(idx_ref[t * tr + i], 1)], buf.at[slot, pl.ds(i, 1)], sem.at[slot]).start())

    @pl.when(step == 0)
    def _():
        start_tile(0, 0)

    @pl.when(step + 1 < n_steps)
    def _():
        start_tile(step + 1, 1 - cur)

    pltpu.make_async_copy(h_hbm.at[pl.ds(0, tr)], buf.at[cur], sem.at[cur]).wait()
    for j in range(buf.shape[2]):
        o_ref[0, :, j * LANES:(j + 1) * LANES] = buf[cur, :, j, :].astype(BF16)


def _gather_rows(h3, idx):
    e, r = idx.shape
    _, s, lanes = h3.shape
    tr = _row_tile(r)
    return pl.pallas_call(
        functools.partial(_gather_kernel, tr),
        out_shape=jax.ShapeDtypeStruct((e, r, s * lanes), BF16),
        grid_spec=pltpu.PrefetchScalarGridSpec(
            num_scalar_prefetch=1, grid=(e, r // tr),
            in_specs=[pl.BlockSpec(memory_space=pl.ANY)],
            out_specs=pl.BlockSpec((1, tr, s * lanes), lambda ei, j, idx_ref: (ei, j, 0)),
            scratch_shapes=[pltpu.VMEM((2, tr, s, lanes), F32), pltpu.SemaphoreType.DMA((2,))]),
        compiler_params=_cparams(("arbitrary", "arbitrary")),
        name="moe_gather",
    )(idx.reshape(-1), h3)


MOE_TF = 256
MOE_ROW_CHUNKS = 4


COMBINE_CHUNK = 64


def _expert_ffn_kernel(xs_ref, wg_ref, wu_ref, wd_ref, tv_ref, o_ref, acc_ref):
    j = pl.program_id(1)
    wg = wg_ref[0, 0].astype(BF16)
    wu = wu_ref[0, 0].astype(BF16)
    wd = wd_ref[0, 0].astype(BF16)
    rows = xs_ref.shape[1]
    rc = rows // MOE_ROW_CHUNKS

    @pl.when(j == 0)
    def _():
        acc_ref[...] = jnp.zeros_like(acc_ref)

    for ci in range(MOE_ROW_CHUNKS):
        sl = slice(ci * rc, (ci + 1) * rc)
        x = xs_ref[0, sl, :]
        g = jnp.dot(x, wg, preferred_element_type=F32)
        u = jnp.dot(x, wu, preferred_element_type=F32)
        a = (g * jax.nn.sigmoid(g) * u).astype(BF16)
        acc_ref[sl, :] += jnp.dot(a, wd, preferred_element_type=F32)

    @pl.when(j == pl.num_programs(1) - 1)
    def _():
        y = acc_ref[...] * tv_ref[0]
        for t in range(o_ref.shape[2]):
            o_ref[0, 0:rows, t, :] = y[:, t * LANES:(t + 1) * LANES]
        o_ref[0, rows:, :, :] = jnp.zeros((o_ref.shape[1] - rows,) + o_ref.shape[2:], F32)


def _expert_ffn(xs, w_gate, w_up, w_down, tv, layer):
    e, r, d = xs.shape
    f = w_gate.shape[3]
    tf = MOE_TF
    return pl.pallas_call(
        _expert_ffn_kernel,
        out_shape=jax.ShapeDtypeStruct((e, r + COMBINE_CHUNK, d // LANES, LANES), F32),
        grid=(e, f // tf),
        in_specs=[pl.BlockSpec((1, r, d), lambda ei, j: (ei, 0, 0)),
                  pl.BlockSpec((1, 1, d, tf), lambda ei, j: (layer, ei, 0, j)),
                  pl.BlockSpec((1, 1, d, tf), lambda ei, j: (layer, ei, 0, j)),
                  pl.BlockSpec((1, 1, tf, d), lambda ei, j: (layer, ei, j, 0)),
                  pl.BlockSpec((1, r, 1), lambda ei, j: (ei, 0, 0))],
        out_specs=pl.BlockSpec((1, r + COMBINE_CHUNK, d // LANES, LANES), lambda ei, j: (ei, 0, 0, 0)),
        scratch_shapes=[pltpu.VMEM((r, d), F32)],
        compiler_params=_cparams(("parallel", "arbitrary"), VMEM_LIMIT),
        name="expert_ffn",
    )(xs, w_gate, w_up, w_down, tv)


COMBINE_UNROLL = 4
COMBINE_PAD = 8


def _combine_kernel(alpha, n_exp, r, n_tiles, idx_ref, st_ref, y_hbm, x_ref, mod_ref, lw_ref, lb_ref, o_ref,
                    acc, head, tail, head_sem, tail_sem):
    g = pl.program_id(0) * pl.num_programs(1) + pl.program_id(1)
    tile_base = g * TOKEN_TILE
    ch = COMBINE_CHUNK
    cur = g % 2

    def run(e, tile):
        s0 = st_ref[e * (n_tiles + 1) + tile]
        return s0, st_ref[e * (n_tiles + 1) + tile + 1] - s0

    def head_copy(e, tile, slot):
        s0, cnt = run(e, tile)
        return pltpu.make_async_copy(y_hbm.at[e, pl.ds(s0, ch)], head.at[slot, e, pl.ds(0, ch)], head_sem.at[slot, e]), cnt

    def start_heads(tile, slot):
        for e in range(n_exp):
            copy, cnt = head_copy(e, tile, slot)
            pl.when(cnt > 0)(copy.start)

    @pl.when(g == 0)
    def _():
        head[...] = jnp.zeros_like(head)
        tail[...] = jnp.zeros_like(tail)
        start_heads(0, 0)

    @pl.when(g + 1 < n_tiles)
    def _():
        start_heads(g + 1, 1 - cur)

    acc[...] = jnp.zeros_like(acc)

    def add_rows(src, first, count):
        def body(it, carry):
            for u in range(COMBINE_UNROLL):
                j = it * COMBINE_UNROLL + u
                t = jnp.where(j < count, idx_ref[first + j] - tile_base, TOKEN_TILE)
                acc[t] = acc[t] + src[j]
            return carry
        lax.fori_loop(0, (count + COMBINE_UNROLL - 1) // COMBINE_UNROLL, body, 0)

    for e in range(n_exp):
        s0, cnt = run(e, g)
        copy, _ = head_copy(e, g, cur)
        pl.when(cnt > 0)(copy.wait)
        add_rows(head.at[cur, e], e * r + s0, jnp.minimum(cnt, ch))
        for c in range(1, TOKEN_TILE // ch):
            @pl.when(cnt > c * ch)
            def _(e=e, c=c, s0=s0, cnt=cnt):
                more = pltpu.make_async_copy(y_hbm.at[e, pl.ds(s0 + c * ch, ch)], tail.at[pl.ds(0, ch)], tail_sem)
                more.start()
                more.wait()
                add_rows(tail, e * r + s0 + c * ch, jnp.minimum(cnt - c * ch, ch))

    f = jnp.concatenate([acc[0:TOKEN_TILE, j, :] for j in range(acc.shape[1])], axis=1)
    res = alpha * x_ref[0] + mod_ref[0, 0, 5:6, :] * f
    o_ref[0] = _ln_rows(res) * lw_ref[...] + lb_ref[...]


def _combine_residual_ln(y, idx, starts, x, mod, ln_w, ln_b, alpha):
    b, na, d = x.shape
    e, r = idx.shape
    n_tiles = b * na // TOKEN_TILE
    tile = pl.BlockSpec((1, TOKEN_TILE, d), lambda bi, i, *_: (bi, i, 0))
    vec = pl.BlockSpec((1, d), lambda bi, i, *_: (0, 0))
    row_tile = (d // LANES, LANES)
    chunk_rows = COMBINE_CHUNK + COMBINE_PAD
    return pl.pallas_call(
        functools.partial(_combine_kernel, alpha, e, r, n_tiles),
        out_shape=jax.ShapeDtypeStruct((b, na, d), F32),
        grid_spec=pltpu.PrefetchScalarGridSpec(
            num_scalar_prefetch=2, grid=(b, na // TOKEN_TILE),
            in_specs=[pl.BlockSpec(memory_space=pl.ANY), tile,
                      pl.BlockSpec((1, 1, N_MOD, d), lambda bi, i, *_: (bi, jnp.minimum(i, 1), 0, 0)), vec, vec],
            out_specs=tile,
            scratch_shapes=[pltpu.VMEM((TOKEN_TILE + COMBINE_PAD,) + row_tile, F32),
                            pltpu.VMEM((2, e, chunk_rows) + row_tile, F32),
                            pltpu.VMEM((chunk_rows,) + row_tile, F32),
                            pltpu.SemaphoreType.DMA((2, e)), pltpu.SemaphoreType.DMA(())]),
        compiler_params=_cparams(("arbitrary", "arbitrary"), VMEM_LIMIT),
        name="moe_combine_ln",
    )(jnp.pad(idx.reshape(-1), (0, COMBINE_PAD)), starts.reshape(-1), y, x, mod, ln_w.reshape(1, d), ln_b.reshape(1, d))


def _moe_residual_ln(x, mod, router_w, router_b, w_gate, w_up, w_down, layer, ln_w, ln_b, alpha):
    b, na, d = x.shape
    n_lat = na - CTX_LEN
    cap_c = EC_CAPACITY_FACTOR * CTX_LEN // N_EXPERTS
    cap_l = EC_CAPACITY_FACTOR * n_lat // N_EXPERTS
    h, aff = _router(x, mod, router_w, router_b)
    sel, _ = _select(aff, ((0, CTX_LEN, cap_c), (CTX_LEN, na, cap_l)))
    idx_c = jnp.argsort(1 - sel[:, :, :CTX_LEN], axis=-1, stable=True)[..., :cap_c]
    idx_l = jnp.argsort(1 - sel[:, :, CTX_LEN:], axis=-1, stable=True)[..., :cap_l] + CTX_LEN
    idx = jnp.concatenate([idx_c, idx_l], axis=-1).astype(jnp.int32)
    tv = jnp.take_along_axis(aff, idx, axis=-1)
    flat = idx + (jnp.arange(b, dtype=jnp.int32) * na)[:, None, None]
    flat = jnp.transpose(flat, (1, 0, 2)).reshape(N_EXPERTS, -1)
    tv = jnp.transpose(tv, (1, 0, 2)).reshape(N_EXPERTS, -1, 1)
    per_tile = jnp.sum(sel.reshape(b, N_EXPERTS, na // TOKEN_TILE, TOKEN_TILE), axis=-1)
    per_tile = jnp.transpose(per_tile, (1, 0, 2)).reshape(N_EXPERTS, -1)
    starts = jnp.concatenate([jnp.zeros((N_EXPERTS, 1), jnp.int32), jnp.cumsum(per_tile, axis=1, dtype=jnp.int32)], axis=1)
    xs = _gather_rows(h.reshape(b * na, d // LANES, LANES), flat)
    y = _expert_ffn(xs, w_gate, w_up, w_down, tv, layer)
    return _combine_residual_ln(y, flat, starts, x, mod, ln_w, ln_b, alpha)


def _dwconv_kernel(silu, x_ref, w_ref, b_ref, o_ref):
    x = x_ref[0]
    na = x.shape[0]
    row = lax.broadcasted_iota(jnp.int32, x.shape, 0)
    first = (row == 0) | (row == CTX_LEN)
    last = (row == CTX_LEN - 1) | (row == na - 1)
    prev = jnp.where(first, 0.0, pltpu.roll(x, 1, axis=0))
    nxt = jnp.where(last, 0.0, pltpu.roll(x, na - 1, axis=0))
    y = w_ref[0:1, :] * prev + w_ref[1:2, :] * x + w_ref[2:3, :] * nxt + b_ref[...]
    if silu:
        y = y * jax.nn.sigmoid(y)
    o_ref[0, 0] = y


def _segment_dwconv(p, w, bias, groups, silu):
    b, na, c = p.shape
    per = c // groups // LANES
    return pl.pallas_call(
        functools.partial(_dwconv_kernel, silu),
        out_shape=jax.ShapeDtypeStruct((groups, b, na, c // groups), F32),
        grid=(b, c // LANES),
        in_specs=[pl.BlockSpec((1, na, LANES), lambda bi, j: (bi, 0, j)),
                  pl.BlockSpec((SHORT_CONV, LANES), lambda bi, j: (0, j)),
                  pl.BlockSpec((1, LANES), lambda bi, j: (0, j))],
        out_specs=pl.BlockSpec((1, 1, na, LANES), lambda bi, j: (j // per, bi, 0, j % per)),
        compiler_params=_cparams(("parallel", "parallel"), VMEM_LIMIT),
        name="short_conv",
    )(p, w, bias.reshape(1, c))


def _rope_tables(n_lat):
    rows = n_lat // GRID_W
    row = jnp.repeat(jnp.arange(rows, dtype=F32), GRID_W)
    col = (jnp.arange(n_lat) % GRID_W).astype(F32)
    nf = ATTN_HEAD_DIM // 4
    inv = ROPE_THETA ** (-jnp.arange(nf, dtype=F32) / nf)
    ar = row[:, None] * inv
    ac = col[:, None] * inv
    ang = jnp.concatenate([ar, ar, ac, ac], axis=-1)
    return jnp.cos(ang), jnp.sin(ang)


def _mixing(x, mod, w_in, mlstm_conv_w, mlstm_conv_b, mlstm_gate_b, mlstm_norm_w, attn_q_norm_w,
            attn_k_norm_w, hyena_conv_w, hyena_conv_b, hyena_filter, hyena_skip, with_ctx_out):
    b, na, _ = x.shape
    n_lat = na - CTX_LEN
    mqk, mv, mo, aq, ak, av, hy, mg = _in_projection(x, mod, w_in)
    mg = mg[..., :MLSTM_GATES]

    qk = _segment_dwconv(mqk, _pad_heads(mlstm_conv_w), _pad_heads(mlstm_conv_b), groups=1,
                         silu=True)[0]
    g = mg.reshape(b, na, N_DIR, 2, MLSTM_HEADS) + mlstm_gate_b
    g = jnp.stack([g[:, :, :, 0], jax.nn.log_sigmoid(g[:, :, :, 1])], axis=3)
    gcols = jnp.transpose(g, (2, 0, 1, 3, 4)).reshape(N_DIR, b, na, 2 * MLSTM_HEADS)
    grows = jnp.transpose(gcols, (0, 1, 3, 2))
    hm = _mlstm_scan(qk, mv, grows, gcols)

    q, k, v = _qkv_prep(aq, ak, av, attn_q_norm_w, attn_k_norm_w)
    ya = _attention(q, k, v)

    hv, hx1, hx2 = _segment_dwconv(hy, hyena_conv_w, hyena_conv_b, groups=3, silu=False)
    yh = _hyena_latent(hv[:, CTX_LEN:], hx1[:, CTX_LEN:], hx2[:, CTX_LEN:],
                       *_hyena_taps(n_lat, *hyena_filter), hyena_skip)
    if with_ctx_out:
        yh_c = _hyena_context(hv[:, :CTX_LEN], hx1[:, :CTX_LEN], hx2[:, :CTX_LEN],
                              *_hyena_taps(CTX_LEN, *hyena_filter), hyena_skip)
    else:
        yh_c = jnp.zeros((b, CTX_LEN, D_HYENA), F32)
    yh = jnp.concatenate([yh_c, yh], axis=1)
    return hm, mo, ya, yh


def kernel(x, c, ctx, c_ctx, w_mod, b_mod, w_in, mlstm_conv_w, mlstm_conv_b, mlstm_gate_b, mlstm_norm_w, attn_q_norm_w, attn_k_norm_w, hyena_conv_w, hyena_conv_b, hyena_f_w1, hyena_f_b1, hyena_f_freq, hyena_f_w2, hyena_f_b2, hyena_f_w3, hyena_skip, w_out, ln_mix_w, ln_mix_b, router_w, router_b, exp_w_gate, exp_w_up, exp_w_down, ln_ffn_w, ln_ffn_b):
    bsz, seq, d = x.shape
    assert d == D_MODEL and ctx.shape[1] == CTX_LEN == TOKEN_TILE and seq == FFT_R * FFT_R // 2
    alpha = (2.0 * DEPTH) ** 0.25
    xa = jnp.concatenate([ctx, x], axis=1)
    crows = jnp.concatenate([c, jnp.broadcast_to(c_ctx, (8 - bsz, d))], axis=0)
    for l in range(DEPTH):
        last = l == DEPTH - 1
        m = _modulation(crows, w_mod, b_mod[l], l).reshape(8, N_MOD, d)
        mod = jnp.stack([jnp.broadcast_to(m[bsz], (bsz, N_MOD, d)), m[:bsz]], axis=1)
        hm, mo, ya, yh = _mixing(
            xa, mod, w_in[l], mlstm_conv_w[l], mlstm_conv_b[l], mlstm_gate_b[l], mlstm_norm_w[l],
            attn_q_norm_w[l], attn_k_norm_w[l], hyena_conv_w[l], hyena_conv_b[l],
            (hyena_f_w1[l], hyena_f_b1[l], hyena_f_freq[l], hyena_f_w2[l], hyena_f_b2[l], hyena_f_w3[l]),
            hyena_skip[l], not last)
        xa = _out_projection(hm, mo, mlstm_norm_w[l], ya, yh, xa, mod, w_out[l].astype(BF16),
                             ln_mix_w[l], ln_mix_b[l], alpha)
        xa = _moe_residual_ln(xa, mod, router_w[l], router_b[l], exp_w_gate, exp_w_up, exp_w_down, l,
                              ln_ffn_w[l], ln_ffn_b[l], alpha)
    return xa[:, CTX_LEN:]
```

```python
import functools
import math

import numpy as np
import jax
import jax.numpy as jnp
from jax import lax
from jax.experimental import pallas as pl
from jax.experimental.pallas import tpu as pltpu

F32 = jnp.float32
BF16 = jnp.bfloat16
HI = lax.Precision.HIGHEST

D_MODEL = 1024
DEPTH = 2
GRID_W = 64
CTX_LEN = 256
N_DIR = 2
SHORT_CONV = 3

D_MLSTM = 256
MLSTM_HEAD_DIM = 64
MLSTM_HEADS = 4
MLSTM_GATES = 16
MLSTM_CHUNK = 128

D_ATTN = 512
ATTN_HEAD_DIM = 64
ATTN_HEADS = 8
ATTN_KV_HEADS = 2
ATTN_GROUP = 4
D_KV = 128
ROPE_THETA = 10000.0

D_HYENA = 256
HYENA_ORDER = 2
HYENA_BANDS = 16
HYENA_FAST_DECAY = 0.3
HYENA_SLOW_DECAY = 1.5
HYENA_DECAY_TARGET = 1e-2
HYENA_WINDOW_SHIFT = 0.05

N_IN = 4 * D_MLSTM + MLSTM_GATES + D_ATTN + 2 * D_KV + 3 * D_HYENA
N_EXPERTS = 16
EC_CAPACITY_FACTOR = 2
D_FF_EXPERT = 2816
N_MOD = 6
LN_EPS = 1e-5
RMS_EPS = 1e-6

LANES = 128
ROW_SUB = D_MODEL // LANES
TOKEN_TILE = 256
FFT_R = 128
FFT_GROUP = 8
VMEM_LIMIT = 56 * 1024 * 1024


def _cparams(sem, vmem=None):
    return pltpu.CompilerParams(dimension_semantics=sem, vmem_limit_bytes=vmem)


def _mod_kernel(c_ref, w_ref, b_ref, o_ref):
    cs = c_ref[...]
    cs = cs * jax.nn.sigmoid(cs)
    o_ref[...] = jnp.dot(cs, w_ref[0], precision=HI, preferred_element_type=F32) + b_ref[...]


def _modulation(crows, w_mod, b_mod, layer):
    rows, d = crows.shape
    n = w_mod.shape[2]
    tn = 1024
    return pl.pallas_call(
        _mod_kernel,
        out_shape=jax.ShapeDtypeStruct((rows, n), F32),
        grid=(n // tn,),
        in_specs=[pl.BlockSpec((rows, d), lambda j: (0, 0)),
                  pl.BlockSpec((1, d, tn), lambda j: (layer, 0, j)),
                  pl.BlockSpec((1, tn), lambda j: (0, j))],
        out_specs=pl.BlockSpec((rows, tn), lambda j: (0, j)),
        compiler_params=_cparams(("parallel",)),
        name="adaln_mod",
    )(crows, w_mod, b_mod.reshape(1, n))


def _ln_rows(x):
    mu = jnp.mean(x, axis=-1, keepdims=True)
    xc = x - mu
    var = jnp.mean(xc * xc, axis=-1, keepdims=True)
    return xc * lax.rsqrt(var + LN_EPS)


def _mod_spec():
    return pl.BlockSpec((1, 1, N_MOD, D_MODEL), lambda b, i: (b, jnp.minimum(i, 1), 0, 0))


HEAD_DIM = 64
_C_MG = 4 * D_MLSTM
_C_AQ = _C_MG + MLSTM_GATES
IN_GROUPS = (("mqk", 0, 2 * D_MLSTM, True), ("mv", 2 * D_MLSTM, D_MLSTM, True), ("mo", 3 * D_MLSTM, D_MLSTM, False),
             ("aq", _C_AQ, D_ATTN, True), ("ak", _C_AQ + D_ATTN, D_KV, True), ("av", _C_AQ + D_ATTN + D_KV, D_KV, True),
             ("hy", _C_AQ + D_ATTN + 2 * D_KV, 3 * D_HYENA, False), ("mg", _C_MG, MLSTM_GATES, False))


def _group_width(src_width, head_padded):
    return src_width // HEAD_DIM * LANES if head_padded else -(-src_width // LANES) * LANES


def _pad_heads(a, axis=-1):
    axis = axis % a.ndim
    nh = a.shape[axis] // HEAD_DIM
    a = a.reshape(a.shape[:axis] + (nh, HEAD_DIM) + a.shape[axis + 1:])
    pad = [(0, 0)] * a.ndim
    pad[axis + 1] = (0, LANES - HEAD_DIM)
    a = jnp.pad(a, pad)
    return a.reshape(a.shape[:axis] + (nh * LANES,) + a.shape[axis + 2:])


def _inproj_kernel(x_ref, mod_ref, w_ref, *o_refs):
    y = _ln_rows(x_ref[0])
    h = (y * (1.0 + mod_ref[0, 0, 1:2, :]) + mod_ref[0, 0, 0:1, :]).astype(BF16)
    off = 0
    for o_ref in o_refs:
        width = o_ref.shape[2]
        o_ref[0] = jnp.dot(h, w_ref[:, off:off + width], preferred_element_type=F32)
        off += width


def _in_projection(x, mod, w_in):
    b, na, d = x.shape
    cols = []
    for _, start, src, head_padded in IN_GROUPS:
        wg = w_in[:, start:start + src]
        if head_padded:
            wg = _pad_heads(wg)
        elif src % LANES:
            wg = jnp.pad(wg, ((0, 0), (0, LANES - src % LANES)))
        cols.append(wg)
    w = jnp.concatenate(cols, axis=1).astype(BF16)
    n = w.shape[1]
    widths = [_group_width(src, hp) for _, _, src, hp in IN_GROUPS]
    return pl.pallas_call(
        _inproj_kernel,
        out_shape=[jax.ShapeDtypeStruct((b, na, width), F32) for width in widths],
        grid=(b, na // TOKEN_TILE),
        in_specs=[pl.BlockSpec((1, TOKEN_TILE, d), lambda bi, i: (bi, i, 0)),
                  _mod_spec(),
                  pl.BlockSpec((d, n), lambda bi, i: (0, 0))],
        out_specs=[pl.BlockSpec((1, TOKEN_TILE, width), lambda bi, i: (bi, i, 0)) for width in widths],
        compiler_params=_cparams(("parallel", "parallel"), VMEM_LIMIT),
        name="in_proj",
    )(x, mod, w)


def _compact_heads(x):
    tiles = [x[:, j * LANES:(j + 1) * LANES] for j in range(x.shape[1] // LANES)]
    return jnp.concatenate([tiles[j] + pltpu.roll(tiles[j + 1], HEAD_DIM, axis=1) for j in range(0, len(tiles), 2)],
                           axis=1)


def _outproj_kernel(alpha, hm0_ref, hm1_ref, mo_ref, nw_ref, seg_ref, ya_ref, yh_ref, x_ref, mod_ref, w_ref,
                    lw_ref, lb_ref, o_ref):
    hm = _compact_heads(hm0_ref[0, 0] + hm1_ref[0, 0])
    ss = jnp.dot(hm * hm, seg_ref[...], precision=HI, preferred_element_type=F32)
    hn = hm * lax.rsqrt(ss * (1.0 / MLSTM_HEAD_DIM) + RMS_EPS) * nw_ref[...]
    ym = jax.nn.sigmoid(mo_ref[0]) * hn
    mix = jnp.dot(ym.astype(BF16), w_ref[0:D_MLSTM, :], preferred_element_type=F32)
    mix = mix + jnp.dot(_compact_heads(ya_ref[0]).astype(BF16), w_ref[D_MLSTM:D_MLSTM + D_ATTN, :],
                        preferred_element_type=F32)
    mix = mix + jnp.dot(yh_ref[0].astype(BF16), w_ref[D_MLSTM + D_ATTN:, :], preferred_element_type=F32)
    r = alpha * x_ref[0] + mod_ref[0, 0, 2:3, :] * mix
    o_ref[0] = _ln_rows(r) * lw_ref[...] + lb_ref[...]


def _out_projection(hm, mo, norm_w, ya, yh, x, mod, w_bf16, ln_w, ln_b, alpha):
    b, na, d = x.shape
    dm = mo.shape[2]
    head = np.arange(dm) // MLSTM_HEAD_DIM
    seg = jnp.asarray((head[:, None] == head[None, :]).astype(np.float32))

    def tile(width):
        return pl.BlockSpec((1, TOKEN_TILE, width), lambda bi, i: (bi, i, 0))

    def hm_spec(direction):
        return pl.BlockSpec((1, 1, TOKEN_TILE, hm.shape[3]), lambda bi, i: (direction, bi, i, 0))

    vec = pl.BlockSpec((1, d), lambda bi, i: (0, 0))
    return pl.pallas_call(
        functools.partial(_outproj_kernel, alpha),
        out_shape=jax.ShapeDtypeStruct((b, na, d), F32),
        grid=(b, na // TOKEN_TILE),
        in_specs=[hm_spec(0), hm_spec(1), tile(dm), pl.BlockSpec((1, dm), lambda bi, i: (0, 0)),
                  pl.BlockSpec((dm, dm), lambda bi, i: (0, 0)), tile(ya.shape[2]), tile(yh.shape[2]), tile(d),
                  _mod_spec(), pl.BlockSpec((d, d), lambda bi, i: (0, 0)), vec, vec],
        out_specs=tile(d),
        compiler_params=_cparams(("parallel", "parallel"), VMEM_LIMIT),
        name="out_proj_ln",
    )(hm, hm, mo, norm_w.reshape(1, dm), seg, ya, yh, x, mod, w_bf16, ln_w.reshape(1, d), ln_b.reshape(1, d))


def _mlstm_kernel(qk_ref, v_ref, gr_ref, gc_ref, o_ref, c_sc, n_sc, m_sc):
    t = MLSTM_CHUNK
    nh = MLSTM_HEADS
    d = pl.program_id(0)

    @pl.when(pl.program_id(2) == 0)
    def _():
        c_sc[...] = jnp.zeros_like(c_sc)
        n_sc[...] = jnp.zeros_like(n_sc)
        m_sc[...] = jnp.zeros_like(m_sc)

    row = lax.broadcasted_iota(jnp.int32, (t, t), 0)
    col = lax.broadcasted_iota(jnp.int32, (t, t), 1)
    mask = jnp.where(d == 0, col - row, row - col) <= 0
    maskf = mask.astype(F32)
    grows = gr_ref[0, 0]
    gcols = gc_ref[0, 0]
    cum_cols = jnp.dot(maskf, gcols, precision=HI, preferred_element_type=F32)
    cum_rows = lax.dot_general(grows, maskf, (((1,), (1,)), ((), ())), precision=HI,
                               preferred_element_type=F32)
    lane8 = lax.broadcasted_iota(jnp.int32, (t, 2 * nh), 1)

    def column(a, idx):
        return jnp.sum(jnp.where(lane8 == idx, a, 0.0), axis=1, keepdims=True)

    dh = LANES
    for h in range(nh):
        qc = qk_ref[0, :, h * dh:(h + 1) * dh] * (MLSTM_HEAD_DIM ** -0.5)
        kc = qk_ref[0, :, (nh + h) * dh:(nh + h + 1) * dh]
        vc = v_ref[0, :, h * dh:(h + 1) * dh]
        ic_row = grows[h:h + 1, :]
        ic_col = column(gcols, h)
        bcol = column(cum_cols, nh + h)
        brow = cum_rows[nh + h:nh + h + 1, :]
        tot = jnp.sum(grows[nh + h:nh + h + 1, :], axis=1, keepdims=True)
        m0 = m_sc[h:h + 1, 0:1]
        log_inter = bcol + m0
        log_intra = jnp.where(mask, bcol - brow + ic_row, -jnp.inf)
        mrow = jnp.maximum(log_inter, jnp.max(log_intra, axis=1, keepdims=True))
        w_inter = jnp.exp(log_inter - mrow)
        qb = qc.astype(BF16)
        vb = vc.astype(BF16)
        scores = lax.dot_general(qb, kc.astype(BF16), (((1,), (1,)), ((), ())),
                                 preferred_element_type=F32) * jnp.exp(log_intra - mrow)
        ct = c_sc[h]
        n0 = n_sc[h]
        num = (w_inter * jnp.dot(qb, ct.astype(BF16), preferred_element_type=F32)
               + jnp.dot(scores.astype(BF16), vb, preferred_element_type=F32))
        den = w_inter * jnp.sum(qc * n0, axis=1, keepdims=True) + jnp.sum(scores, axis=1, keepdims=True)
        o_ref[0, 0, :, h * dh:(h + 1) * dh] = num / jnp.maximum(jnp.abs(den), jnp.exp(-mrow))
        log_src = tot - bcol + ic_col
        m_new = jnp.maximum(tot + m0, jnp.max(log_src, axis=0, keepdims=True))
        wk = jnp.exp(log_src - m_new) * kc
        decay = jnp.exp(tot + m0 - m_new)
        c_sc[h] = decay * ct + lax.dot_general(wk.astype(BF16), vb, (((0,), (0,)), ((), ())),
                                               preferred_element_type=F32)
        n_sc[h] = decay * n0 + jnp.sum(wk, axis=0, keepdims=True)
        m_sc[h:h + 1, :] = jnp.broadcast_to(m_new, (1, 128))


def _mlstm_scan(qk, v, grows, gcols):
    b, na, dv = v.shape
    nh, dh = MLSTM_HEADS, LANES
    t = MLSTM_CHUNK
    nc = na // t
    nctx = CTX_LEN // t

    def chunk(d, c):
        rev = jnp.where(c < nctx, nctx - 1 - c, nc + nctx - 1 - c)
        return jnp.where(d == 0, c, rev)

    return pl.pallas_call(
        _mlstm_kernel,
        out_shape=jax.ShapeDtypeStruct((N_DIR, b, na, dv), F32),
        grid=(N_DIR, b, nc),
        in_specs=[pl.BlockSpec((1, t, 2 * dv), lambda d, bi, c: (bi, chunk(d, c), 0)),
                  pl.BlockSpec((1, t, dv), lambda d, bi, c: (bi, chunk(d, c), 0)),
                  pl.BlockSpec((1, 1, 2 * nh, t), lambda d, bi, c: (d, bi, 0, chunk(d, c))),
                  pl.BlockSpec((1, 1, t, 2 * nh), lambda d, bi, c: (d, bi, chunk(d, c), 0))],
        out_specs=pl.BlockSpec((1, 1, t, dv), lambda d, bi, c: (d, bi, chunk(d, c), 0)),
        scratch_shapes=[pltpu.VMEM((nh, dh, dh), F32), pltpu.VMEM((nh, 1, dh), F32), pltpu.VMEM((8, 128), F32)],
        compiler_params=_cparams(("parallel", "parallel", "arbitrary")),
        name="mlstm_scan",
    )(qk, v, grows, gcols)


ATTN_TQ = 256
LOG2E = 1.4426950408889634


def _qkv_prep_kernel(aq_ref, ak_ref, av_ref, cos_ref, sin_ref, qw_ref, kw_ref, q_ref, k_ref, v_ref):
    lane = lax.broadcasted_iota(jnp.int32, (aq_ref.shape[1], LANES), 1)
    head_lane = lane < ATTN_HEAD_DIM
    first_half = (lane % (ATTN_HEAD_DIM // 2)) < (ATTN_HEAD_DIM // 4)
    cos = cos_ref[...]
    sin = sin_ref[...]
    quarter = ATTN_HEAD_DIM // 4

    def norm_rope(x, w):
        y = x * lax.rsqrt(jnp.sum(x * x, axis=1, keepdims=True) * (1.0 / ATTN_HEAD_DIM) + RMS_EPS) * w
        rot = jnp.where(first_half, -pltpu.roll(y, LANES - quarter, axis=1), pltpu.roll(y, quarter, axis=1))
        return jnp.where(head_lane, y * cos + rot * sin, 0.0)

    for h in range(ATTN_HEADS):
        q_ref[0, h] = norm_rope(aq_ref[0, :, h * LANES:(h + 1) * LANES], qw_ref[...]).astype(BF16)
    for h in range(ATTN_KV_HEADS):
        k_ref[0, h] = norm_rope(ak_ref[0, :, h * LANES:(h + 1) * LANES], kw_ref[...]).astype(BF16)
        v_ref[0, h] = jnp.where(lane == ATTN_HEAD_DIM, 1.0, av_ref[0, :, h * LANES:(h + 1) * LANES]).astype(BF16)


def _qkv_prep(aq, ak, av, q_norm_w, k_norm_w):
    b, na, _ = aq.shape
    cos, sin = _rope_tables(na - CTX_LEN)
    pad = ((CTX_LEN, 0), (0, LANES - ATTN_HEAD_DIM))
    cos = jnp.pad(cos, pad, constant_values=1.0)
    sin = jnp.pad(sin, pad)
    qw = jnp.pad(q_norm_w * (ATTN_HEAD_DIM ** -0.5 * LOG2E), (0, LANES - ATTN_HEAD_DIM)).reshape(1, LANES)
    kw = jnp.pad(k_norm_w, (0, LANES - ATTN_HEAD_DIM)).reshape(1, LANES)
    tok = lambda width: pl.BlockSpec((1, TOKEN_TILE, width), lambda bi, i: (bi, i, 0))
    tab = pl.BlockSpec((TOKEN_TILE, LANES), lambda bi, i: (i, 0))
    vec = pl.BlockSpec((1, LANES), lambda bi, i: (0, 0))
    heads = lambda nh: pl.BlockSpec((1, nh, TOKEN_TILE, LANES), lambda bi, i: (bi, 0, i, 0))
    return pl.pallas_call(
        _qkv_prep_kernel,
        out_shape=(jax.ShapeDtypeStruct((b, ATTN_HEADS, na, LANES), BF16),
                   jax.ShapeDtypeStruct((b, ATTN_KV_HEADS, na, LANES), BF16),
                   jax.ShapeDtypeStruct((b, ATTN_KV_HEADS, na, LANES), BF16)),
        grid=(b, na // TOKEN_TILE),
        in_specs=[tok(aq.shape[2]), tok(ak.shape[2]), tok(av.shape[2]), tab, tab, vec, vec],
        out_specs=(heads(ATTN_HEADS), heads(ATTN_KV_HEADS), heads(ATTN_KV_HEADS)),
        compiler_params=_cparams(("parallel", "parallel")),
        name="qkv_prep",
    )(aq, ak, av, cos, sin, qw, kw)


def _attn_kernel(q_ref, k_ref, v_ref, o_ref, s_ref, p_ref):
    def attend(nk):
        s_ref[:, 0:nk] = lax.dot_general(q_ref[0, 0], k_ref[0, 0, 0:nk, :], (((1,), (1,)), ((), ())),
                                         preferred_element_type=F32)
        m = jnp.max(s_ref[:, 0:nk], axis=1, keepdims=True)
        p_ref[:, 0:nk] = jnp.exp2(s_ref[:, 0:nk] - m).astype(BF16)
        acc = jnp.dot(p_ref[:, 0:nk], v_ref[0, 0, 0:nk, :], preferred_element_type=F32)
        lane = lax.broadcasted_iota(jnp.int32, acc.shape, 1)
        o_ref[0] = jnp.where(lane < ATTN_HEAD_DIM, acc / acc[:, ATTN_HEAD_DIM:ATTN_HEAD_DIM + 1], 0.0)

    is_ctx = pl.program_id(2) < CTX_LEN // ATTN_TQ
    pl.when(is_ctx)(lambda: attend(CTX_LEN))
    pl.when(jnp.logical_not(is_ctx))(lambda: attend(k_ref.shape[2]))


def _attention(q, k, v):
    b, nh, na, _ = q.shape
    group = nh // k.shape[1]
    kv_spec = pl.BlockSpec((1, 1, na, LANES), lambda bi, h, i: (bi, h // group, 0, 0))
    return pl.pallas_call(
        _attn_kernel,
        out_shape=jax.ShapeDtypeStruct((b, na, nh * LANES), F32),
        grid=(b, nh, na // ATTN_TQ),
        in_specs=[pl.BlockSpec((1, 1, ATTN_TQ, LANES), lambda bi, h, i: (bi, h, i, 0)), kv_spec, kv_spec],
        out_specs=pl.BlockSpec((1, ATTN_TQ, LANES), lambda bi, h, i: (bi, i, h)),
        scratch_shapes=[pltpu.VMEM((ATTN_TQ, na), F32), pltpu.VMEM((ATTN_TQ, na), BF16)],
        compiler_params=_cparams(("parallel", "parallel", "parallel"), VMEM_LIMIT),
        name="attention",
    )(q, k, v)


def _dft_tables(t2_len):
    r = FFT_R
    n = r * r
    idx = np.arange(r, dtype=np.float64)
    kb = idx[None, :, None]
    t1 = idx[:, None, None]
    t2 = np.arange(t2_len, dtype=np.float64)[None, None, :]
    ang = -2.0 * np.pi * (t2 * kb / r + t1 * kb / n)
    f1 = np.concatenate([np.cos(ang), np.sin(ang)], axis=1)
    ang2 = -2.0 * np.pi * np.outer(idx, idx) / r
    f2 = np.stack([np.cos(ang2), np.sin(ang2)])
    return f1.astype(np.float32), f2.astype(np.float32)


def _idft_table(t2_len):
    r = FFT_R
    n = r * r
    t1 = np.arange(r, dtype=np.float64)[:, None, None]
    t2 = np.arange(t2_len, dtype=np.float64)[None, :, None]
    kb = np.arange(r, dtype=np.float64)[None, None, :]
    ang = 2.0 * np.pi * (t2 * kb / r + t1 * kb / n)
    return (np.stack([np.cos(ang), np.sin(ang)], axis=1) / n).astype(np.float32)


def _dot_hi(table, x):
    return jnp.dot(table, x.astype(BF16), preferred_element_type=F32)


def _stage_block(block, sc):
    rows = block.shape[0]
    for t in range(rows):
        for h in range(sc.shape[0]):
            sc[h, pl.ds(t * FFT_GROUP, FFT_GROUP), :] = block[t, :, h * LANES:(h + 1) * LANES]


def _middle_rows(sc, i):
    rows = sc.shape[1] // FFT_GROUP
    return jnp.concatenate([sc[h, pl.ds(i, rows, stride=FFT_GROUP), :] for h in range(sc.shape[0])], axis=1)


def _stage_scratch(rows, c):
    return pltpu.VMEM((c // LANES, rows * FFT_GROUP, LANES), F32)


def _fft1_kernel(f_ref, z_ref, g_ref, z_sc):
    _stage_block(z_ref.at[0], z_sc)
    for i in range(FFT_GROUP):
        g_ref[0, i] = _dot_hi(f_ref[i], _middle_rows(z_sc, i))


def _fft_stage1(z4, f1):
    b, t2_len, r, c = z4.shape
    g = FFT_GROUP
    return pl.pallas_call(
        _fft1_kernel,
        out_shape=jax.ShapeDtypeStruct((b, r, 2 * r, c), F32),
        grid=(b, r // g),
        in_specs=[pl.BlockSpec((g, 2 * r, t2_len), lambda bi, j: (j, 0, 0)),
                  pl.BlockSpec((1, t2_len, g, c), lambda bi, j: (bi, 0, j, 0))],
        out_specs=pl.BlockSpec((1, g, 2 * r, c), lambda bi, j: (bi, j, 0, 0)),
        scratch_shapes=[_stage_scratch(t2_len, c)],
        compiler_params=_cparams(("parallel", "parallel")),
        name="fft_stage1",
    )(f1, z4)


def _fft2_kernel(with_filter, f_ref, gr_ref, gi_ref, aux_ref, h_ref, gr_sc, gi_sc):
    fr = f_ref[0]
    fi = f_ref[1]
    _stage_block(gr_ref.at[0], gr_sc)
    _stage_block(gi_ref.at[0], gi_sc)
    for i in range(FFT_GROUP):
        gr = _middle_rows(gr_sc, i).astype(BF16)
        gi = _middle_rows(gi_sc, i).astype(BF16)
        xr = _dot_hi(fr, gr) - _dot_hi(fi, gi)
        xi = _dot_hi(fr, gi) + _dot_hi(fi, gr)
        if with_filter:
            tr = aux_ref[0, 0, i]
            ti = aux_ref[0, 1, i]
            yr = xr * tr - xi * ti
            yi = xr * ti + xi * tr
            xr = _dot_hi(fr, yr) + _dot_hi(fi, yi)
            xi = _dot_hi(fr, yi) - _dot_hi(fi, yr)
        else:
            xr = xr * aux_ref[0]
            xi = xi * aux_ref[0]
        h_ref[0, 0, i] = xr
        h_ref[0, 1, i] = xi


def _fft_stage2(g, f2, tf=None, scale=None):
    b, r, _, c = g.shape
    grp = FFT_GROUP
    nblk = r // grp
    in_specs = [pl.BlockSpec((2, r, r), lambda bi, j: (0, 0, 0)),
                pl.BlockSpec((1, r, grp, c), lambda bi, j: (bi, 0, j, 0)),
                pl.BlockSpec((1, r, grp, c), lambda bi, j: (bi, 0, nblk + j, 0))]
    args = [f2, g, g]
    if tf is not None:
        in_specs.append(pl.BlockSpec((1, 2, grp, r, c), lambda bi, j: (0, 0, j, 0, 0)))
        args.append(tf)
    else:
        in_specs.append(pl.BlockSpec((1, 1, c), lambda bi, j: (bi, 0, 0)))
        args.append(scale)
    return pl.pallas_call(
        functools.partial(_fft2_kernel, tf is not None),
        out_shape=jax.ShapeDtypeStruct((b, 2, r, r, c), F32),
        grid=(b, nblk),
        in_specs=in_specs,
        out_specs=pl.BlockSpec((1, 2, grp, r, c), lambda bi, j: (bi, 0, j, 0, 0)),
        scratch_shapes=[_stage_scratch(r, c), _stage_scratch(r, c)],
        compiler_params=_cparams(("parallel", "parallel"), VMEM_LIMIT),
        name="fft_stage2",
    )(*args)


def _ifft_kernel(e_ref, h_ref, z_ref, x_ref, skip_ref, o_ref, hr_sc, hi_sc, z_sc, x_sc, o_sc):
    _stage_block(h_ref.at[0, 0], hr_sc)
    _stage_block(h_ref.at[0, 1], hi_sc)
    _stage_block(z_ref.at[0], z_sc)
    _stage_block(x_ref.at[0], x_sc)
    rows = z_ref.shape[1]
    for i in range(FFT_GROUP):
        y = _dot_hi(e_ref[i, 0], _middle_rows(hr_sc, i)) - _dot_hi(e_ref[i, 1], _middle_rows(hi_sc, i))
        y = y * (1.0 / (FFT_R * FFT_R))
        out = _middle_rows(x_sc, i) * (y + skip_ref[...] * _middle_rows(z_sc, i))
        for h in range(o_sc.shape[0]):
            o_sc[h, pl.ds(i, rows, stride=FFT_GROUP), :] = out[:, h * LANES:(h + 1) * LANES]
    for t in range(rows):
        o_ref[0, t] = jnp.concatenate([o_sc[h, pl.ds(t * FFT_GROUP, FFT_GROUP), :] for h in range(o_sc.shape[0])],
                                      axis=1)


def _ifft_gate(h, e, z4, x4, skip):
    b, t2_len, r, c = z4.shape
    g = FFT_GROUP
    tok = pl.BlockSpec((1, t2_len, g, c), lambda bi, j: (bi, 0, j, 0))
    return pl.pallas_call(
        _ifft_kernel,
        out_shape=jax.ShapeDtypeStruct(z4.shape, F32),
        grid=(b, r // g),
        in_specs=[pl.BlockSpec((g, 2, t2_len, r), lambda bi, j: (j, 0, 0, 0)),
                  pl.BlockSpec((1, 2, r, g, c), lambda bi, j: (bi, 0, 0, j, 0)),
                  tok, tok, pl.BlockSpec((1, c), lambda bi, j: (0, 0))],
        out_specs=tok,
        scratch_shapes=[_stage_scratch(r, c), _stage_scratch(r, c), _stage_scratch(t2_len, c),
                        _stage_scratch(t2_len, c), _stage_scratch(t2_len, c)],
        compiler_params=_cparams(("parallel", "parallel"), VMEM_LIMIT),
        name="ifft_gate",
    )(e, h, z4, x4, skip.reshape(1, c))


TAPS_ROWS = 1024
FEAT_PAD = 128


def _taps_kernel(n, feat_ref, w1_ref, b1_ref, freq_ref, w2_ref, b2_ref, w3_ref, delta_ref, taps_ref, l1_ref):
    c = D_HYENA
    rows = feat_ref.shape[0]
    step = pl.program_id(0)

    @pl.when(step == 0)
    def _():
        l1_ref[...] = jnp.zeros_like(l1_ref)

    f = feat_ref[...]
    hid = jnp.sin(freq_ref[0:1, :] * (jnp.dot(f, w1_ref[...], precision=HI, preferred_element_type=F32) + b1_ref[...]))
    hid = jnp.sin(freq_ref[1:2, :] * (jnp.dot(hid, w2_ref[...], precision=HI, preferred_element_type=F32) + b2_ref[...]))
    filt = jnp.dot(hid, w3_ref[...], precision=HI, preferred_element_type=F32)
    window = jnp.exp(-f[:, 0:1] * delta_ref[...]) + HYENA_WINDOW_SHIFT
    i = step * rows + lax.broadcasted_iota(jnp.int32, (rows, c), 0)
    for o in range(HYENA_ORDER):
        fwd = filt[:, o * c:(o + 1) * c]
        bwd = filt[:, (HYENA_ORDER + o) * c:(HYENA_ORDER + o + 1) * c]
        tap = jnp.where(i < n, fwd, jnp.where(i > n, bwd, 0.0)) * window
        taps_ref[o] = tap
        l1_ref[o:o + 1, :] += jnp.sum(jnp.abs(tap), axis=0, keepdims=True)


def _hyena_taps(n, f_w1, f_b1, f_freq, f_w2, f_b2, f_w3):
    i = np.arange(2 * n)
    t = (np.where(i < n, i, 2 * n - i).astype(np.float32) / np.float32(n)).astype(np.float32)
    ang = (np.float32(2.0 * math.pi) * t[:, None]) * np.arange(1, HYENA_BANDS + 1, dtype=np.float32)
    feats = np.zeros((2 * n, FEAT_PAD), np.float32)
    feats[:, 0] = t
    feats[:, 1:1 + HYENA_BANDS] = np.cos(ang.astype(np.float64))
    feats[:, 1 + HYENA_BANDS:1 + 2 * HYENA_BANDS] = np.sin(ang.astype(np.float64))
    log_target = abs(math.log(HYENA_DECAY_TARGET))
    deltas = jnp.linspace(log_target / HYENA_SLOW_DECAY, log_target / HYENA_FAST_DECAY, D_HYENA, dtype=F32)
    hid = f_w2.shape[0]
    w1 = jnp.concatenate([f_w1, jnp.zeros((FEAT_PAD - f_w1.shape[0], hid), F32)], axis=0)
    rows = min(TAPS_ROWS, 2 * n)
    full = lambda shape: pl.BlockSpec(shape, lambda j: (0,) * len(shape))
    return pl.pallas_call(
        functools.partial(_taps_kernel, n),
        out_shape=(jax.ShapeDtypeStruct((HYENA_ORDER, 2 * n, D_HYENA), F32),
                   jax.ShapeDtypeStruct((HYENA_ORDER, D_HYENA), F32)),
        grid=(2 * n // rows,),
        in_specs=[pl.BlockSpec((rows, FEAT_PAD), lambda j: (j, 0)), full((FEAT_PAD, hid)), full((1, hid)),
                  full((2, hid)), full((hid, hid)), full((1, hid)), full(f_w3.shape), full((1, D_HYENA))],
        out_specs=(pl.BlockSpec((HYENA_ORDER, rows, D_HYENA), lambda j: (0, j, 0)), full((HYENA_ORDER, D_HYENA))),
        compiler_params=_cparams(("arbitrary",)),
        name="hyena_taps",
    )(jnp.asarray(feats), w1, f_b1.reshape(1, hid), f_freq, f_w2, f_b2.reshape(1, hid), f_w3, deltas.reshape(1, -1))


def _hyena_latent(v, x1, x2, taps, l1, skip):
    b, n, c = v.shape
    r = FFT_R
    t2_len = n // r
    f1_full, f2 = _dft_tables(r)
    f1_half = jnp.asarray(f1_full[:, :, :t2_len]).astype(BF16)
    f1_full = jnp.asarray(f1_full).astype(BF16)
    f2 = jnp.asarray(f2).astype(BF16)
    e = jnp.asarray(_idft_table(t2_len) * (r * r)).astype(BF16)
    taps4 = taps.reshape(HYENA_ORDER, r, r, c)
    tf = _fft_stage2(_fft_stage1(taps4, f1_full), f2, scale=(1.0 / l1)[:, None, :])
    z = v.reshape(b, t2_len, r, c)
    for o, gate in enumerate((x1, x2)):
        g = _fft_stage1(z, f1_half)
        h = _fft_stage2(g, f2, tf[o:o + 1])
        z = _ifft_gate(h, e, z, gate.reshape(b, t2_len, r, c), skip[o])
    return z.reshape(b, n, c)


def _hyena_ctx_kernel(n, v_ref, x1_ref, x2_ref, k_ref, skip_ref, o_ref, z_sc):
    z_sc[...] = v_ref[0]
    for o, gate_ref in enumerate((x1_ref, x2_ref)):
        def body(s, acc):
            return acc + k_ref[o, pl.ds(n - 1 - s, n), :] * z_sc[pl.ds(s, 1), :]
        conv = lax.fori_loop(0, n, body, jnp.zeros(z_sc.shape, F32))
        z_sc[...] = gate_ref[0] * (conv + skip_ref[o:o + 1, :] * z_sc[...])
    o_ref[0] = z_sc[...]


def _hyena_context(v, x1, x2, taps, l1, skip):
    b, n, c = v.shape
    cb = 128
    k2 = jnp.roll(taps, n - 1, axis=1) / l1[:, None, :]
    tok = pl.BlockSpec((1, n, cb), lambda bi, j: (bi, 0, j))
    return pl.pallas_call(
        functools.partial(_hyena_ctx_kernel, n),
        out_shape=jax.ShapeDtypeStruct((b, n, c), F32),
        grid=(b, c // cb),
        in_specs=[tok, tok, tok,
                  pl.BlockSpec((HYENA_ORDER, 2 * n, cb), lambda bi, j: (0, 0, j)),
                  pl.BlockSpec((HYENA_ORDER, cb), lambda bi, j: (0, j))],
        out_specs=tok,
        scratch_shapes=[pltpu.VMEM((n, cb), F32)],
        compiler_params=_cparams(("parallel", "parallel")),
        name="hyena_ctx",
    )(v, x1, x2, k2, skip)


def _router_kernel(x_ref, mod_ref, rw_ref, rb_ref, h_ref, aff_ref):
    y = _ln_rows(x_ref[0])
    h = y * (1.0 + mod_ref[0, 0, 4:5, :]) + mod_ref[0, 0, 3:4, :]
    for j in range(ROW_SUB):
        h_ref[0, pl.ds(j, TOKEN_TILE, stride=ROW_SUB), :] = h[:, j * LANES:(j + 1) * LANES]
    logits = lax.dot_general(rw_ref[...], h, (((1,), (1,)), ((), ())), precision=HI,
                             preferred_element_type=F32) + rb_ref[...]
    z = jnp.exp(logits - jnp.max(logits, axis=0, keepdims=True))
    aff_ref[0] = z / jnp.sum(z, axis=0, keepdims=True)


def _router(x, mod, router_w, router_b):
    b, na, d = x.shape
    e = router_w.shape[1]
    return pl.pallas_call(
        _router_kernel,
        out_shape=(jax.ShapeDtypeStruct((b, na * ROW_SUB, LANES), F32), jax.ShapeDtypeStruct((b, e, na), F32)),
        grid=(b, na // TOKEN_TILE),
        in_specs=[pl.BlockSpec((1, TOKEN_TILE, d), lambda bi, i: (bi, i, 0)),
                  _mod_spec(),
                  pl.BlockSpec((e, d), lambda bi, i: (0, 0)),
                  pl.BlockSpec((e, 1), lambda bi, i: (0, 0))],
        out_specs=(pl.BlockSpec((1, TOKEN_TILE * ROW_SUB, LANES), lambda bi, i: (bi, i, 0)),
                   pl.BlockSpec((1, e, TOKEN_TILE), lambda bi, i: (bi, 0, i))),
        compiler_params=_cparams(("parallel", "parallel")),
        name="router",
    )(x, mod, router_w.T, router_b.reshape(e, 1))


def _prefix_count(x):
    n = x.shape[1]
    lane = lax.broadcasted_iota(jnp.int32, x.shape, 1)
    sh = 1
    while sh < n:
        x = x + jnp.where(lane >= sh, pltpu.roll(x, sh, axis=1), 0)
        sh *= 2
    return x


SELECT_BISECTIONS = 160


def _select_kernel(segments, aff_ref, sel_ref, pos_ref):
    nb, ne, _ = aff_ref.shape
    parts = [(b, s0, s1, cap) for b in range(nb) for (s0, s1, cap) in segments]

    def body(_, bounds):
        out = []
        for (b, s0, s1, cap), (lo, hi) in zip(parts, bounds):
            mid = 0.5 * (lo + hi)
            cnt = jnp.sum((aff_ref[b, :, s0:s1] >= mid).astype(jnp.int32), axis=1, keepdims=True)
            ok = cnt >= cap
            out.append((jnp.where(ok, mid, lo), jnp.where(ok, hi, mid)))
        return tuple(out)

    init = tuple((jnp.zeros((ne, 1), F32), jnp.full((ne, 1), 2.0, F32)) for _ in parts)
    bounds = lax.fori_loop(0, SELECT_BISECTIONS, body, init)
    for (b, s0, s1, cap), (lo, hi) in zip(parts, bounds):
        a = aff_ref[b, :, s0:s1]
        above = (a >= hi).astype(jnp.int32)
        tied = jnp.where(a >= lo, 1, 0) - above
        need = cap - jnp.sum(above, axis=1, keepdims=True)
        tie_rank = _prefix_count(tied) - tied
        sel = above + tied * (tie_rank < need).astype(jnp.int32)
        sel_ref[b, :, s0:s1] = sel
        pos_ref[b, :, s0:s1] = _prefix_count(sel) - sel


def _select(aff, segments):
    b, e, na = aff.shape
    blk = pl.BlockSpec((b, e, na), lambda i: (0, 0, 0))
    return pl.pallas_call(
        functools.partial(_select_kernel, segments),
        out_shape=(jax.ShapeDtypeStruct((b, e, na), jnp.int32), jax.ShapeDtypeStruct((b, e, na), jnp.int32)),
        grid=(1,),
        in_specs=[blk],
        out_specs=(blk, blk),
        compiler_params=_cparams(("arbitrary",)),
        name="expert_select",
    )(aff)


MOE_ROW_TILE = 264


def _row_tile(r):
    return next(t for t in range(MOE_ROW_TILE, 7, -8) if r % t == 0)


DMA_UNROLL = 8


def _for_each_row(tr, fn):
    def body(i, c):
        fn(i)
        return c
    lax.fori_loop(0, tr, body, 0, unroll=DMA_UNROLL)


def _gather_kernel(tr, idx_ref, h_hbm, o_ref, buf, sem):
    step = pl.program_id(0) * pl.num_programs(1) + pl.program_id(1)
    n_steps = pl.num_programs(0) * pl.num_programs(1)
    cur = step % 2

    def start_tile(t, slot):
        _for_each_row(tr, lambda i: pltpu.make_async_copy(
            h_hbm.at[pl.ds(pl.multiple_of(idx_ref[t * tr + i] * ROW_SUB, ROW_SUB), ROW_SUB)],
            buf.at[slot, pl.ds(i * ROW_SUB, ROW_SUB)], sem.at[slot]).start())

    @pl.when(step == 0)
    def _():
        start_tile(0, 0)

    @pl.when(step + 1 < n_steps)
    def _():
        start_tile(step + 1, 1 - cur)

    pltpu.make_async_copy(h_hbm.at[pl.ds(0, tr * ROW_SUB)], buf.at[cur], sem.at[cur]).wait()
    rows = buf.at[cur]
    for j in range(ROW_SUB):
        o_ref[0, :, j * LANES:(j + 1) * LANES] = rows[pl.ds(j, tr, stride=ROW_SUB), :].astype(BF16)


def _gather_rows(h2, idx):
    e, r = idx.shape
    tr = _row_tile(r)
    return pl.pallas_call(
        functools.partial(_gather_kernel, tr),
        out_shape=jax.ShapeDtypeStruct((e, r, ROW_SUB * LANES), BF16),
        grid_spec=pltpu.PrefetchScalarGridSpec(
            num_scalar_prefetch=1, grid=(e, r // tr),
            in_specs=[pl.BlockSpec(memory_space=pl.ANY)],
            out_specs=pl.BlockSpec((1, tr, ROW_SUB * LANES), lambda ei, j, idx_ref: (ei, j, 0)),
            scratch_shapes=[pltpu.VMEM((2, tr * ROW_SUB, LANES), F32), pltpu.SemaphoreType.DMA((2,))]),
        compiler_params=_cparams(("arbitrary", "arbitrary")),
        name="moe_gather",
    )(idx.reshape(-1), h2)


MOE_TF = 256
MOE_ROW_CHUNKS = 4


COMBINE_CHUNK = 64


def _expert_ffn_kernel(xs_ref, wg_ref, wu_ref, wd_ref, tv_ref, o_ref, acc_ref):
    j = pl.program_id(1)
    wg = wg_ref[0, 0].astype(BF16)
    wu = wu_ref[0, 0].astype(BF16)
    wd = wd_ref[0, 0].astype(BF16)
    rows = xs_ref.shape[1]
    rc = rows // MOE_ROW_CHUNKS

    @pl.when(j == 0)
    def _():
        acc_ref[...] = jnp.zeros_like(acc_ref)

    for ci in range(MOE_ROW_CHUNKS):
        sl = slice(ci * rc, (ci + 1) * rc)
        x = xs_ref[0, sl, :]
        g = jnp.dot(x, wg, preferred_element_type=F32)
        u = jnp.dot(x, wu, preferred_element_type=F32)
        a = (g * jax.nn.sigmoid(g) * u).astype(BF16)
        acc_ref[sl, :] += jnp.dot(a, wd, preferred_element_type=F32)

    @pl.when(j == pl.num_programs(1) - 1)
    def _():
        y = acc_ref[...] * tv_ref[0]
        for t in range(ROW_SUB):
            o_ref[0, pl.ds(t, rows, stride=ROW_SUB), :] = y[:, t * LANES:(t + 1) * LANES]
        o_ref[0, rows * ROW_SUB:, :] = jnp.zeros((o_ref.shape[1] - rows * ROW_SUB, LANES), F32)


def _expert_ffn(xs, w_gate, w_up, w_down, tv, layer):
    e, r, d = xs.shape
    f = w_gate.shape[3]
    tf = MOE_TF
    return pl.pallas_call(
        _expert_ffn_kernel,
        out_shape=jax.ShapeDtypeStruct((e, (r + COMBINE_CHUNK) * ROW_SUB, LANES), F32),
        grid=(e, f // tf),
        in_specs=[pl.BlockSpec((1, r, d), lambda ei, j: (ei, 0, 0)),
                  pl.BlockSpec((1, 1, d, tf), lambda ei, j: (layer, ei, 0, j)),
                  pl.BlockSpec((1, 1, d, tf), lambda ei, j: (layer, ei, 0, j)),
                  pl.BlockSpec((1, 1, tf, d), lambda ei, j: (layer, ei, j, 0)),
                  pl.BlockSpec((1, r, 1), lambda ei, j: (ei, 0, 0))],
        out_specs=pl.BlockSpec((1, (r + COMBINE_CHUNK) * ROW_SUB, LANES), lambda ei, j: (ei, 0, 0)),
        scratch_shapes=[pltpu.VMEM((r, d), F32)],
        compiler_params=_cparams(("parallel", "arbitrary"), VMEM_LIMIT),
        name="expert_ffn",
    )(xs, w_gate, w_up, w_down, tv)


COMBINE_UNROLL = 4
COMBINE_PAD = 8


def _combine_kernel(alpha, n_exp, r, n_tiles, idx_ref, st_ref, y_hbm, x_ref, mod_ref, lw_ref, lb_ref, o_ref,
                    acc, head, tail, head_sem, tail_sem):
    g = pl.program_id(0) * pl.num_programs(1) + pl.program_id(1)
    tile_base = g * TOKEN_TILE
    ch = COMBINE_CHUNK
    cur = g % 2

    def run(e, tile):
        s0 = st_ref[e * (n_tiles + 1) + tile]
        return s0, st_ref[e * (n_tiles + 1) + tile + 1] - s0

    def rows_of(first, count):
        start = first * ROW_SUB
        return pl.ds(start if isinstance(start, int) else pl.multiple_of(start, ROW_SUB), count * ROW_SUB)

    def head_copy(e, tile, slot):
        s0, cnt = run(e, tile)
        return pltpu.make_async_copy(y_hbm.at[e, rows_of(s0, ch)], head.at[slot, e, rows_of(0, ch)],
                                     head_sem.at[slot, e]), cnt

    def start_heads(tile, slot):
        for e in range(n_exp):
            copy, cnt = head_copy(e, tile, slot)
            pl.when(cnt > 0)(copy.start)

    @pl.when(g == 0)
    def _():
        head[...] = jnp.zeros_like(head)
        tail[...] = jnp.zeros_like(tail)
        start_heads(0, 0)

    @pl.when(g + 1 < n_tiles)
    def _():
        start_heads(g + 1, 1 - cur)

    acc[...] = jnp.zeros_like(acc)

    def add_rows(src, first, count):
        def body(it, carry):
            for u in range(COMBINE_UNROLL):
                j = it * COMBINE_UNROLL + u
                t = jnp.where(j < count, idx_ref[first + j] - tile_base, TOKEN_TILE)
                acc[rows_of(t, 1), :] = acc[rows_of(t, 1), :] + src[rows_of(j, 1), :]
            return carry
        lax.fori_loop(0, (count + COMBINE_UNROLL - 1) // COMBINE_UNROLL, body, 0)

    for e in range(n_exp):
        s0, cnt = run(e, g)
        copy, _ = head_copy(e, g, cur)
        pl.when(cnt > 0)(copy.wait)
        add_rows(head.at[cur, e], e * r + s0, jnp.minimum(cnt, ch))
        for c in range(1, TOKEN_TILE // ch):
            @pl.when(cnt > c * ch)
            def _(e=e, c=c, s0=s0, cnt=cnt):
                more = pltpu.make_async_copy(y_hbm.at[e, rows_of(s0 + c * ch, ch)], tail.at[rows_of(0, ch)], tail_sem)
                more.start()
                more.wait()
                add_rows(tail, e * r + s0 + c * ch, jnp.minimum(cnt - c * ch, ch))

    f = jnp.concatenate([acc[pl.ds(j, TOKEN_TILE, stride=ROW_SUB), :] for j in range(ROW_SUB)], axis=1)
    res = alpha * x_ref[0] + mod_ref[0, 0, 5:6, :] * f
    o_ref[0] = _ln_rows(res) * lw_ref[...] + lb_ref[...]


def _combine_residual_ln(y, idx, starts, x, mod, ln_w, ln_b, alpha):
    b, na, d = x.shape
    e, r = idx.shape
    n_tiles = b * na // TOKEN_TILE
    tile = pl.BlockSpec((1, TOKEN_TILE, d), lambda bi, i, *_: (bi, i, 0))
    vec = pl.BlockSpec((1, d), lambda bi, i, *_: (0, 0))
    chunk_rows = (COMBINE_CHUNK + COMBINE_PAD) * ROW_SUB
    return pl.pallas_call(
        functools.partial(_combine_kernel, alpha, e, r, n_tiles),
        out_shape=jax.ShapeDtypeStruct((b, na, d), F32),
        grid_spec=pltpu.PrefetchScalarGridSpec(
            num_scalar_prefetch=2, grid=(b, na // TOKEN_TILE),
            in_specs=[pl.BlockSpec(memory_space=pl.ANY), tile,
                      pl.BlockSpec((1, 1, N_MOD, d), lambda bi, i, *_: (bi, jnp.minimum(i, 1), 0, 0)), vec, vec],
            out_specs=tile,
            scratch_shapes=[pltpu.VMEM(((TOKEN_TILE + COMBINE_PAD) * ROW_SUB, LANES), F32),
                            pltpu.VMEM((2, e, chunk_rows, LANES), F32),
                            pltpu.VMEM((chunk_rows, LANES), F32),
                            pltpu.SemaphoreType.DMA((2, e)), pltpu.SemaphoreType.DMA(())]),
        compiler_params=_cparams(("arbitrary", "arbitrary"), VMEM_LIMIT),
        name="moe_combine_ln",
    )(jnp.pad(idx.reshape(-1), (0, COMBINE_PAD)), starts.reshape(-1), y, x, mod, ln_w.reshape(1, d), ln_b.reshape(1, d))


def _moe_residual_ln(x, mod, router_w, router_b, w_gate, w_up, w_down, layer, ln_w, ln_b, alpha):
    b, na, d = x.shape
    n_lat = na - CTX_LEN
    cap_c = EC_CAPACITY_FACTOR * CTX_LEN // N_EXPERTS
    cap_l = EC_CAPACITY_FACTOR * n_lat // N_EXPERTS
    h, aff = _router(x, mod, router_w, router_b)
    sel, _ = _select(aff, ((0, CTX_LEN, cap_c), (CTX_LEN, na, cap_l)))
    idx_c = jnp.argsort(1 - sel[:, :, :CTX_LEN], axis=-1, stable=True)[..., :cap_c]
    idx_l = jnp.argsort(1 - sel[:, :, CTX_LEN:], axis=-1, stable=True)[..., :cap_l] + CTX_LEN
    idx = jnp.concatenate([idx_c, idx_l], axis=-1).astype(jnp.int32)
    tv = jnp.take_along_axis(aff, idx, axis=-1)
    flat = idx + (jnp.arange(b, dtype=jnp.int32) * na)[:, None, None]
    flat = jnp.transpose(flat, (1, 0, 2)).reshape(N_EXPERTS, -1)
    tv = jnp.transpose(tv, (1, 0, 2)).reshape(N_EXPERTS, -1, 1)
    per_tile = jnp.sum(sel.reshape(b, N_EXPERTS, na // TOKEN_TILE, TOKEN_TILE), axis=-1)
    per_tile = jnp.transpose(per_tile, (1, 0, 2)).reshape(N_EXPERTS, -1)
    starts = jnp.concatenate([jnp.zeros((N_EXPERTS, 1), jnp.int32), jnp.cumsum(per_tile, axis=1, dtype=jnp.int32)], axis=1)
    xs = _gather_rows(h.reshape(b * na * ROW_SUB, LANES), flat)
    y = _expert_ffn(xs, w_gate, w_up, w_down, tv, layer)
    return _combine_residual_ln(y, flat, starts, x, mod, ln_w, ln_b, alpha)


def _dwconv_kernel(silu, x_ref, w_ref, b_ref, o_ref):
    x = x_ref[0]
    na = x.shape[0]
    row = lax.broadcasted_iota(jnp.int32, x.shape, 0)
    first = (row == 0) | (row == CTX_LEN)
    last = (row == CTX_LEN - 1) | (row == na - 1)
    prev = jnp.where(first, 0.0, pltpu.roll(x, 1, axis=0))
    nxt = jnp.where(last, 0.0, pltpu.roll(x, na - 1, axis=0))
    y = w_ref[0:1, :] * prev + w_ref[1:2, :] * x + w_ref[2:3, :] * nxt + b_ref[...]
    if silu:
        y = y * jax.nn.sigmoid(y)
    o_ref[0, 0] = y


def _segment_dwconv(p, w, bias, groups, silu):
    b, na, c = p.shape
    per = c // groups // LANES
    return pl.pallas_call(
        functools.partial(_dwconv_kernel, silu),
        out_shape=jax.ShapeDtypeStruct((groups, b, na, c // groups), F32),
        grid=(b, c // LANES),
        in_specs=[pl.BlockSpec((1, na, LANES), lambda bi, j: (bi, 0, j)),
                  pl.BlockSpec((SHORT_CONV, LANES), lambda bi, j: (0, j)),
                  pl.BlockSpec((1, LANES), lambda bi, j: (0, j))],
        out_specs=pl.BlockSpec((1, 1, na, LANES), lambda bi, j: (j // per, bi, 0, j % per)),
        compiler_params=_cparams(("parallel", "parallel"), VMEM_LIMIT),
        name="short_conv",
    )(p, w, bias.reshape(1, c))


def _rope_tables(n_lat):
    rows = n_lat // GRID_W
    row = jnp.repeat(jnp.arange(rows, dtype=F32), GRID_W)
    col = (jnp.arange(n_lat) % GRID_W).astype(F32)
    nf = ATTN_HEAD_DIM // 4
    inv = ROPE_THETA ** (-jnp.arange(nf, dtype=F32) / nf)
    ar = row[:, None] * inv
    ac = col[:, None] * inv
    ang = jnp.concatenate([ar, ar, ac, ac], axis=-1)
    return jnp.cos(ang), jnp.sin(ang)


def _mixing(x, mod, w_in, mlstm_conv_w, mlstm_conv_b, mlstm_gate_b, mlstm_norm_w, attn_q_norm_w,
            attn_k_norm_w, hyena_conv_w, hyena_conv_b, hyena_filter, hyena_skip, with_ctx_out):
    b, na, _ = x.shape
    n_lat = na - CTX_LEN
    mqk, mv, mo, aq, ak, av, hy, mg = _in_projection(x, mod, w_in)
    mg = mg[..., :MLSTM_GATES]

    qk = _segment_dwconv(mqk, _pad_heads(mlstm_conv_w), _pad_heads(mlstm_conv_b), groups=1,
                         silu=True)[0]
    g = mg.reshape(b, na, N_DIR, 2, MLSTM_HEADS) + mlstm_gate_b
    g = jnp.stack([g[:, :, :, 0], jax.nn.log_sigmoid(g[:, :, :, 1])], axis=3)
    gcols = jnp.transpose(g, (2, 0, 1, 3, 4)).reshape(N_DIR, b, na, 2 * MLSTM_HEADS)
    grows = jnp.transpose(gcols, (0, 1, 3, 2))
    hm = _mlstm_scan(qk, mv, grows, gcols)

    q, k, v = _qkv_prep(aq, ak, av, attn_q_norm_w, attn_k_norm_w)
    ya = _attention(q, k, v)

    hv, hx1, hx2 = _segment_dwconv(hy, hyena_conv_w, hyena_conv_b, groups=3, silu=False)
    yh = _hyena_latent(hv[:, CTX_LEN:], hx1[:, CTX_LEN:], hx2[:, CTX_LEN:],
                       *_hyena_taps(n_lat, *hyena_filter), hyena_skip)
    if with_ctx_out:
        yh_c = _hyena_context(hv[:, :CTX_LEN], hx1[:, :CTX_LEN], hx2[:, :CTX_LEN],
                              *_hyena_taps(CTX_LEN, *hyena_filter), hyena_skip)
    else:
        yh_c = jnp.zeros((b, CTX_LEN, D_HYENA), F32)
    yh = jnp.concatenate([yh_c, yh], axis=1)
    return hm, mo, ya, yh


def kernel(x, c, ctx, c_ctx, w_mod, b_mod, w_in, mlstm_conv_w, mlstm_conv_b, mlstm_gate_b, mlstm_norm_w, attn_q_norm_w, attn_k_norm_w, hyena_conv_w, hyena_conv_b, hyena_f_w1, hyena_f_b1, hyena_f_freq, hyena_f_w2, hyena_f_b2, hyena_f_w3, hyena_skip, w_out, ln_mix_w, ln_mix_b, router_w, router_b, exp_w_gate, exp_w_up, exp_w_down, ln_ffn_w, ln_ffn_b):
    bsz, seq, d = x.shape
    assert d == D_MODEL and ctx.shape[1] == CTX_LEN == TOKEN_TILE and seq == FFT_R * FFT_R // 2
    alpha = (2.0 * DEPTH) ** 0.25
    xa = jnp.concatenate([ctx, x], axis=1)
    crows = jnp.concatenate([c, jnp.broadcast_to(c_ctx, (8 - bsz, d))], axis=0)
    for l in range(DEPTH):
        last = l == DEPTH - 1
        m = _modulation(crows, w_mod, b_mod[l], l).reshape(8, N_MOD, d)
        mod = jnp.stack([jnp.broadcast_to(m[bsz], (bsz, N_MOD, d)), m[:bsz]], axis=1)
        hm, mo, ya, yh = _mixing(
            xa, mod, w_in[l], mlstm_conv_w[l], mlstm_conv_b[l], mlstm_gate_b[l], mlstm_norm_w[l],
            attn_q_norm_w[l], attn_k_norm_w[l], hyena_conv_w[l], hyena_conv_b[l],
            (hyena_f_w1[l], hyena_f_b1[l], hyena_f_freq[l], hyena_f_w2[l], hyena_f_b2[l], hyena_f_w3[l]),
            hyena_skip[l], not last)
        xa = _out_projection(hm, mo, mlstm_norm_w[l], ya, yh, xa, mod, w_out[l].astype(BF16),
                             ln_mix_w[l], ln_mix_b[l], alpha)
        xa = _moe_residual_ln(xa, mod, router_w[l], router_b[l], exp_w_gate, exp_w_up, exp_w_down, l,
                              ln_ffn_w[l], ln_ffn_b[l], alpha)
    return xa[:, CTX_LEN:]
```

```python
import functools
import math

import numpy as np
import jax
import jax.numpy as jnp
from jax import lax
from jax.experimental import pallas as pl
from jax.experimental.pallas import tpu as pltpu

F32 = jnp.float32
BF16 = jnp.bfloat16
HI = lax.Precision.HIGHEST

D_MODEL = 1024
DEPTH = 2
GRID_W = 64
CTX_LEN = 256
N_DIR = 2
SHORT_CONV = 3

D_MLSTM = 256
MLSTM_HEAD_DIM = 64
MLSTM_HEADS = 4
MLSTM_GATES = 16
MLSTM_CHUNK = 128

D_ATTN = 512
ATTN_HEAD_DIM = 64
ATTN_HEADS = 8
ATTN_KV_HEADS = 2
ATTN_GROUP = 4
D_KV = 128
ROPE_THETA = 10000.0

D_HYENA = 256
HYENA_ORDER = 2
HYENA_BANDS = 16
HYENA_FAST_DECAY = 0.3
HYENA_SLOW_DECAY = 1.5
HYENA_DECAY_TARGET = 1e-2
HYENA_WINDOW_SHIFT = 0.05

N_IN = 4 * D_MLSTM + MLSTM_GATES + D_ATTN + 2 * D_KV + 3 * D_HYENA
N_EXPERTS = 16
EC_CAPACITY_FACTOR = 2
D_FF_EXPERT = 2816
N_MOD = 6
LN_EPS = 1e-5
RMS_EPS = 1e-6

LANES = 128
ROW_SUB = D_MODEL // LANES
TOKEN_TILE = 256
FFT_R = 128
FFT_GROUP = 8
VMEM_LIMIT = 56 * 1024 * 1024


def _cparams(sem, vmem=None):
    return pltpu.CompilerParams(dimension_semantics=sem, vmem_limit_bytes=vmem)


def _mod_kernel(c_ref, w_ref, b_ref, o_ref):
    cs = c_ref[...]
    cs = cs * jax.nn.sigmoid(cs)
    o_ref[...] = jnp.dot(cs, w_ref[0], precision=HI, preferred_element_type=F32) + b_ref[...]


def _modulation(crows, w_mod, b_mod, layer):
    rows, d = crows.shape
    n = w_mod.shape[2]
    tn = 1024
    return pl.pallas_call(
        _mod_kernel,
        out_shape=jax.ShapeDtypeStruct((rows, n), F32),
        grid=(n // tn,),
        in_specs=[pl.BlockSpec((rows, d), lambda j: (0, 0)),
                  pl.BlockSpec((1, d, tn), lambda j: (layer, 0, j)),
                  pl.BlockSpec((1, tn), lambda j: (0, j))],
        out_specs=pl.BlockSpec((rows, tn), lambda j: (0, j)),
        compiler_params=_cparams(("parallel",)),
        name="adaln_mod",
    )(crows, w_mod, b_mod.reshape(1, n))


def _ln_rows(x):
    mu = jnp.mean(x, axis=-1, keepdims=True)
    xc = x - mu
    var = jnp.mean(xc * xc, axis=-1, keepdims=True)
    return xc * lax.rsqrt(var + LN_EPS)


def _mod_spec():
    return pl.BlockSpec((1, 1, N_MOD, D_MODEL), lambda b, i: (b, jnp.minimum(i, 1), 0, 0))


HEAD_DIM = 64
_C_MG = 4 * D_MLSTM
_C_AQ = _C_MG + MLSTM_GATES
IN_GROUPS = (("mqk", 0, 2 * D_MLSTM, True), ("mv", 2 * D_MLSTM, D_MLSTM, True), ("mo", 3 * D_MLSTM, D_MLSTM, False),
             ("aq", _C_AQ, D_ATTN, True), ("ak", _C_AQ + D_ATTN, D_KV, True), ("av", _C_AQ + D_ATTN + D_KV, D_KV, True),
             ("hy", _C_AQ + D_ATTN + 2 * D_KV, 3 * D_HYENA, False), ("mg", _C_MG, MLSTM_GATES, False))


def _group_width(src_width, head_padded):
    return src_width // HEAD_DIM * LANES if head_padded else -(-src_width // LANES) * LANES


def _pad_heads(a, axis=-1):
    axis = axis % a.ndim
    nh = a.shape[axis] // HEAD_DIM
    a = a.reshape(a.shape[:axis] + (nh, HEAD_DIM) + a.shape[axis + 1:])
    pad = [(0, 0)] * a.ndim
    pad[axis + 1] = (0, LANES - HEAD_DIM)
    a = jnp.pad(a, pad)
    return a.reshape(a.shape[:axis] + (nh * LANES,) + a.shape[axis + 2:])


def _inproj_kernel(x_ref, mod_ref, w_ref, *o_refs):
    y = _ln_rows(x_ref[0])
    h = (y * (1.0 + mod_ref[0, 0, 1:2, :]) + mod_ref[0, 0, 0:1, :]).astype(BF16)
    off = 0
    for o_ref in o_refs:
        width = o_ref.shape[2]
        o_ref[0] = jnp.dot(h, w_ref[:, off:off + width], preferred_element_type=F32)
        off += width


def _in_projection(x, mod, w_in):
    b, na, d = x.shape
    cols = []
    for _, start, src, head_padded in IN_GROUPS:
        wg = w_in[:, start:start + src]
        if head_padded:
            wg = _pad_heads(wg)
        elif src % LANES:
            wg = jnp.pad(wg, ((0, 0), (0, LANES - src % LANES)))
        cols.append(wg)
    w = jnp.concatenate(cols, axis=1).astype(BF16)
    n = w.shape[1]
    widths = [_group_width(src, hp) for _, _, src, hp in IN_GROUPS]
    return pl.pallas_call(
        _inproj_kernel,
        out_shape=[jax.ShapeDtypeStruct((b, na, width), F32) for width in widths],
        grid=(b, na // TOKEN_TILE),
        in_specs=[pl.BlockSpec((1, TOKEN_TILE, d), lambda bi, i: (bi, i, 0)),
                  _mod_spec(),
                  pl.BlockSpec((d, n), lambda bi, i: (0, 0))],
        out_specs=[pl.BlockSpec((1, TOKEN_TILE, width), lambda bi, i: (bi, i, 0)) for width in widths],
        compiler_params=_cparams(("parallel", "parallel"), VMEM_LIMIT),
        name="in_proj",
    )(x, mod, w)


def _compact_heads(x):
    tiles = [x[:, j * LANES:(j + 1) * LANES] for j in range(x.shape[1] // LANES)]
    return jnp.concatenate([tiles[j] + pltpu.roll(tiles[j + 1], HEAD_DIM, axis=1) for j in range(0, len(tiles), 2)],
                           axis=1)


def _outproj_kernel(alpha, hm0_ref, hm1_ref, mo_ref, nw_ref, seg_ref, ya_ref, yh_ref, x_ref, mod_ref, w_ref,
                    lw_ref, lb_ref, o_ref):
    hm = _compact_heads(hm0_ref[0, 0] + hm1_ref[0, 0])
    ss = jnp.dot(hm * hm, seg_ref[...], precision=HI, preferred_element_type=F32)
    hn = hm * lax.rsqrt(ss * (1.0 / MLSTM_HEAD_DIM) + RMS_EPS) * nw_ref[...]
    ym = jax.nn.sigmoid(mo_ref[0]) * hn
    mix = jnp.dot(ym.astype(BF16), w_ref[0:D_MLSTM, :], preferred_element_type=F32)
    mix = mix + jnp.dot(_compact_heads(ya_ref[0]).astype(BF16), w_ref[D_MLSTM:D_MLSTM + D_ATTN, :],
                        preferred_element_type=F32)
    mix = mix + jnp.dot(yh_ref[0].astype(BF16), w_ref[D_MLSTM + D_ATTN:, :], preferred_element_type=F32)
    r = alpha * x_ref[0] + mod_ref[0, 0, 2:3, :] * mix
    o_ref[0] = _ln_rows(r) * lw_ref[...] + lb_ref[...]


def _out_projection(hm, mo, norm_w, ya, yh, x, mod, w_bf16, ln_w, ln_b, alpha):
    b, na, d = x.shape
    dm = mo.shape[2]
    head = np.arange(dm) // MLSTM_HEAD_DIM
    seg = jnp.asarray((head[:, None] == head[None, :]).astype(np.float32))

    def tile(width):
        return pl.BlockSpec((1, TOKEN_TILE, width), lambda bi, i: (bi, i, 0))

    def hm_spec(direction):
        return pl.BlockSpec((1, 1, TOKEN_TILE, hm.shape[3]), lambda bi, i: (direction, bi, i, 0))

    vec = pl.BlockSpec((1, d), lambda bi, i: (0, 0))
    return pl.pallas_call(
        functools.partial(_outproj_kernel, alpha),
        out_shape=jax.ShapeDtypeStruct((b, na, d), F32),
        grid=(b, na // TOKEN_TILE),
        in_specs=[hm_spec(0), hm_spec(1), tile(dm), pl.BlockSpec((1, dm), lambda bi, i: (0, 0)),
                  pl.BlockSpec((dm, dm), lambda bi, i: (0, 0)), tile(ya.shape[2]), tile(yh.shape[2]), tile(d),
                  _mod_spec(), pl.BlockSpec((d, d), lambda bi, i: (0, 0)), vec, vec],
        out_specs=tile(d),
        compiler_params=_cparams(("parallel", "parallel"), VMEM_LIMIT),
        name="out_proj_ln",
    )(hm, hm, mo, norm_w.reshape(1, dm), seg, ya, yh, x, mod, w_bf16, ln_w.reshape(1, d), ln_b.reshape(1, d))


def _mlstm_kernel(qk_ref, v_ref, gr_ref, gc_ref, o_ref, c_sc, n_sc, m_sc):
    t = MLSTM_CHUNK
    nh = MLSTM_HEADS
    d = pl.program_id(0)

    @pl.when(pl.program_id(2) == 0)
    def _():
        c_sc[...] = jnp.zeros_like(c_sc)
        n_sc[...] = jnp.zeros_like(n_sc)
        m_sc[...] = jnp.zeros_like(m_sc)

    row = lax.broadcasted_iota(jnp.int32, (t, t), 0)
    col = lax.broadcasted_iota(jnp.int32, (t, t), 1)
    mask = jnp.where(d == 0, col - row, row - col) <= 0
    maskb = mask.astype(BF16)
    grows = gr_ref[0, 0]
    gcols = gc_ref[0, 0]

    def split3(a):
        hi = a.astype(BF16)
        rest = a - hi.astype(F32)
        mid = rest.astype(BF16)
        return hi, mid, (rest - mid.astype(F32)).astype(BF16)

    def dot01(parts, ones, dims=(((1,), (0,)), ((), ()))):
        return sum(lax.dot_general(p, ones, dims, preferred_element_type=F32) for p in parts)

    def ones_dot(ones, parts):
        return sum(jnp.dot(ones, p, preferred_element_type=F32) for p in parts)

    gcol_parts = split3(gcols)
    cum_cols = ones_dot(maskb, gcol_parts)
    cum_rows = dot01(split3(grows), maskb, (((1,), (1,)), ((), ())))
    spread = (lax.broadcasted_iota(jnp.int32, (2 * nh, 2 * nh * LANES), 1) // LANES
              == lax.broadcasted_iota(jnp.int32, (2 * nh, 2 * nh * LANES), 0)).astype(BF16)
    g_lanes = dot01(gcol_parts, spread)
    cum_lanes = dot01(split3(cum_cols), spread)

    def column(a, idx):
        return a[:, idx * LANES:(idx + 1) * LANES]

    dh = LANES
    for h in range(nh):
        qc = qk_ref[0, :, h * dh:(h + 1) * dh] * (MLSTM_HEAD_DIM ** -0.5)
        kc = qk_ref[0, :, (nh + h) * dh:(nh + h + 1) * dh]
        vc = v_ref[0, :, h * dh:(h + 1) * dh]
        ic_row = grows[h:h + 1, :]
        ic_col = column(g_lanes, h)
        bcol = column(cum_lanes, nh + h)
        brow = cum_rows[nh + h:nh + h + 1, :]
        tot = jnp.sum(grows[nh + h:nh + h + 1, :], axis=1, keepdims=True)
        m0 = m_sc[h:h + 1, 0:1]
        log_inter = bcol + m0
        log_intra = jnp.where(mask, bcol - brow + ic_row, -jnp.inf)
        mrow = jnp.maximum(log_inter, jnp.max(log_intra, axis=1, keepdims=True))
        w_inter = jnp.exp(log_inter - mrow)
        qb = qc.astype(BF16)
        vb = vc.astype(BF16)
        scores = lax.dot_general(qb, kc.astype(BF16), (((1,), (1,)), ((), ())),
                                 preferred_element_type=F32) * jnp.exp(log_intra - mrow)
        ct = c_sc[h]
        n0 = n_sc[h]
        num = (w_inter * jnp.dot(qb, ct.astype(BF16), preferred_element_type=F32)
               + jnp.dot(scores.astype(BF16), vb, preferred_element_type=F32))
        den = w_inter * jnp.sum(qc * n0, axis=1, keepdims=True) + jnp.sum(scores, axis=1, keepdims=True)
        o_ref[0, 0, :, h * dh:(h + 1) * dh] = num / jnp.maximum(jnp.abs(den), jnp.exp(-mrow))
        log_src = tot - bcol + ic_col
        m_new = jnp.maximum(tot + m0, jnp.max(log_src, axis=0, keepdims=True))
        wk = jnp.exp(log_src - m_new) * kc
        decay = jnp.exp(tot + m0 - m_new)
        c_sc[h] = decay * ct + lax.dot_general(wk.astype(BF16), vb, (((0,), (0,)), ((), ())),
                                               preferred_element_type=F32)
        n_sc[h] = decay * n0 + jnp.sum(wk, axis=0, keepdims=True)
        m_sc[h:h + 1, :] = jnp.broadcast_to(m_new, (1, 128))


def _mlstm_scan(qk, v, grows, gcols):
    b, na, dv = v.shape
    nh, dh = MLSTM_HEADS, LANES
    t = MLSTM_CHUNK
    nc = na // t
    nctx = CTX_LEN // t

    def chunk(d, c):
        rev = jnp.where(c < nctx, nctx - 1 - c, nc + nctx - 1 - c)
        return jnp.where(d == 0, c, rev)

    return pl.pallas_call(
        _mlstm_kernel,
        out_shape=jax.ShapeDtypeStruct((N_DIR, b, na, dv), F32),
        grid=(N_DIR, b, nc),
        in_specs=[pl.BlockSpec((1, t, 2 * dv), lambda d, bi, c: (bi, chunk(d, c), 0)),
                  pl.BlockSpec((1, t, dv), lambda d, bi, c: (bi, chunk(d, c), 0)),
                  pl.BlockSpec((1, 1, 2 * nh, t), lambda d, bi, c: (d, bi, 0, chunk(d, c))),
                  pl.BlockSpec((1, 1, t, 2 * nh), lambda d, bi, c: (d, bi, chunk(d, c), 0))],
        out_specs=pl.BlockSpec((1, 1, t, dv), lambda d, bi, c: (d, bi, chunk(d, c), 0)),
        scratch_shapes=[pltpu.VMEM((nh, dh, dh), F32), pltpu.VMEM((nh, 1, dh), F32), pltpu.VMEM((8, 128), F32)],
        compiler_params=_cparams(("parallel", "parallel", "arbitrary")),
        name="mlstm_scan",
    )(qk, v, grows, gcols)


ATTN_TQ = 256
LOG2E = 1.4426950408889634


def _qkv_prep_kernel(aq_ref, ak_ref, av_ref, cos_ref, sin_ref, qw_ref, kw_ref, q_ref, k_ref, v_ref):
    lane = lax.broadcasted_iota(jnp.int32, (aq_ref.shape[1], LANES), 1)
    head_lane = lane < ATTN_HEAD_DIM
    first_half = (lane % (ATTN_HEAD_DIM // 2)) < (ATTN_HEAD_DIM // 4)
    cos = cos_ref[...]
    sin = sin_ref[...]
    quarter = ATTN_HEAD_DIM // 4

    def norm_rope(x, w):
        y = x * lax.rsqrt(jnp.sum(x * x, axis=1, keepdims=True) * (1.0 / ATTN_HEAD_DIM) + RMS_EPS) * w
        rot = jnp.where(first_half, -pltpu.roll(y, LANES - quarter, axis=1), pltpu.roll(y, quarter, axis=1))
        return jnp.where(head_lane, y * cos + rot * sin, 0.0)

    for h in range(ATTN_HEADS):
        q_ref[0, h] = norm_rope(aq_ref[0, :, h * LANES:(h + 1) * LANES], qw_ref[...]).astype(BF16)
    for h in range(ATTN_KV_HEADS):
        k_ref[0, h] = norm_rope(ak_ref[0, :, h * LANES:(h + 1) * LANES], kw_ref[...]).astype(BF16)
        v_ref[0, h] = jnp.where(lane == ATTN_HEAD_DIM, 1.0, av_ref[0, :, h * LANES:(h + 1) * LANES]).astype(BF16)


def _qkv_prep(aq, ak, av, q_norm_w, k_norm_w):
    b, na, _ = aq.shape
    cos, sin = _rope_tables(na - CTX_LEN)
    pad = ((CTX_LEN, 0), (0, LANES - ATTN_HEAD_DIM))
    cos = jnp.pad(cos, pad, constant_values=1.0)
    sin = jnp.pad(sin, pad)
    qw = jnp.pad(q_norm_w * (ATTN_HEAD_DIM ** -0.5 * LOG2E), (0, LANES - ATTN_HEAD_DIM)).reshape(1, LANES)
    kw = jnp.pad(k_norm_w, (0, LANES - ATTN_HEAD_DIM)).reshape(1, LANES)
    tok = lambda width: pl.BlockSpec((1, TOKEN_TILE, width), lambda bi, i: (bi, i, 0))
    tab = pl.BlockSpec((TOKEN_TILE, LANES), lambda bi, i: (i, 0))
    vec = pl.BlockSpec((1, LANES), lambda bi, i: (0, 0))
    heads = lambda nh: pl.BlockSpec((1, nh, TOKEN_TILE, LANES), lambda bi, i: (bi, 0, i, 0))
    return pl.pallas_call(
        _qkv_prep_kernel,
        out_shape=(jax.ShapeDtypeStruct((b, ATTN_HEADS, na, LANES), BF16),
                   jax.ShapeDtypeStruct((b, ATTN_KV_HEADS, na, LANES), BF16),
                   jax.ShapeDtypeStruct((b, ATTN_KV_HEADS, na, LANES), BF16)),
        grid=(b, na // TOKEN_TILE),
        in_specs=[tok(aq.shape[2]), tok(ak.shape[2]), tok(av.shape[2]), tab, tab, vec, vec],
        out_specs=(heads(ATTN_HEADS), heads(ATTN_KV_HEADS), heads(ATTN_KV_HEADS)),
        compiler_params=_cparams(("parallel", "parallel")),
        name="qkv_prep",
    )(aq, ak, av, cos, sin, qw, kw)


def _attn_kernel(q_ref, k_ref, v_ref, o_ref, s_ref, p_ref):
    def attend(nk):
        s_ref[:, 0:nk] = lax.dot_general(q_ref[0, 0], k_ref[0, 0, 0:nk, :], (((1,), (1,)), ((), ())),
                                         preferred_element_type=F32)
        m = jnp.max(s_ref[:, 0:nk], axis=1, keepdims=True)
        p_ref[:, 0:nk] = jnp.exp2(s_ref[:, 0:nk] - m).astype(BF16)
        acc = jnp.dot(p_ref[:, 0:nk], v_ref[0, 0, 0:nk, :], preferred_element_type=F32)
        lane = lax.broadcasted_iota(jnp.int32, acc.shape, 1)
        o_ref[0] = jnp.where(lane < ATTN_HEAD_DIM, acc / acc[:, ATTN_HEAD_DIM:ATTN_HEAD_DIM + 1], 0.0)

    is_ctx = pl.program_id(2) < CTX_LEN // ATTN_TQ
    pl.when(is_ctx)(lambda: attend(CTX_LEN))
    pl.when(jnp.logical_not(is_ctx))(lambda: attend(k_ref.shape[2]))


def _attention(q, k, v):
    b, nh, na, _ = q.shape
    group = nh // k.shape[1]
    kv_spec = pl.BlockSpec((1, 1, na, LANES), lambda bi, h, i: (bi, h // group, 0, 0))
    return pl.pallas_call(
        _attn_kernel,
        out_shape=jax.ShapeDtypeStruct((b, na, nh * LANES), F32),
        grid=(b, nh, na // ATTN_TQ),
        in_specs=[pl.BlockSpec((1, 1, ATTN_TQ, LANES), lambda bi, h, i: (bi, h, i, 0)), kv_spec, kv_spec],
        out_specs=pl.BlockSpec((1, ATTN_TQ, LANES), lambda bi, h, i: (bi, i, h)),
        scratch_shapes=[pltpu.VMEM((ATTN_TQ, na), F32), pltpu.VMEM((ATTN_TQ, na), BF16)],
        compiler_params=_cparams(("parallel", "parallel", "parallel"), VMEM_LIMIT),
        name="attention",
    )(q, k, v)


def _dft_tables(t2_len):
    r = FFT_R
    n = r * r
    idx = np.arange(r, dtype=np.float64)
    kb = idx[None, :, None]
    t1 = idx[:, None, None]
    t2 = np.arange(t2_len, dtype=np.float64)[None, None, :]
    ang = -2.0 * np.pi * (t2 * kb / r + t1 * kb / n)
    f1 = np.concatenate([np.cos(ang), np.sin(ang)], axis=1)
    ang2 = -2.0 * np.pi * np.outer(idx, idx) / r
    f2 = np.stack([np.cos(ang2), np.sin(ang2)])
    return f1.astype(np.float32), f2.astype(np.float32)


def _idft_table(t2_len):
    r = FFT_R
    n = r * r
    t1 = np.arange(r, dtype=np.float64)[:, None, None]
    t2 = np.arange(t2_len, dtype=np.float64)[None, :, None]
    kb = np.arange(r, dtype=np.float64)[None, None, :]
    ang = 2.0 * np.pi * (t2 * kb / r + t1 * kb / n)
    return (np.stack([np.cos(ang), np.sin(ang)], axis=1) / n).astype(np.float32)


def _dot_hi(table, x):
    return jnp.dot(table, x.astype(BF16), preferred_element_type=F32)


def _stage_block(block, sc):
    rows = block.shape[0]
    for t in range(rows):
        for h in range(sc.shape[0]):
            sc[h, pl.ds(t * FFT_GROUP, FFT_GROUP), :] = block[t, :, h * LANES:(h + 1) * LANES]


def _middle_rows(sc, i):
    rows = sc.shape[1] // FFT_GROUP
    return jnp.concatenate([sc[h, pl.ds(i, rows, stride=FFT_GROUP), :] for h in range(sc.shape[0])], axis=1)


def _stage_scratch(rows, c):
    return pltpu.VMEM((c // LANES, rows * FFT_GROUP, LANES), F32)


def _fft1_kernel(f_ref, z_ref, g_ref, z_sc):
    _stage_block(z_ref.at[0], z_sc)
    for i in range(FFT_GROUP):
        g_ref[0, i] = _dot_hi(f_ref[i], _middle_rows(z_sc, i))


def _fft_stage1(z4, f1):
    b, t2_len, r, c = z4.shape
    g = FFT_GROUP
    return pl.pallas_call(
        _fft1_kernel,
        out_shape=jax.ShapeDtypeStruct((b, r, 2 * r, c), F32),
        grid=(b, r // g),
        in_specs=[pl.BlockSpec((g, 2 * r, t2_len), lambda bi, j: (j, 0, 0)),
                  pl.BlockSpec((1, t2_len, g, c), lambda bi, j: (bi, 0, j, 0))],
        out_specs=pl.BlockSpec((1, g, 2 * r, c), lambda bi, j: (bi, j, 0, 0)),
        scratch_shapes=[_stage_scratch(t2_len, c)],
        compiler_params=_cparams(("parallel", "parallel")),
        name="fft_stage1",
    )(f1, z4)


def _fft2_kernel(with_filter, f_ref, gr_ref, gi_ref, aux_ref, h_ref, gr_sc, gi_sc):
    fr = f_ref[0]
    fi = f_ref[1]
    _stage_block(gr_ref.at[0], gr_sc)
    _stage_block(gi_ref.at[0], gi_sc)
    for i in range(FFT_GROUP):
        gr = _middle_rows(gr_sc, i).astype(BF16)
        gi = _middle_rows(gi_sc, i).astype(BF16)
        xr = _dot_hi(fr, gr) - _dot_hi(fi, gi)
        xi = _dot_hi(fr, gi) + _dot_hi(fi, gr)
        if with_filter:
            tr = aux_ref[0, 0, i]
            ti = aux_ref[0, 1, i]
            yr = xr * tr - xi * ti
            yi = xr * ti + xi * tr
            xr = _dot_hi(fr, yr) + _dot_hi(fi, yi)
            xi = _dot_hi(fr, yi) - _dot_hi(fi, yr)
        else:
            xr = xr * aux_ref[0]
            xi = xi * aux_ref[0]
        h_ref[0, 0, i] = xr
        h_ref[0, 1, i] = xi


def _fft_stage2(g, f2, tf=None, scale=None):
    b, r, _, c = g.shape
    grp = FFT_GROUP
    nblk = r // grp
    in_specs = [pl.BlockSpec((2, r, r), lambda bi, j: (0, 0, 0)),
                pl.BlockSpec((1, r, grp, c), lambda bi, j: (bi, 0, j, 0)),
                pl.BlockSpec((1, r, grp, c), lambda bi, j: (bi, 0, nblk + j, 0))]
    args = [f2, g, g]
    if tf is not None:
        in_specs.append(pl.BlockSpec((1, 2, grp, r, c), lambda bi, j: (0, 0, j, 0, 0)))
        args.append(tf)
    else:
        in_specs.append(pl.BlockSpec((1, 1, c), lambda bi, j: (bi, 0, 0)))
        args.append(scale)
    return pl.pallas_call(
        functools.partial(_fft2_kernel, tf is not None),
        out_shape=jax.ShapeDtypeStruct((b, 2, r, r, c), F32),
        grid=(b, nblk),
        in_specs=in_specs,
        out_specs=pl.BlockSpec((1, 2, grp, r, c), lambda bi, j: (bi, 0, j, 0, 0)),
        scratch_shapes=[_stage_scratch(r, c), _stage_scratch(r, c)],
        compiler_params=_cparams(("parallel", "parallel"), VMEM_LIMIT),
        name="fft_stage2",
    )(*args)


def _ifft_kernel(e_ref, h_ref, z_ref, x_ref, skip_ref, o_ref, hr_sc, hi_sc, z_sc, x_sc, o_sc):
    _stage_block(h_ref.at[0, 0], hr_sc)
    _stage_block(h_ref.at[0, 1], hi_sc)
    _stage_block(z_ref.at[0], z_sc)
    _stage_block(x_ref.at[0], x_sc)
    rows = z_ref.shape[1]
    for i in range(FFT_GROUP):
        y = _dot_hi(e_ref[i, 0], _middle_rows(hr_sc, i)) - _dot_hi(e_ref[i, 1], _middle_rows(hi_sc, i))
        y = y * (1.0 / (FFT_R * FFT_R))
        out = _middle_rows(x_sc, i) * (y + skip_ref[...] * _middle_rows(z_sc, i))
        for h in range(o_sc.shape[0]):
            o_sc[h, pl.ds(i, rows, stride=FFT_GROUP), :] = out[:, h * LANES:(h + 1) * LANES]
    for t in range(rows):
        o_ref[0, t] = jnp.concatenate([o_sc[h, pl.ds(t * FFT_GROUP, FFT_GROUP), :] for h in range(o_sc.shape[0])],
                                      axis=1)


def _ifft_gate(h, e, z4, x4, skip):
    b, t2_len, r, c = z4.shape
    g = FFT_GROUP
    tok = pl.BlockSpec((1, t2_len, g, c), lambda bi, j: (bi, 0, j, 0))
    return pl.pallas_call(
        _ifft_kernel,
        out_shape=jax.ShapeDtypeStruct(z4.shape, F32),
        grid=(b, r // g),
        in_specs=[pl.BlockSpec((g, 2, t2_len, r), lambda bi, j: (j, 0, 0, 0)),
                  pl.BlockSpec((1, 2, r, g, c), lambda bi, j: (bi, 0, 0, j, 0)),
                  tok, tok, pl.BlockSpec((1, c), lambda bi, j: (0, 0))],
        out_specs=tok,
        scratch_shapes=[_stage_scratch(r, c), _stage_scratch(r, c), _stage_scratch(t2_len, c),
                        _stage_scratch(t2_len, c), _stage_scratch(t2_len, c)],
        compiler_params=_cparams(("parallel", "parallel"), VMEM_LIMIT),
        name="ifft_gate",
    )(e, h, z4, x4, skip.reshape(1, c))


TAPS_ROWS = 1024
FEAT_PAD = 128


def _taps_kernel(n, feat_ref, w1_ref, b1_ref, freq_ref, w2_ref, b2_ref, w3_ref, delta_ref, taps_ref, l1_ref):
    c = D_HYENA
    rows = feat_ref.shape[0]
    step = pl.program_id(0)

    @pl.when(step == 0)
    def _():
        l1_ref[...] = jnp.zeros_like(l1_ref)

    f = feat_ref[...]
    hid = jnp.sin(freq_ref[0:1, :] * (jnp.dot(f, w1_ref[...], precision=HI, preferred_element_type=F32) + b1_ref[...]))
    hid = jnp.sin(freq_ref[1:2, :] * (jnp.dot(hid, w2_ref[...], precision=HI, preferred_element_type=F32) + b2_ref[...]))
    filt = jnp.dot(hid, w3_ref[...], precision=HI, preferred_element_type=F32)
    window = jnp.exp(-f[:, 0:1] * delta_ref[...]) + HYENA_WINDOW_SHIFT
    i = step * rows + lax.broadcasted_iota(jnp.int32, (rows, c), 0)
    for o in range(HYENA_ORDER):
        fwd = filt[:, o * c:(o + 1) * c]
        bwd = filt[:, (HYENA_ORDER + o) * c:(HYENA_ORDER + o + 1) * c]
        tap = jnp.where(i < n, fwd, jnp.where(i > n, bwd, 0.0)) * window
        taps_ref[o] = tap
        l1_ref[o:o + 1, :] += jnp.sum(jnp.abs(tap), axis=0, keepdims=True)


def _hyena_taps(n, f_w1, f_b1, f_freq, f_w2, f_b2, f_w3):
    i = np.arange(2 * n)
    t = (np.where(i < n, i, 2 * n - i).astype(np.float32) / np.float32(n)).astype(np.float32)
    ang = (np.float32(2.0 * math.pi) * t[:, None]) * np.arange(1, HYENA_BANDS + 1, dtype=np.float32)
    feats = np.zeros((2 * n, FEAT_PAD), np.float32)
    feats[:, 0] = t
    feats[:, 1:1 + HYENA_BANDS] = np.cos(ang.astype(np.float64))
    feats[:, 1 + HYENA_BANDS:1 + 2 * HYENA_BANDS] = np.sin(ang.astype(np.float64))
    log_target = abs(math.log(HYENA_DECAY_TARGET))
    deltas = jnp.linspace(log_target / HYENA_SLOW_DECAY, log_target / HYENA_FAST_DECAY, D_HYENA, dtype=F32)
    hid = f_w2.shape[0]
    w1 = jnp.concatenate([f_w1, jnp.zeros((FEAT_PAD - f_w1.shape[0], hid), F32)], axis=0)
    rows = min(TAPS_ROWS, 2 * n)
    full = lambda shape: pl.BlockSpec(shape, lambda j: (0,) * len(shape))
    return pl.pallas_call(
        functools.partial(_taps_kernel, n),
        out_shape=(jax.ShapeDtypeStruct((HYENA_ORDER, 2 * n, D_HYENA), F32),
                   jax.ShapeDtypeStruct((HYENA_ORDER, D_HYENA), F32)),
        grid=(2 * n // rows,),
        in_specs=[pl.BlockSpec((rows, FEAT_PAD), lambda j: (j, 0)), full((FEAT_PAD, hid)), full((1, hid)),
                  full((2, hid)), full((hid, hid)), full((1, hid)), full(f_w3.shape), full((1, D_HYENA))],
        out_specs=(pl.BlockSpec((HYENA_ORDER, rows, D_HYENA), lambda j: (0, j, 0)), full((HYENA_ORDER, D_HYENA))),
        compiler_params=_cparams(("arbitrary",)),
        name="hyena_taps",
    )(jnp.asarray(feats), w1, f_b1.reshape(1, hid), f_freq, f_w2, f_b2.reshape(1, hid), f_w3, deltas.reshape(1, -1))


def _hyena_latent(v, x1, x2, taps, l1, skip):
    b, n, c = v.shape
    r = FFT_R
    t2_len = n // r
    f1_full, f2 = _dft_tables(r)
    f1_half = jnp.asarray(f1_full[:, :, :t2_len]).astype(BF16)
    f1_full = jnp.asarray(f1_full).astype(BF16)
    f2 = jnp.asarray(f2).astype(BF16)
    e = jnp.asarray(_idft_table(t2_len) * (r * r)).astype(BF16)
    taps4 = taps.reshape(HYENA_ORDER, r, r, c)
    tf = _fft_stage2(_fft_stage1(taps4, f1_full), f2, scale=(1.0 / l1)[:, None, :])
    z = v.reshape(b, t2_len, r, c)
    for o, gate in enumerate((x1, x2)):
        g = _fft_stage1(z, f1_half)
        h = _fft_stage2(g, f2, tf[o:o + 1])
        z = _ifft_gate(h, e, z, gate.reshape(b, t2_len, r, c), skip[o])
    return z.reshape(b, n, c)


def _hyena_ctx_kernel(n, v_ref, x1_ref, x2_ref, k_ref, skip_ref, o_ref, z_sc):
    z_sc[...] = v_ref[0]
    for o, gate_ref in enumerate((x1_ref, x2_ref)):
        def body(s, acc):
            return acc + k_ref[o, pl.ds(n - 1 - s, n), :] * z_sc[pl.ds(s, 1), :]
        conv = lax.fori_loop(0, n, body, jnp.zeros(z_sc.shape, F32))
        z_sc[...] = gate_ref[0] * (conv + skip_ref[o:o + 1, :] * z_sc[...])
    o_ref[0] = z_sc[...]


def _hyena_context(v, x1, x2, taps, l1, skip):
    b, n, c = v.shape
    cb = 128
    k2 = jnp.roll(taps, n - 1, axis=1) / l1[:, None, :]
    tok = pl.BlockSpec((1, n, cb), lambda bi, j: (bi, 0, j))
    return pl.pallas_call(
        functools.partial(_hyena_ctx_kernel, n),
        out_shape=jax.ShapeDtypeStruct((b, n, c), F32),
        grid=(b, c // cb),
        in_specs=[tok, tok, tok,
                  pl.BlockSpec((HYENA_ORDER, 2 * n, cb), lambda bi, j: (0, 0, j)),
                  pl.BlockSpec((HYENA_ORDER, cb), lambda bi, j: (0, j))],
        out_specs=tok,
        scratch_shapes=[pltpu.VMEM((n, cb), F32)],
        compiler_params=_cparams(("parallel", "parallel")),
        name="hyena_ctx",
    )(v, x1, x2, k2, skip)


def _router_kernel(x_ref, mod_ref, rw_ref, rb_ref, h_ref, aff_ref):
    y = _ln_rows(x_ref[0])
    h = y * (1.0 + mod_ref[0, 0, 4:5, :]) + mod_ref[0, 0, 3:4, :]
    for j in range(ROW_SUB):
        h_ref[0, pl.ds(j, TOKEN_TILE, stride=ROW_SUB), :] = h[:, j * LANES:(j + 1) * LANES]
    logits = lax.dot_general(rw_ref[...], h, (((1,), (1,)), ((), ())), precision=HI,
                             preferred_element_type=F32) + rb_ref[...]
    z = jnp.exp(logits - jnp.max(logits, axis=0, keepdims=True))
    aff_ref[0] = z / jnp.sum(z, axis=0, keepdims=True)


def _router(x, mod, router_w, router_b):
    b, na, d = x.shape
    e = router_w.shape[1]
    return pl.pallas_call(
        _router_kernel,
        out_shape=(jax.ShapeDtypeStruct((b, na * ROW_SUB, LANES), F32), jax.ShapeDtypeStruct((b, e, na), F32)),
        grid=(b, na // TOKEN_TILE),
        in_specs=[pl.BlockSpec((1, TOKEN_TILE, d), lambda bi, i: (bi, i, 0)),
                  _mod_spec(),
                  pl.BlockSpec((e, d), lambda bi, i: (0, 0)),
                  pl.BlockSpec((e, 1), lambda bi, i: (0, 0))],
        out_specs=(pl.BlockSpec((1, TOKEN_TILE * ROW_SUB, LANES), lambda bi, i: (bi, i, 0)),
                   pl.BlockSpec((1, e, TOKEN_TILE), lambda bi, i: (bi, 0, i))),
        compiler_params=_cparams(("parallel", "parallel")),
        name="router",
    )(x, mod, router_w.T, router_b.reshape(e, 1))


def _prefix_count(x):
    n = x.shape[1]
    lane = lax.broadcasted_iota(jnp.int32, x.shape, 1)
    sh = 1
    while sh < n:
        x = x + jnp.where(lane >= sh, pltpu.roll(x, sh, axis=1), 0)
        sh *= 2
    return x


SELECT_BISECTIONS = 160


def _select_kernel(segments, aff_ref, sel_ref, pos_ref):
    nb, ne, _ = aff_ref.shape
    parts = [(b, s0, s1, cap) for b in range(nb) for (s0, s1, cap) in segments]

    def body(_, bounds):
        out = []
        for (b, s0, s1, cap), (lo, hi) in zip(parts, bounds):
            mid = 0.5 * (lo + hi)
            cnt = jnp.sum((aff_ref[b, :, s0:s1] >= mid).astype(jnp.int32), axis=1, keepdims=True)
            ok = cnt >= cap
            out.append((jnp.where(ok, mid, lo), jnp.where(ok, hi, mid)))
        return tuple(out)

    init = tuple((jnp.zeros((ne, 1), F32), jnp.full((ne, 1), 2.0, F32)) for _ in parts)
    bounds = lax.fori_loop(0, SELECT_BISECTIONS, body, init)
    for (b, s0, s1, cap), (lo, hi) in zip(parts, bounds):
        a = aff_ref[b, :, s0:s1]
        above = (a >= hi).astype(jnp.int32)
        tied = jnp.where(a >= lo, 1, 0) - above
        need = cap - jnp.sum(above, axis=1, keepdims=True)
        tie_rank = _prefix_count(tied) - tied
        sel = above + tied * (tie_rank < need).astype(jnp.int32)
        sel_ref[b, :, s0:s1] = sel
        pos_ref[b, :, s0:s1] = _prefix_count(sel) - sel


def _select(aff, segments):
    b, e, na = aff.shape
    blk = pl.BlockSpec((b, e, na), lambda i: (0, 0, 0))
    return pl.pallas_call(
        functools.partial(_select_kernel, segments),
        out_shape=(jax.ShapeDtypeStruct((b, e, na), jnp.int32), jax.ShapeDtypeStruct((b, e, na), jnp.int32)),
        grid=(1,),
        in_specs=[blk],
        out_specs=(blk, blk),
        compiler_params=_cparams(("arbitrary",)),
        name="expert_select",
    )(aff)


MOE_ROW_TILE = 264


def _row_tile(r):
    return next(t for t in range(MOE_ROW_TILE, 7, -8) if r % t == 0)


DMA_UNROLL = 8


def _for_each_row(tr, fn):
    def body(i, c):
        fn(i)
        return c
    lax.fori_loop(0, tr, body, 0, unroll=DMA_UNROLL)


def _gather_kernel(tr, idx_ref, h_hbm, o_ref, buf, sem):
    step = pl.program_id(0) * pl.num_programs(1) + pl.program_id(1)
    n_steps = pl.num_programs(0) * pl.num_programs(1)
    cur = step % 2

    def start_tile(t, slot):
        _for_each_row(tr, lambda i: pltpu.make_async_copy(
            h_hbm.at[pl.ds(pl.multiple_of(idx_ref[t * tr + i] * ROW_SUB, ROW_SUB), ROW_SUB)],
            buf.at[slot, pl.ds(i * ROW_SUB, ROW_SUB)], sem.at[slot]).start())

    @pl.when(step == 0)
    def _():
        start_tile(0, 0)

    @pl.when(step + 1 < n_steps)
    def _():
        start_tile(step + 1, 1 - cur)

    pltpu.make_async_copy(h_hbm.at[pl.ds(0, tr * ROW_SUB)], buf.at[cur], sem.at[cur]).wait()
    rows = buf.at[cur]
    for j in range(ROW_SUB):
        o_ref[0, :, j * LANES:(j + 1) * LANES] = rows[pl.ds(j, tr, stride=ROW_SUB), :].astype(BF16)


def _gather_rows(h2, idx):
    e, r = idx.shape
    tr = _row_tile(r)
    return pl.pallas_call(
        functools.partial(_gather_kernel, tr),
        out_shape=jax.ShapeDtypeStruct((e, r, ROW_SUB * LANES), BF16),
        grid_spec=pltpu.PrefetchScalarGridSpec(
            num_scalar_prefetch=1, grid=(e, r // tr),
            in_specs=[pl.BlockSpec(memory_space=pl.ANY)],
            out_specs=pl.BlockSpec((1, tr, ROW_SUB * LANES), lambda ei, j, idx_ref: (ei, j, 0)),
            scratch_shapes=[pltpu.VMEM((2, tr * ROW_SUB, LANES), F32), pltpu.SemaphoreType.DMA((2,))]),
        compiler_params=_cparams(("arbitrary", "arbitrary")),
        name="moe_gather",
    )(idx.reshape(-1), h2)


MOE_TF = 256
MOE_ROW_CHUNKS = 4


COMBINE_CHUNK = 64


def _expert_ffn_kernel(xs_ref, wg_ref, wu_ref, wd_ref, tv_ref, o_ref, acc_ref):
    j = pl.program_id(1)
    wg = wg_ref[0, 0].astype(BF16)
    wu = wu_ref[0, 0].astype(BF16)
    wd = wd_ref[0, 0].astype(BF16)
    rows = xs_ref.shape[1]
    rc = rows // MOE_ROW_CHUNKS

    @pl.when(j == 0)
    def _():
        acc_ref[...] = jnp.zeros_like(acc_ref)

    for ci in range(MOE_ROW_CHUNKS):
        sl = slice(ci * rc, (ci + 1) * rc)
        x = xs_ref[0, sl, :]
        g = jnp.dot(x, wg, preferred_element_type=F32)
        u = jnp.dot(x, wu, preferred_element_type=F32)
        a = (g * jax.nn.sigmoid(g) * u).astype(BF16)
        acc_ref[sl, :] += jnp.dot(a, wd, preferred_element_type=F32)

    @pl.when(j == pl.num_programs(1) - 1)
    def _():
        y = acc_ref[...] * tv_ref[0]
        for t in range(ROW_SUB):
            o_ref[0, pl.ds(t, rows, stride=ROW_SUB), :] = y[:, t * LANES:(t + 1) * LANES]
        o_ref[0, rows * ROW_SUB:, :] = jnp.zeros((o_ref.shape[1] - rows * ROW_SUB, LANES), F32)


def _expert_ffn(xs, w_gate, w_up, w_down, tv, layer):
    e, r, d = xs.shape
    f = w_gate.shape[3]
    tf = MOE_TF
    return pl.pallas_call(
        _expert_ffn_kernel,
        out_shape=jax.ShapeDtypeStruct((e, (r + COMBINE_CHUNK) * ROW_SUB, LANES), F32),
        grid=(e, f // tf),
        in_specs=[pl.BlockSpec((1, r, d), lambda ei, j: (ei, 0, 0)),
                  pl.BlockSpec((1, 1, d, tf), lambda ei, j: (layer, ei, 0, j)),
                  pl.BlockSpec((1, 1, d, tf), lambda ei, j: (layer, ei, 0, j)),
                  pl.BlockSpec((1, 1, tf, d), lambda ei, j: (layer, ei, j, 0)),
                  pl.BlockSpec((1, r, 1), lambda ei, j: (ei, 0, 0))],
        out_specs=pl.BlockSpec((1, (r + COMBINE_CHUNK) * ROW_SUB, LANES), lambda ei, j: (ei, 0, 0)),
        scratch_shapes=[pltpu.VMEM((r, d), F32)],
        compiler_params=_cparams(("parallel", "arbitrary"), VMEM_LIMIT),
        name="expert_ffn",
    )(xs, w_gate, w_up, w_down, tv)


COMBINE_UNROLL = 4
COMBINE_PAD = 8


def _combine_kernel(alpha, n_exp, r, n_tiles, idx_ref, st_ref, y_hbm, x_ref, mod_ref, lw_ref, lb_ref, o_ref,
                    acc, head, tail, head_sem, tail_sem):
    g = pl.program_id(0) * pl.num_programs(1) + pl.program_id(1)
    tile_base = g * TOKEN_TILE
    ch = COMBINE_CHUNK
    cur = g % 2

    def run(e, tile):
        s0 = st_ref[e * (n_tiles + 1) + tile]
        return s0, st_ref[e * (n_tiles + 1) + tile + 1] - s0

    def rows_of(first, count):
        start = first * ROW_SUB
        return pl.ds(start if isinstance(start, int) else pl.multiple_of(start, ROW_SUB), count * ROW_SUB)

    def head_copy(e, tile, slot):
        s0, cnt = run(e, tile)
        return pltpu.make_async_copy(y_hbm.at[e, rows_of(s0, ch)], head.at[slot, e, rows_of(0, ch)],
                                     head_sem.at[slot, e]), cnt

    def start_heads(tile, slot):
        for e in range(n_exp):
            copy, cnt = head_copy(e, tile, slot)
            pl.when(cnt > 0)(copy.start)

    @pl.when(g == 0)
    def _():
        head[...] = jnp.zeros_like(head)
        tail[...] = jnp.zeros_like(tail)
        start_heads(0, 0)

    @pl.when(g + 1 < n_tiles)
    def _():
        start_heads(g + 1, 1 - cur)

    acc[...] = jnp.zeros_like(acc)

    def add_rows(src, first, count):
        def body(it, carry):
            for u in range(COMBINE_UNROLL):
                j = it * COMBINE_UNROLL + u
                t = jnp.where(j < count, idx_ref[first + j] - tile_base, TOKEN_TILE)
                acc[rows_of(t, 1), :] = acc[rows_of(t, 1), :] + src[rows_of(j, 1), :]
            return carry
        lax.fori_loop(0, (count + COMBINE_UNROLL - 1) // COMBINE_UNROLL, body, 0)

    for e in range(n_exp):
        s0, cnt = run(e, g)
        copy, _ = head_copy(e, g, cur)
        pl.when(cnt > 0)(copy.wait)
        add_rows(head.at[cur, e], e * r + s0, jnp.minimum(cnt, ch))
        for c in range(1, TOKEN_TILE // ch):
            @pl.when(cnt > c * ch)
            def _(e=e, c=c, s0=s0, cnt=cnt):
                more = pltpu.make_async_copy(y_hbm.at[e, rows_of(s0 + c * ch, ch)], tail.at[rows_of(0, ch)], tail_sem)
                more.start()
                more.wait()
                add_rows(tail, e * r + s0 + c * ch, jnp.minimum(cnt - c * ch, ch))

    f = jnp.concatenate([acc[pl.ds(j, TOKEN_TILE, stride=ROW_SUB), :] for j in range(ROW_SUB)], axis=1)
    res = alpha * x_ref[0] + mod_ref[0, 0, 5:6, :] * f
    o_ref[0] = _ln_rows(res) * lw_ref[...] + lb_ref[...]


def _combine_residual_ln(y, idx, starts, x, mod, ln_w, ln_b, alpha):
    b, na, d = x.shape
    e, r = idx.shape
    n_tiles = b * na // TOKEN_TILE
    tile = pl.BlockSpec((1, TOKEN_TILE, d), lambda bi, i, *_: (bi, i, 0))
    vec = pl.BlockSpec((1, d), lambda bi, i, *_: (0, 0))
    chunk_rows = (COMBINE_CHUNK + COMBINE_PAD) * ROW_SUB
    return pl.pallas_call(
        functools.partial(_combine_kernel, alpha, e, r, n_tiles),
        out_shape=jax.ShapeDtypeStruct((b, na, d), F32),
        grid_spec=pltpu.PrefetchScalarGridSpec(
            num_scalar_prefetch=2, grid=(b, na // TOKEN_TILE),
            in_specs=[pl.BlockSpec(memory_space=pl.ANY), tile,
                      pl.BlockSpec((1, 1, N_MOD, d), lambda bi, i, *_: (bi, jnp.minimum(i, 1), 0, 0)), vec, vec],
            out_specs=tile,
            scratch_shapes=[pltpu.VMEM(((TOKEN_TILE + COMBINE_PAD) * ROW_SUB, LANES), F32),
                            pltpu.VMEM((2, e, chunk_rows, LANES), F32),
                            pltpu.VMEM((chunk_rows, LANES), F32),
                            pltpu.SemaphoreType.DMA((2, e)), pltpu.SemaphoreType.DMA(())]),
        compiler_params=_cparams(("arbitrary", "arbitrary"), VMEM_LIMIT),
        name="moe_combine_ln",
    )(jnp.pad(idx.reshape(-1), (0, COMBINE_PAD)), starts.reshape(-1), y, x, mod, ln_w.reshape(1, d), ln_b.reshape(1, d))


def _moe_residual_ln(x, mod, router_w, router_b, w_gate, w_up, w_down, layer, ln_w, ln_b, alpha):
    b, na, d = x.shape
    n_lat = na - CTX_LEN
    cap_c = EC_CAPACITY_FACTOR * CTX_LEN // N_EXPERTS
    cap_l = EC_CAPACITY_FACTOR * n_lat // N_EXPERTS
    h, aff = _router(x, mod, router_w, router_b)
    sel, _ = _select(aff, ((0, CTX_LEN, cap_c), (CTX_LEN, na, cap_l)))
    idx_c = jnp.argsort(1 - sel[:, :, :CTX_LEN], axis=-1, stable=True)[..., :cap_c]
    idx_l = jnp.argsort(1 - sel[:, :, CTX_LEN:], axis=-1, stable=True)[..., :cap_l] + CTX_LEN
    idx = jnp.concatenate([idx_c, idx_l], axis=-1).astype(jnp.int32)
    tv = jnp.take_along_axis(aff, idx, axis=-1)
    flat = idx + (jnp.arange(b, dtype=jnp.int32) * na)[:, None, None]
    flat = jnp.transpose(flat, (1, 0, 2)).reshape(N_EXPERTS, -1)
    tv = jnp.transpose(tv, (1, 0, 2)).reshape(N_EXPERTS, -1, 1)
    per_tile = jnp.sum(sel.reshape(b, N_EXPERTS, na // TOKEN_TILE, TOKEN_TILE), axis=-1)
    per_tile = jnp.transpose(per_tile, (1, 0, 2)).reshape(N_EXPERTS, -1)
    starts = jnp.concatenate([jnp.zeros((N_EXPERTS, 1), jnp.int32), jnp.cumsum(per_tile, axis=1, dtype=jnp.int32)], axis=1)
    xs = _gather_rows(h.reshape(b * na * ROW_SUB, LANES), flat)
    y = _expert_ffn(xs, w_gate, w_up, w_down, tv, layer)
    return _combine_residual_ln(y, flat, starts, x, mod, ln_w, ln_b, alpha)


def _dwconv_kernel(silu, x_ref, w_ref, b_ref, o_ref):
    x = x_ref[0]
    na = x.shape[0]
    row = lax.broadcasted_iota(jnp.int32, x.shape, 0)
    first = (row == 0) | (row == CTX_LEN)
    last = (row == CTX_LEN - 1) | (row == na - 1)
    prev = jnp.where(first, 0.0, pltpu.roll(x, 1, axis=0))
    nxt = jnp.where(last, 0.0, pltpu.roll(x, na - 1, axis=0))
    y = w_ref[0:1, :] * prev + w_ref[1:2, :] * x + w_ref[2:3, :] * nxt + b_ref[...]
    if silu:
        y = y * jax.nn.sigmoid(y)
    o_ref[0, 0] = y


def _segment_dwconv(p, w, bias, groups, silu):
    b, na, c = p.shape
    per = c // groups // LANES
    return pl.pallas_call(
        functools.partial(_dwconv_kernel, silu),
        out_shape=jax.ShapeDtypeStruct((groups, b, na, c // groups), F32),
        grid=(b, c // LANES),
        in_specs=[pl.BlockSpec((1, na, LANES), lambda bi, j: (bi, 0, j)),
                  pl.BlockSpec((SHORT_CONV, LANES), lambda bi, j: (0, j)),
                  pl.BlockSpec((1, LANES), lambda bi, j: (0, j))],
        out_specs=pl.BlockSpec((1, 1, na, LANES), lambda bi, j: (j // per, bi, 0, j % per)),
        compiler_params=_cparams(("parallel", "parallel"), VMEM_LIMIT),
        name="short_conv",
    )(p, w, bias.reshape(1, c))


def _rope_tables(n_lat):
    rows = n_lat // GRID_W
    row = jnp.repeat(jnp.arange(rows, dtype=F32), GRID_W)
    col = (jnp.arange(n_lat) % GRID_W).astype(F32)
    nf = ATTN_HEAD_DIM // 4
    inv = ROPE_THETA ** (-jnp.arange(nf, dtype=F32) / nf)
    ar = row[:, None] * inv
    ac = col[:, None] * inv
    ang = jnp.concatenate([ar, ar, ac, ac], axis=-1)
    return jnp.cos(ang), jnp.sin(ang)


def _mixing(x, mod, w_in, mlstm_conv_w, mlstm_conv_b, mlstm_gate_b, mlstm_norm_w, attn_q_norm_w,
            attn_k_norm_w, hyena_conv_w, hyena_conv_b, hyena_filter, hyena_skip, with_ctx_out):
    b, na, _ = x.shape
    n_lat = na - CTX_LEN
    mqk, mv, mo, aq, ak, av, hy, mg = _in_projection(x, mod, w_in)
    mg = mg[..., :MLSTM_GATES]

    qk = _segment_dwconv(mqk, _pad_heads(mlstm_conv_w), _pad_heads(mlstm_conv_b), groups=1,
                         silu=True)[0]
    g = mg.reshape(b, na, N_DIR, 2, MLSTM_HEADS) + mlstm_gate_b
    g = jnp.stack([g[:, :, :, 0], jax.nn.log_sigmoid(g[:, :, :, 1])], axis=3)
    gcols = jnp.transpose(g, (2, 0, 1, 3, 4)).reshape(N_DIR, b, na, 2 * MLSTM_HEADS)
    grows = jnp.transpose(gcols, (0, 1, 3, 2))
    hm = _mlstm_scan(qk, mv, grows, gcols)

    q, k, v = _qkv_prep(aq, ak, av, attn_q_norm_w, attn_k_norm_w)
    ya = _attention(q, k, v)

    hv, hx1, hx2 = _segment_dwconv(hy, hyena_conv_w, hyena_conv_b, groups=3, silu=False)
    yh = _hyena_latent(hv[:, CTX_LEN:], hx1[:, CTX_LEN:], hx2[:, CTX_LEN:],
                       *_hyena_taps(n_lat, *hyena_filter), hyena_skip)
    if with_ctx_out:
        yh_c = _hyena_context(hv[:, :CTX_LEN], hx1[:, :CTX_LEN], hx2[:, :CTX_LEN],
                              *_hyena_taps(CTX_LEN, *hyena_filter), hyena_skip)
    else:
        yh_c = jnp.zeros((b, CTX_LEN, D_HYENA), F32)
    yh = jnp.concatenate([yh_c, yh], axis=1)
    return hm, mo, ya, yh


def kernel(x, c, ctx, c_ctx, w_mod, b_mod, w_in, mlstm_conv_w, mlstm_conv_b, mlstm_gate_b, mlstm_norm_w, attn_q_norm_w, attn_k_norm_w, hyena_conv_w, hyena_conv_b, hyena_f_w1, hyena_f_b1, hyena_f_freq, hyena_f_w2, hyena_f_b2, hyena_f_w3, hyena_skip, w_out, ln_mix_w, ln_mix_b, router_w, router_b, exp_w_gate, exp_w_up, exp_w_down, ln_ffn_w, ln_ffn_b):
    bsz, seq, d = x.shape
    assert d == D_MODEL and ctx.shape[1] == CTX_LEN == TOKEN_TILE and seq == FFT_R * FFT_R // 2
    alpha = (2.0 * DEPTH) ** 0.25
    xa = jnp.concatenate([ctx, x], axis=1)
    crows = jnp.concatenate([c, jnp.broadcast_to(c_ctx, (8 - bsz, d))], axis=0)
    for l in range(DEPTH):
        last = l == DEPTH - 1
        m = _modulation(crows, w_mod, b_mod[l], l).reshape(8, N_MOD, d)
        mod = jnp.stack([jnp.broadcast_to(m[bsz], (bsz, N_MOD, d)), m[:bsz]], axis=1)
        hm, mo, ya, yh = _mixing(
            xa, mod, w_in[l], mlstm_conv_w[l], mlstm_conv_b[l], mlstm_gate_b[l], mlstm_norm_w[l],
            attn_q_norm_w[l], attn_k_norm_w[l], hyena_conv_w[l], hyena_conv_b[l],
            (hyena_f_w1[l], hyena_f_b1[l], hyena_f_freq[l], hyena_f_w2[l], hyena_f_b2[l], hyena_f_w3[l]),
            hyena_skip[l], not last)
        xa = _out_projection(hm, mo, mlstm_norm_w[l], ya, yh, xa, mod, w_out[l].astype(BF16),
                             ln_mix_w[l], ln_mix_b[l], alpha)
        xa = _moe_residual_ln(xa, mod, router_w[l], router_b[l], exp_w_gate, exp_w_up, exp_w_down, l,
                              ln_ffn_w[l], ln_ffn_b[l], alpha)
    return xa[:, CTX_LEN:]
```

```python
import functools
import math

import numpy as np
import jax
import jax.numpy as jnp
from jax import lax
from jax.experimental import pallas as pl
from jax.experimental.pallas import tpu as pltpu

F32 = jnp.float32
BF16 = jnp.bfloat16
HI = lax.Precision.HIGHEST

D_MODEL = 1024
DEPTH = 2
GRID_W = 64
CTX_LEN = 256
N_DIR = 2
SHORT_CONV = 3

D_MLSTM = 256
MLSTM_HEAD_DIM = 64
MLSTM_HEADS = 4
MLSTM_GATES = 16
MLSTM_CHUNK = 128

D_ATTN = 512
ATTN_HEAD_DIM = 64
ATTN_HEADS = 8
ATTN_KV_HEADS = 2
ATTN_GROUP = 4
D_KV = 128
ROPE_THETA = 10000.0

D_HYENA = 256
HYENA_ORDER = 2
HYENA_BANDS = 16
HYENA_FAST_DECAY = 0.3
HYENA_SLOW_DECAY = 1.5
HYENA_DECAY_TARGET = 1e-2
HYENA_WINDOW_SHIFT = 0.05

N_IN = 4 * D_MLSTM + MLSTM_GATES + D_ATTN + 2 * D_KV + 3 * D_HYENA
N_EXPERTS = 16
EC_CAPACITY_FACTOR = 2
D_FF_EXPERT = 2816
N_MOD = 6
LN_EPS = 1e-5
RMS_EPS = 1e-6

LANES = 128
ROW_SUB = D_MODEL // LANES
TOKEN_TILE = 256
FFT_R = 128
FFT_GROUP = 8
VMEM_LIMIT = 56 * 1024 * 1024


def _cparams(sem, vmem=None):
    return pltpu.CompilerParams(dimension_semantics=sem, vmem_limit_bytes=vmem)


def _mod_kernel(c_ref, w_ref, b_ref, o_ref):
    cs = c_ref[...]
    cs = cs * jax.nn.sigmoid(cs)
    o_ref[...] = jnp.dot(cs, w_ref[0], precision=HI, preferred_element_type=F32) + b_ref[...]


def _modulation(crows, w_mod, b_mod, layer):
    rows, d = crows.shape
    n = w_mod.shape[2]
    tn = 1024
    return pl.pallas_call(
        _mod_kernel,
        out_shape=jax.ShapeDtypeStruct((rows, n), F32),
        grid=(n // tn,),
        in_specs=[pl.BlockSpec((rows, d), lambda j: (0, 0)),
                  pl.BlockSpec((1, d, tn), lambda j: (layer, 0, j)),
                  pl.BlockSpec((1, tn), lambda j: (0, j))],
        out_specs=pl.BlockSpec((rows, tn), lambda j: (0, j)),
        compiler_params=_cparams(("parallel",)),
        name="adaln_mod",
    )(crows, w_mod, b_mod.reshape(1, n))


def _ln_rows(x):
    mu = jnp.mean(x, axis=-1, keepdims=True)
    xc = x - mu
    var = jnp.mean(xc * xc, axis=-1, keepdims=True)
    return xc * lax.rsqrt(var + LN_EPS)


def _mod_spec():
    return pl.BlockSpec((1, 1, N_MOD, D_MODEL), lambda b, i: (b, jnp.minimum(i, 1), 0, 0))


HEAD_DIM = 64
_C_MG = 4 * D_MLSTM
_C_AQ = _C_MG + MLSTM_GATES
IN_GROUPS = (("mqk", 0, 2 * D_MLSTM, True), ("mv", 2 * D_MLSTM, D_MLSTM, True), ("mo", 3 * D_MLSTM, D_MLSTM, False),
             ("aq", _C_AQ, D_ATTN, True), ("ak", _C_AQ + D_ATTN, D_KV, True), ("av", _C_AQ + D_ATTN + D_KV, D_KV, True),
             ("hy", _C_AQ + D_ATTN + 2 * D_KV, 3 * D_HYENA, False), ("mg", _C_MG, MLSTM_GATES, False))


def _group_width(src_width, head_padded):
    return src_width // HEAD_DIM * LANES if head_padded else -(-src_width // LANES) * LANES


def _pad_heads(a, axis=-1):
    axis = axis % a.ndim
    nh = a.shape[axis] // HEAD_DIM
    a = a.reshape(a.shape[:axis] + (nh, HEAD_DIM) + a.shape[axis + 1:])
    pad = [(0, 0)] * a.ndim
    pad[axis + 1] = (0, LANES - HEAD_DIM)
    a = jnp.pad(a, pad)
    return a.reshape(a.shape[:axis] + (nh * LANES,) + a.shape[axis + 2:])


def _inproj_kernel(x_ref, mod_ref, w_ref, *o_refs):
    y = _ln_rows(x_ref[0])
    h = (y * (1.0 + mod_ref[0, 0, 1:2, :]) + mod_ref[0, 0, 0:1, :]).astype(BF16)
    off = 0
    for o_ref in o_refs:
        width = o_ref.shape[2]
        o_ref[0] = jnp.dot(h, w_ref[:, off:off + width], preferred_element_type=F32)
        off += width


def _in_projection(x, mod, w_in):
    b, na, d = x.shape
    cols = []
    for _, start, src, head_padded in IN_GROUPS:
        wg = w_in[:, start:start + src]
        if head_padded:
            wg = _pad_heads(wg)
        elif src % LANES:
            wg = jnp.pad(wg, ((0, 0), (0, LANES - src % LANES)))
        cols.append(wg)
    w = jnp.concatenate(cols, axis=1).astype(BF16)
    n = w.shape[1]
    widths = [_group_width(src, hp) for _, _, src, hp in IN_GROUPS]
    return pl.pallas_call(
        _inproj_kernel,
        out_shape=[jax.ShapeDtypeStruct((b, na, width), F32) for width in widths],
        grid=(b, na // TOKEN_TILE),
        in_specs=[pl.BlockSpec((1, TOKEN_TILE, d), lambda bi, i: (bi, i, 0)),
                  _mod_spec(),
                  pl.BlockSpec((d, n), lambda bi, i: (0, 0))],
        out_specs=[pl.BlockSpec((1, TOKEN_TILE, width), lambda bi, i: (bi, i, 0)) for width in widths],
        compiler_params=_cparams(("parallel", "parallel"), VMEM_LIMIT),
        name="in_proj",
    )(x, mod, w)


def _compact_heads(x):
    tiles = [x[:, j * LANES:(j + 1) * LANES] for j in range(x.shape[1] // LANES)]
    return jnp.concatenate([tiles[j] + pltpu.roll(tiles[j + 1], HEAD_DIM, axis=1) for j in range(0, len(tiles), 2)],
                           axis=1)


def _outproj_kernel(alpha, hm0_ref, hm1_ref, mo_ref, nw_ref, seg_ref, ya_ref, yh_ref, x_ref, mod_ref, w_ref,
                    lw_ref, lb_ref, o_ref):
    hm = _compact_heads(hm0_ref[0, 0] + hm1_ref[0, 0])
    ss = jnp.dot(hm * hm, seg_ref[...], precision=HI, preferred_element_type=F32)
    hn = hm * lax.rsqrt(ss * (1.0 / MLSTM_HEAD_DIM) + RMS_EPS) * nw_ref[...]
    ym = jax.nn.sigmoid(mo_ref[0]) * hn
    mix = jnp.dot(ym.astype(BF16), w_ref[0:D_MLSTM, :], preferred_element_type=F32)
    mix = mix + jnp.dot(_compact_heads(ya_ref[0]).astype(BF16), w_ref[D_MLSTM:D_MLSTM + D_ATTN, :],
                        preferred_element_type=F32)
    mix = mix + jnp.dot(yh_ref[0].astype(BF16), w_ref[D_MLSTM + D_ATTN:, :], preferred_element_type=F32)
    r = alpha * x_ref[0] + mod_ref[0, 0, 2:3, :] * mix
    o_ref[0] = _ln_rows(r) * lw_ref[...] + lb_ref[...]


def _out_projection(hm, mo, norm_w, ya, yh, x, mod, w_bf16, ln_w, ln_b, alpha):
    b, na, d = x.shape
    dm = mo.shape[2]
    head = np.arange(dm) // MLSTM_HEAD_DIM
    seg = jnp.asarray((head[:, None] == head[None, :]).astype(np.float32))

    def tile(width):
        return pl.BlockSpec((1, TOKEN_TILE, width), lambda bi, i: (bi, i, 0))

    def hm_spec(direction):
        return pl.BlockSpec((1, 1, TOKEN_TILE, hm.shape[3]), lambda bi, i: (direction, bi, i, 0))

    vec = pl.BlockSpec((1, d), lambda bi, i: (0, 0))
    return pl.pallas_call(
        functools.partial(_outproj_kernel, alpha),
        out_shape=jax.ShapeDtypeStruct((b, na, d), F32),
        grid=(b, na // TOKEN_TILE),
        in_specs=[hm_spec(0), hm_spec(1), tile(dm), pl.BlockSpec((1, dm), lambda bi, i: (0, 0)),
                  pl.BlockSpec((dm, dm), lambda bi, i: (0, 0)), tile(ya.shape[2]), tile(yh.shape[2]), tile(d),
                  _mod_spec(), pl.BlockSpec((d, d), lambda bi, i: (0, 0)), vec, vec],
        out_specs=tile(d),
        compiler_params=_cparams(("parallel", "parallel"), VMEM_LIMIT),
        name="out_proj_ln",
    )(hm, hm, mo, norm_w.reshape(1, dm), seg, ya, yh, x, mod, w_bf16, ln_w.reshape(1, d), ln_b.reshape(1, d))


def _mlstm_kernel(qk_ref, v_ref, gr_ref, gc_ref, o_ref, c_sc, n_sc, m_sc):
    t = MLSTM_CHUNK
    nh = MLSTM_HEADS
    d = pl.program_id(0)

    @pl.when(pl.program_id(2) == 0)
    def _():
        c_sc[...] = jnp.zeros_like(c_sc)
        n_sc[...] = jnp.zeros_like(n_sc)
        m_sc[...] = jnp.zeros_like(m_sc)

    row = lax.broadcasted_iota(jnp.int32, (t, t), 0)
    col = lax.broadcasted_iota(jnp.int32, (t, t), 1)
    mask = jnp.where(d == 0, col - row, row - col) <= 0
    maskb = mask.astype(BF16)
    grows = gr_ref[0, 0]
    gcols = gc_ref[0, 0]

    def split3(a):
        hi = a.astype(BF16)
        rest = a - hi.astype(F32)
        mid = rest.astype(BF16)
        return hi, mid, (rest - mid.astype(F32)).astype(BF16)

    def dot01(parts, ones, dims=(((1,), (0,)), ((), ()))):
        return sum(lax.dot_general(p, ones, dims, preferred_element_type=F32) for p in parts)

    def ones_dot(ones, parts):
        return sum(jnp.dot(ones, p, preferred_element_type=F32) for p in parts)

    gcol_parts = split3(gcols)
    cum_cols = ones_dot(maskb, gcol_parts)
    cum_rows = dot01(split3(grows), maskb, (((1,), (1,)), ((), ())))
    spread = (lax.broadcasted_iota(jnp.int32, (2 * nh, 2 * nh * LANES), 1) // LANES
              == lax.broadcasted_iota(jnp.int32, (2 * nh, 2 * nh * LANES), 0)).astype(BF16)
    g_lanes = dot01(gcol_parts, spread)
    cum_lanes = dot01(split3(cum_cols), spread)

    def column(a, idx):
        return a[:, idx * LANES:(idx + 1) * LANES]

    dh = LANES
    for h in range(nh):
        qc = qk_ref[0, :, h * dh:(h + 1) * dh] * (MLSTM_HEAD_DIM ** -0.5)
        kc = qk_ref[0, :, (nh + h) * dh:(nh + h + 1) * dh]
        vc = v_ref[0, :, h * dh:(h + 1) * dh]
        ic_row = grows[h:h + 1, :]
        ic_col = column(g_lanes, h)
        bcol = column(cum_lanes, nh + h)
        brow = cum_rows[nh + h:nh + h + 1, :]
        tot = jnp.sum(grows[nh + h:nh + h + 1, :], axis=1, keepdims=True)
        m0 = m_sc[h:h + 1, 0:1]
        log_inter = bcol + m0
        log_intra = jnp.where(mask, bcol - brow + ic_row, -jnp.inf)
        mrow = jnp.maximum(log_inter, jnp.max(log_intra, axis=1, keepdims=True))
        w_inter = jnp.exp(log_inter - mrow)
        qb = qc.astype(BF16)
        vb = vc.astype(BF16)
        scores = lax.dot_general(qb, kc.astype(BF16), (((1,), (1,)), ((), ())),
                                 preferred_element_type=F32) * jnp.exp(log_intra - mrow)
        ct = c_sc[h]
        n0 = n_sc[h]
        num = (w_inter * jnp.dot(qb, ct.astype(BF16), preferred_element_type=F32)
               + jnp.dot(scores.astype(BF16), vb, preferred_element_type=F32))
        den = w_inter * jnp.sum(qc * n0, axis=1, keepdims=True) + jnp.sum(scores, axis=1, keepdims=True)
        o_ref[0, 0, :, h * dh:(h + 1) * dh] = num / jnp.maximum(jnp.abs(den), jnp.exp(-mrow))
        log_src = tot - bcol + ic_col
        m_new = jnp.maximum(tot + m0, jnp.max(log_src, axis=0, keepdims=True))
        wk = jnp.exp(log_src - m_new) * kc
        decay = jnp.exp(tot + m0 - m_new)
        c_sc[h] = decay * ct + lax.dot_general(wk.astype(BF16), vb, (((0,), (0,)), ((), ())),
                                               preferred_element_type=F32)
        n_sc[h] = decay * n0 + jnp.sum(wk, axis=0, keepdims=True)
        m_sc[h:h + 1, :] = jnp.broadcast_to(m_new, (1, 128))


def _mlstm_scan(qk, v, grows, gcols):
    b, na, dv = v.shape
    nh, dh = MLSTM_HEADS, LANES
    t = MLSTM_CHUNK
    nc = na // t
    nctx = CTX_LEN // t

    def chunk(d, c):
        rev = jnp.where(c < nctx, nctx - 1 - c, nc + nctx - 1 - c)
        return jnp.where(d == 0, c, rev)

    return pl.pallas_call(
        _mlstm_kernel,
        out_shape=jax.ShapeDtypeStruct((N_DIR, b, na, dv), F32),
        grid=(N_DIR, b, nc),
        in_specs=[pl.BlockSpec((1, t, 2 * dv), lambda d, bi, c: (bi, chunk(d, c), 0)),
                  pl.BlockSpec((1, t, dv), lambda d, bi, c: (bi, chunk(d, c), 0)),
                  pl.BlockSpec((1, 1, 2 * nh, t), lambda d, bi, c: (d, bi, 0, chunk(d, c))),
                  pl.BlockSpec((1, 1, t, 2 * nh), lambda d, bi, c: (d, bi, chunk(d, c), 0))],
        out_specs=pl.BlockSpec((1, 1, t, dv), lambda d, bi, c: (d, bi, chunk(d, c), 0)),
        scratch_shapes=[pltpu.VMEM((nh, dh, dh), F32), pltpu.VMEM((nh, 1, dh), F32), pltpu.VMEM((8, 128), F32)],
        compiler_params=_cparams(("parallel", "parallel", "arbitrary")),
        name="mlstm_scan",
    )(qk, v, grows, gcols)


ATTN_TQ = 256
LOG2E = 1.4426950408889634


def _qkv_prep_kernel(aq_ref, ak_ref, av_ref, cos_ref, sin_ref, qw_ref, kw_ref, q_ref, k_ref, v_ref):
    lane = lax.broadcasted_iota(jnp.int32, (aq_ref.shape[1], LANES), 1)
    head_lane = lane < ATTN_HEAD_DIM
    first_half = (lane % (ATTN_HEAD_DIM // 2)) < (ATTN_HEAD_DIM // 4)
    cos = cos_ref[...]
    sin = sin_ref[...]
    quarter = ATTN_HEAD_DIM // 4

    def norm_rope(x, w):
        y = x * lax.rsqrt(jnp.sum(x * x, axis=1, keepdims=True) * (1.0 / ATTN_HEAD_DIM) + RMS_EPS) * w
        rot = jnp.where(first_half, -pltpu.roll(y, LANES - quarter, axis=1), pltpu.roll(y, quarter, axis=1))
        return jnp.where(head_lane, y * cos + rot * sin, 0.0)

    for h in range(ATTN_HEADS):
        q_ref[0, h] = norm_rope(aq_ref[0, :, h * LANES:(h + 1) * LANES], qw_ref[...]).astype(BF16)
    for h in range(ATTN_KV_HEADS):
        k_ref[0, h] = norm_rope(ak_ref[0, :, h * LANES:(h + 1) * LANES], kw_ref[...]).astype(BF16)
        v_ref[0, h] = jnp.where(lane == ATTN_HEAD_DIM, 1.0, av_ref[0, :, h * LANES:(h + 1) * LANES]).astype(BF16)


def _qkv_prep(aq, ak, av, q_norm_w, k_norm_w):
    b, na, _ = aq.shape
    cos, sin = _rope_tables(na - CTX_LEN)
    pad = ((CTX_LEN, 0), (0, LANES - ATTN_HEAD_DIM))
    cos = jnp.pad(cos, pad, constant_values=1.0)
    sin = jnp.pad(sin, pad)
    qw = jnp.pad(q_norm_w * (ATTN_HEAD_DIM ** -0.5 * LOG2E), (0, LANES - ATTN_HEAD_DIM)).reshape(1, LANES)
    kw = jnp.pad(k_norm_w, (0, LANES - ATTN_HEAD_DIM)).reshape(1, LANES)
    tok = lambda width: pl.BlockSpec((1, TOKEN_TILE, width), lambda bi, i: (bi, i, 0))
    tab = pl.BlockSpec((TOKEN_TILE, LANES), lambda bi, i: (i, 0))
    vec = pl.BlockSpec((1, LANES), lambda bi, i: (0, 0))
    heads = lambda nh: pl.BlockSpec((1, nh, TOKEN_TILE, LANES), lambda bi, i: (bi, 0, i, 0))
    return pl.pallas_call(
        _qkv_prep_kernel,
        out_shape=(jax.ShapeDtypeStruct((b, ATTN_HEADS, na, LANES), BF16),
                   jax.ShapeDtypeStruct((b, ATTN_KV_HEADS, na, LANES), BF16),
                   jax.ShapeDtypeStruct((b, ATTN_KV_HEADS, na, LANES), BF16)),
        grid=(b, na // TOKEN_TILE),
        in_specs=[tok(aq.shape[2]), tok(ak.shape[2]), tok(av.shape[2]), tab, tab, vec, vec],
        out_specs=(heads(ATTN_HEADS), heads(ATTN_KV_HEADS), heads(ATTN_KV_HEADS)),
        compiler_params=_cparams(("parallel", "parallel")),
        name="qkv_prep",
    )(aq, ak, av, cos, sin, qw, kw)


def _attn_kernel(q_ref, k_ref, v_ref, o_ref, s_ref, p_ref):
    def attend(nk):
        s_ref[:, 0:nk] = lax.dot_general(q_ref[0, 0], k_ref[0, 0, 0:nk, :], (((1,), (1,)), ((), ())),
                                         preferred_element_type=F32)
        m = jnp.max(s_ref[:, 0:nk], axis=1, keepdims=True)
        p_ref[:, 0:nk] = jnp.exp2(s_ref[:, 0:nk] - m).astype(BF16)
        acc = jnp.dot(p_ref[:, 0:nk], v_ref[0, 0, 0:nk, :], preferred_element_type=F32)
        lane = lax.broadcasted_iota(jnp.int32, acc.shape, 1)
        o_ref[0] = jnp.where(lane < ATTN_HEAD_DIM, acc / acc[:, ATTN_HEAD_DIM:ATTN_HEAD_DIM + 1], 0.0)

    is_ctx = pl.program_id(2) < CTX_LEN // ATTN_TQ
    pl.when(is_ctx)(lambda: attend(CTX_LEN))
    pl.when(jnp.logical_not(is_ctx))(lambda: attend(k_ref.shape[2]))


def _attention(q, k, v):
    b, nh, na, _ = q.shape
    group = nh // k.shape[1]
    kv_spec = pl.BlockSpec((1, 1, na, LANES), lambda bi, h, i: (bi, h // group, 0, 0))
    return pl.pallas_call(
        _attn_kernel,
        out_shape=jax.ShapeDtypeStruct((b, na, nh * LANES), F32),
        grid=(b, nh, na // ATTN_TQ),
        in_specs=[pl.BlockSpec((1, 1, ATTN_TQ, LANES), lambda bi, h, i: (bi, h, i, 0)), kv_spec, kv_spec],
        out_specs=pl.BlockSpec((1, ATTN_TQ, LANES), lambda bi, h, i: (bi, i, h)),
        scratch_shapes=[pltpu.VMEM((ATTN_TQ, na), F32), pltpu.VMEM((ATTN_TQ, na), BF16)],
        compiler_params=_cparams(("parallel", "parallel", "parallel"), VMEM_LIMIT),
        name="attention",
    )(q, k, v)


def _dft_tables(t2_len):
    r = FFT_R
    n = r * r
    idx = np.arange(r, dtype=np.float64)
    kb = idx[None, :, None]
    t1 = idx[:, None, None]
    t2 = np.arange(t2_len, dtype=np.float64)[None, None, :]
    ang = -2.0 * np.pi * (t2 * kb / r + t1 * kb / n)
    f1 = np.concatenate([np.cos(ang), np.sin(ang)], axis=1)
    ang2 = -2.0 * np.pi * np.outer(idx, idx) / r
    f2 = np.stack([np.cos(ang2), np.sin(ang2)])
    return f1.astype(np.float32), f2.astype(np.float32)


def _idft_table(t2_len):
    r = FFT_R
    n = r * r
    t1 = np.arange(r, dtype=np.float64)[:, None, None]
    t2 = np.arange(t2_len, dtype=np.float64)[None, :, None]
    kb = np.arange(r, dtype=np.float64)[None, None, :]
    ang = 2.0 * np.pi * (t2 * kb / r + t1 * kb / n)
    return (np.stack([np.cos(ang), np.sin(ang)], axis=1) / n).astype(np.float32)


def _dot_hi(table, x):
    return jnp.dot(table, x.astype(BF16), preferred_element_type=F32)


def _stage_block(block, sc):
    rows = block.shape[0]
    for t in range(rows):
        for h in range(sc.shape[0]):
            sc[h, pl.ds(t * FFT_GROUP, FFT_GROUP), :] = block[t, :, h * LANES:(h + 1) * LANES]


def _middle_rows(sc, i):
    rows = sc.shape[1] // FFT_GROUP
    return jnp.concatenate([sc[h, pl.ds(i, rows, stride=FFT_GROUP), :] for h in range(sc.shape[0])], axis=1)


def _stage_scratch(rows, c):
    return pltpu.VMEM((c // LANES, rows * FFT_GROUP, LANES), F32)


def _fft1_kernel(f_ref, z_ref, g_ref, z_sc):
    _stage_block(z_ref.at[0], z_sc)
    for i in range(FFT_GROUP):
        g_ref[0, i] = _dot_hi(f_ref[i], _middle_rows(z_sc, i))


def _fft_stage1(z4, f1):
    b, t2_len, r, c = z4.shape
    g = FFT_GROUP
    return pl.pallas_call(
        _fft1_kernel,
        out_shape=jax.ShapeDtypeStruct((b, r, 2 * r, c), F32),
        grid=(b, r // g),
        in_specs=[pl.BlockSpec((g, 2 * r, t2_len), lambda bi, j: (j, 0, 0)),
                  pl.BlockSpec((1, t2_len, g, c), lambda bi, j: (bi, 0, j, 0))],
        out_specs=pl.BlockSpec((1, g, 2 * r, c), lambda bi, j: (bi, j, 0, 0)),
        scratch_shapes=[_stage_scratch(t2_len, c)],
        compiler_params=_cparams(("parallel", "parallel")),
        name="fft_stage1",
    )(f1, z4)


def _fft2_kernel(with_filter, f_ref, gr_ref, gi_ref, aux_ref, h_ref, gr_sc, gi_sc):
    fr = f_ref[0]
    fi = f_ref[1]
    _stage_block(gr_ref.at[0], gr_sc)
    _stage_block(gi_ref.at[0], gi_sc)
    for i in range(FFT_GROUP):
        gr = _middle_rows(gr_sc, i).astype(BF16)
        gi = _middle_rows(gi_sc, i).astype(BF16)
        xr = _dot_hi(fr, gr) - _dot_hi(fi, gi)
        xi = _dot_hi(fr, gi) + _dot_hi(fi, gr)
        if with_filter:
            tr = aux_ref[0, 0, i]
            ti = aux_ref[0, 1, i]
            yr = xr * tr - xi * ti
            yi = xr * ti + xi * tr
            xr = _dot_hi(fr, yr) + _dot_hi(fi, yi)
            xi = _dot_hi(fr, yi) - _dot_hi(fi, yr)
        else:
            xr = xr * aux_ref[0]
            xi = xi * aux_ref[0]
        h_ref[0, 0, i] = xr
        h_ref[0, 1, i] = xi


def _fft_stage2(g, f2, tf=None, scale=None):
    b, r, _, c = g.shape
    grp = FFT_GROUP
    nblk = r // grp
    in_specs = [pl.BlockSpec((2, r, r), lambda bi, j: (0, 0, 0)),
                pl.BlockSpec((1, r, grp, c), lambda bi, j: (bi, 0, j, 0)),
                pl.BlockSpec((1, r, grp, c), lambda bi, j: (bi, 0, nblk + j, 0))]
    args = [f2, g, g]
    if tf is not None:
        in_specs.append(pl.BlockSpec((1, 2, grp, r, c), lambda bi, j: (0, 0, j, 0, 0)))
        args.append(tf)
    else:
        in_specs.append(pl.BlockSpec((1, 1, c), lambda bi, j: (bi, 0, 0)))
        args.append(scale)
    return pl.pallas_call(
        functools.partial(_fft2_kernel, tf is not None),
        out_shape=jax.ShapeDtypeStruct((b, 2, r, r, c), F32),
        grid=(b, nblk),
        in_specs=in_specs,
        out_specs=pl.BlockSpec((1, 2, grp, r, c), lambda bi, j: (bi, 0, j, 0, 0)),
        scratch_shapes=[_stage_scratch(r, c), _stage_scratch(r, c)],
        compiler_params=_cparams(("parallel", "parallel"), VMEM_LIMIT),
        name="fft_stage2",
    )(*args)


def _ifft_kernel(e_ref, h_ref, z_ref, x_ref, skip_ref, o_ref, hr_sc, hi_sc, z_sc, x_sc, o_sc):
    _stage_block(h_ref.at[0, 0], hr_sc)
    _stage_block(h_ref.at[0, 1], hi_sc)
    _stage_block(z_ref.at[0], z_sc)
    _stage_block(x_ref.at[0], x_sc)
    rows = z_ref.shape[1]
    for i in range(FFT_GROUP):
        y = _dot_hi(e_ref[i, 0], _middle_rows(hr_sc, i)) - _dot_hi(e_ref[i, 1], _middle_rows(hi_sc, i))
        y = y * (1.0 / (FFT_R * FFT_R))
        out = _middle_rows(x_sc, i) * (y + skip_ref[...] * _middle_rows(z_sc, i))
        for h in range(o_sc.shape[0]):
            o_sc[h, pl.ds(i, rows, stride=FFT_GROUP), :] = out[:, h * LANES:(h + 1) * LANES]
    for t in range(rows):
        o_ref[0, t] = jnp.concatenate([o_sc[h, pl.ds(t * FFT_GROUP, FFT_GROUP), :] for h in range(o_sc.shape[0])],
                                      axis=1)


def _ifft_gate(h, e, z4, x4, skip):
    b, t2_len, r, c = z4.shape
    g = FFT_GROUP
    tok = pl.BlockSpec((1, t2_len, g, c), lambda bi, j: (bi, 0, j, 0))
    return pl.pallas_call(
        _ifft_kernel,
        out_shape=jax.ShapeDtypeStruct(z4.shape, F32),
        grid=(b, r // g),
        in_specs=[pl.BlockSpec((g, 2, t2_len, r), lambda bi, j: (j, 0, 0, 0)),
                  pl.BlockSpec((1, 2, r, g, c), lambda bi, j: (bi, 0, 0, j, 0)),
                  tok, tok, pl.BlockSpec((1, c), lambda bi, j: (0, 0))],
        out_specs=tok,
        scratch_shapes=[_stage_scratch(r, c), _stage_scratch(r, c), _stage_scratch(t2_len, c),
                        _stage_scratch(t2_len, c), _stage_scratch(t2_len, c)],
        compiler_params=_cparams(("parallel", "parallel"), VMEM_LIMIT),
        name="ifft_gate",
    )(e, h, z4, x4, skip.reshape(1, c))


TAPS_ROWS = 1024
FEAT_PAD = 128


def _taps_kernel(n, feat_ref, w1_ref, b1_ref, freq_ref, w2_ref, b2_ref, w3_ref, delta_ref, taps_ref, l1_ref):
    c = D_HYENA
    rows = feat_ref.shape[0]
    step = pl.program_id(0)

    @pl.when(step == 0)
    def _():
        l1_ref[...] = jnp.zeros_like(l1_ref)

    def split2(a):
        hi = a.astype(BF16)
        return hi, (a - hi.astype(F32)).astype(BF16)

    def dot3(a, b):
        a_hi, a_lo = split2(a)
        b_hi, b_lo = split2(b)
        mm = lambda x, y: jnp.dot(x, y, preferred_element_type=F32)
        return mm(a_hi, b_hi) + mm(a_lo, b_hi) + mm(a_hi, b_lo)

    f = feat_ref[...]
    hid = jnp.sin(freq_ref[0:1, :] * (dot3(f, w1_ref[...]) + b1_ref[...]))
    hid = jnp.sin(freq_ref[1:2, :] * (dot3(hid, w2_ref[...]) + b2_ref[...]))
    filt = dot3(hid, w3_ref[...])
    window = jnp.exp(-f[:, 0:1] * delta_ref[...]) + HYENA_WINDOW_SHIFT
    i = step * rows + lax.broadcasted_iota(jnp.int32, (rows, c), 0)
    for o in range(HYENA_ORDER):
        fwd = filt[:, o * c:(o + 1) * c]
        bwd = filt[:, (HYENA_ORDER + o) * c:(HYENA_ORDER + o + 1) * c]
        tap = jnp.where(i < n, fwd, jnp.where(i > n, bwd, 0.0)) * window
        taps_ref[o] = tap
        l1_ref[o:o + 1, :] += jnp.sum(jnp.abs(tap), axis=0, keepdims=True)


def _hyena_taps(n, f_w1, f_b1, f_freq, f_w2, f_b2, f_w3):
    i = np.arange(2 * n)
    t = (np.where(i < n, i, 2 * n - i).astype(np.float32) / np.float32(n)).astype(np.float32)
    ang = (np.float32(2.0 * math.pi) * t[:, None]) * np.arange(1, HYENA_BANDS + 1, dtype=np.float32)
    feats = np.zeros((2 * n, FEAT_PAD), np.float32)
    feats[:, 0] = t
    feats[:, 1:1 + HYENA_BANDS] = np.cos(ang.astype(np.float64))
    feats[:, 1 + HYENA_BANDS:1 + 2 * HYENA_BANDS] = np.sin(ang.astype(np.float64))
    log_target = abs(math.log(HYENA_DECAY_TARGET))
    deltas = jnp.linspace(log_target / HYENA_SLOW_DECAY, log_target / HYENA_FAST_DECAY, D_HYENA, dtype=F32)
    hid = f_w2.shape[0]
    w1 = jnp.concatenate([f_w1, jnp.zeros((FEAT_PAD - f_w1.shape[0], hid), F32)], axis=0)
    rows = min(TAPS_ROWS, 2 * n)
    full = lambda shape: pl.BlockSpec(shape, lambda j: (0,) * len(shape))
    return pl.pallas_call(
        functools.partial(_taps_kernel, n),
        out_shape=(jax.ShapeDtypeStruct((HYENA_ORDER, 2 * n, D_HYENA), F32),
                   jax.ShapeDtypeStruct((HYENA_ORDER, D_HYENA), F32)),
        grid=(2 * n // rows,),
        in_specs=[pl.BlockSpec((rows, FEAT_PAD), lambda j: (j, 0)), full((FEAT_PAD, hid)), full((1, hid)),
                  full((2, hid)), full((hid, hid)), full((1, hid)), full(f_w3.shape), full((1, D_HYENA))],
        out_specs=(pl.BlockSpec((HYENA_ORDER, rows, D_HYENA), lambda j: (0, j, 0)), full((HYENA_ORDER, D_HYENA))),
        compiler_params=_cparams(("arbitrary",)),
        name="hyena_taps",
    )(jnp.asarray(feats), w1, f_b1.reshape(1, hid), f_freq, f_w2, f_b2.reshape(1, hid), f_w3, deltas.reshape(1, -1))


def _hyena_latent(v, x1, x2, taps, l1, skip):
    b, n, c = v.shape
    r = FFT_R
    t2_len = n // r
    f1_full, f2 = _dft_tables(r)
    f1_half = jnp.asarray(f1_full[:, :, :t2_len]).astype(BF16)
    f1_full = jnp.asarray(f1_full).astype(BF16)
    f2 = jnp.asarray(f2).astype(BF16)
    e = jnp.asarray(_idft_table(t2_len) * (r * r)).astype(BF16)
    taps4 = taps.reshape(HYENA_ORDER, r, r, c)
    tf = _fft_stage2(_fft_stage1(taps4, f1_full), f2, scale=(1.0 / l1)[:, None, :])
    z = v.reshape(b, t2_len, r, c)
    for o, gate in enumerate((x1, x2)):
        g = _fft_stage1(z, f1_half)
        h = _fft_stage2(g, f2, tf[o:o + 1])
        z = _ifft_gate(h, e, z, gate.reshape(b, t2_len, r, c), skip[o])
    return z.reshape(b, n, c)


def _hyena_ctx_kernel(n, v_ref, x1_ref, x2_ref, k_ref, skip_ref, o_ref, z_sc):
    z_sc[...] = v_ref[0]
    for o, gate_ref in enumerate((x1_ref, x2_ref)):
        def body(s, acc):
            return acc + k_ref[o, pl.ds(n - 1 - s, n), :] * z_sc[pl.ds(s, 1), :]
        conv = lax.fori_loop(0, n, body, jnp.zeros(z_sc.shape, F32))
        z_sc[...] = gate_ref[0] * (conv + skip_ref[o:o + 1, :] * z_sc[...])
    o_ref[0] = z_sc[...]


def _hyena_context(v, x1, x2, taps, l1, skip):
    b, n, c = v.shape
    cb = 128
    k2 = jnp.roll(taps, n - 1, axis=1) / l1[:, None, :]
    tok = pl.BlockSpec((1, n, cb), lambda bi, j: (bi, 0, j))
    return pl.pallas_call(
        functools.partial(_hyena_ctx_kernel, n),
        out_shape=jax.ShapeDtypeStruct((b, n, c), F32),
        grid=(b, c // cb),
        in_specs=[tok, tok, tok,
                  pl.BlockSpec((HYENA_ORDER, 2 * n, cb), lambda bi, j: (0, 0, j)),
                  pl.BlockSpec((HYENA_ORDER, cb), lambda bi, j: (0, j))],
        out_specs=tok,
        scratch_shapes=[pltpu.VMEM((n, cb), F32)],
        compiler_params=_cparams(("parallel", "parallel")),
        name="hyena_ctx",
    )(v, x1, x2, k2, skip)


def _router_kernel(x_ref, mod_ref, rw_ref, rb_ref, h_ref, aff_ref):
    y = _ln_rows(x_ref[0])
    h = y * (1.0 + mod_ref[0, 0, 4:5, :]) + mod_ref[0, 0, 3:4, :]
    for j in range(ROW_SUB):
        h_ref[0, pl.ds(j, TOKEN_TILE, stride=ROW_SUB), :] = h[:, j * LANES:(j + 1) * LANES]
    logits = lax.dot_general(rw_ref[...], h, (((1,), (1,)), ((), ())), precision=HI,
                             preferred_element_type=F32) + rb_ref[...]
    z = jnp.exp(logits - jnp.max(logits, axis=0, keepdims=True))
    aff_ref[0] = z / jnp.sum(z, axis=0, keepdims=True)


def _router(x, mod, router_w, router_b):
    b, na, d = x.shape
    e = router_w.shape[1]
    return pl.pallas_call(
        _router_kernel,
        out_shape=(jax.ShapeDtypeStruct((b, na * ROW_SUB, LANES), F32), jax.ShapeDtypeStruct((b, e, na), F32)),
        grid=(b, na // TOKEN_TILE),
        in_specs=[pl.BlockSpec((1, TOKEN_TILE, d), lambda bi, i: (bi, i, 0)),
                  _mod_spec(),
                  pl.BlockSpec((e, d), lambda bi, i: (0, 0)),
                  pl.BlockSpec((e, 1), lambda bi, i: (0, 0))],
        out_specs=(pl.BlockSpec((1, TOKEN_TILE * ROW_SUB, LANES), lambda bi, i: (bi, i, 0)),
                   pl.BlockSpec((1, e, TOKEN_TILE), lambda bi, i: (bi, 0, i))),
        compiler_params=_cparams(("parallel", "parallel")),
        name="router",
    )(x, mod, router_w.T, router_b.reshape(e, 1))


def _prefix_count(x):
    n = x.shape[1]
    lane = lax.broadcasted_iota(jnp.int32, x.shape, 1)
    sh = 1
    while sh < n:
        x = x + jnp.where(lane >= sh, pltpu.roll(x, sh, axis=1), 0)
        sh *= 2
    return x


SELECT_BISECTIONS = 160


def _select_kernel(segments, aff_ref, sel_ref, pos_ref):
    nb, ne, _ = aff_ref.shape
    parts = [(b, s0, s1, cap) for b in range(nb) for (s0, s1, cap) in segments]

    def body(_, bounds):
        out = []
        for (b, s0, s1, cap), (lo, hi) in zip(parts, bounds):
            mid = 0.5 * (lo + hi)
            cnt = jnp.sum((aff_ref[b, :, s0:s1] >= mid).astype(jnp.int32), axis=1, keepdims=True)
            ok = cnt >= cap
            out.append((jnp.where(ok, mid, lo), jnp.where(ok, hi, mid)))
        return tuple(out)

    init = tuple((jnp.zeros((ne, 1), F32), jnp.full((ne, 1), 2.0, F32)) for _ in parts)
    bounds = lax.fori_loop(0, SELECT_BISECTIONS, body, init)
    for (b, s0, s1, cap), (lo, hi) in zip(parts, bounds):
        a = aff_ref[b, :, s0:s1]
        above = (a >= hi).astype(jnp.int32)
        tied = jnp.where(a >= lo, 1, 0) - above
        need = cap - jnp.sum(above, axis=1, keepdims=True)
        tie_rank = _prefix_count(tied) - tied
        sel = above + tied * (tie_rank < need).astype(jnp.int32)
        sel_ref[b, :, s0:s1] = sel
        pos_ref[b, :, s0:s1] = _prefix_count(sel) - sel


def _select(aff, segments):
    b, e, na = aff.shape
    blk = pl.BlockSpec((b, e, na), lambda i: (0, 0, 0))
    return pl.pallas_call(
        functools.partial(_select_kernel, segments),
        out_shape=(jax.ShapeDtypeStruct((b, e, na), jnp.int32), jax.ShapeDtypeStruct((b, e, na), jnp.int32)),
        grid=(1,),
        in_specs=[blk],
        out_specs=(blk, blk),
        compiler_params=_cparams(("arbitrary",)),
        name="expert_select",
    )(aff)


MOE_ROW_TILE = 264


def _row_tile(r):
    return next(t for t in range(MOE_ROW_TILE, 7, -8) if r % t == 0)


DMA_UNROLL = 8


def _for_each_row(tr, fn):
    def body(i, c):
        fn(i)
        return c
    lax.fori_loop(0, tr, body, 0, unroll=DMA_UNROLL)


def _gather_kernel(tr, idx_ref, h_hbm, o_ref, buf, sem):
    step = pl.program_id(0) * pl.num_programs(1) + pl.program_id(1)
    n_steps = pl.num_programs(0) * pl.num_programs(1)
    cur = step % 2

    def start_tile(t, slot):
        _for_each_row(tr, lambda i: pltpu.make_async_copy(
            h_hbm.at[pl.ds(pl.multiple_of(idx_ref[t * tr + i] * ROW_SUB, ROW_SUB), ROW_SUB)],
            buf.at[slot, pl.ds(i * ROW_SUB, ROW_SUB)], sem.at[slot]).start())

    @pl.when(step == 0)
    def _():
        start_tile(0, 0)

    @pl.when(step + 1 < n_steps)
    def _():
        start_tile(step + 1, 1 - cur)

    pltpu.make_async_copy(h_hbm.at[pl.ds(0, tr * ROW_SUB)], buf.at[cur], sem.at[cur]).wait()
    rows = buf.at[cur]
    for j in range(ROW_SUB):
        o_ref[0, :, j * LANES:(j + 1) * LANES] = rows[pl.ds(j, tr, stride=ROW_SUB), :].astype(BF16)


def _gather_rows(h2, idx):
    e, r = idx.shape
    tr = _row_tile(r)
    return pl.pallas_call(
        functools.partial(_gather_kernel, tr),
        out_shape=jax.ShapeDtypeStruct((e, r, ROW_SUB * LANES), BF16),
        grid_spec=pltpu.PrefetchScalarGridSpec(
            num_scalar_prefetch=1, grid=(e, r // tr),
            in_specs=[pl.BlockSpec(memory_space=pl.ANY)],
            out_specs=pl.BlockSpec((1, tr, ROW_SUB * LANES), lambda ei, j, idx_ref: (ei, j, 0)),
            scratch_shapes=[pltpu.VMEM((2, tr * ROW_SUB, LANES), F32), pltpu.SemaphoreType.DMA((2,))]),
        compiler_params=_cparams(("arbitrary", "arbitrary")),
        name="moe_gather",
    )(idx.reshape(-1), h2)


MOE_TF = 256
MOE_ROW_CHUNKS = 4


COMBINE_CHUNK = 64


def _expert_ffn_kernel(xs_ref, wg_ref, wu_ref, wd_ref, tv_ref, o_ref, acc_ref):
    j = pl.program_id(1)
    wg = wg_ref[0, 0].astype(BF16)
    wu = wu_ref[0, 0].astype(BF16)
    wd = wd_ref[0, 0].astype(BF16)
    rows = xs_ref.shape[1]
    rc = rows // MOE_ROW_CHUNKS

    @pl.when(j == 0)
    def _():
        acc_ref[...] = jnp.zeros_like(acc_ref)

    for ci in range(MOE_ROW_CHUNKS):
        sl = slice(ci * rc, (ci + 1) * rc)
        x = xs_ref[0, sl, :]
        g = jnp.dot(x, wg, preferred_element_type=F32)
        u = jnp.dot(x, wu, preferred_element_type=F32)
        a = (g * jax.nn.sigmoid(g) * u).astype(BF16)
        acc_ref[sl, :] += jnp.dot(a, wd, preferred_element_type=F32)

    @pl.when(j == pl.num_programs(1) - 1)
    def _():
        y = acc_ref[...] * tv_ref[0]
        for t in range(ROW_SUB):
            o_ref[0, pl.ds(t, rows, stride=ROW_SUB), :] = y[:, t * LANES:(t + 1) * LANES]
        o_ref[0, rows * ROW_SUB:, :] = jnp.zeros((o_ref.shape[1] - rows * ROW_SUB, LANES), F32)


def _expert_ffn(xs, w_gate, w_up, w_down, tv, layer):
    e, r, d = xs.shape
    f = w_gate.shape[3]
    tf = MOE_TF
    return pl.pallas_call(
        _expert_ffn_kernel,
        out_shape=jax.ShapeDtypeStruct((e, (r + COMBINE_CHUNK) * ROW_SUB, LANES), F32),
        grid=(e, f // tf),
        in_specs=[pl.BlockSpec((1, r, d), lambda ei, j: (ei, 0, 0)),
                  pl.BlockSpec((1, 1, d, tf), lambda ei, j: (layer, ei, 0, j)),
                  pl.BlockSpec((1, 1, d, tf), lambda ei, j: (layer, ei, 0, j)),
                  pl.BlockSpec((1, 1, tf, d), lambda ei, j: (layer, ei, j, 0)),
                  pl.BlockSpec((1, r, 1), lambda ei, j: (ei, 0, 0))],
        out_specs=pl.BlockSpec((1, (r + COMBINE_CHUNK) * ROW_SUB, LANES), lambda ei, j: (ei, 0, 0)),
        scratch_shapes=[pltpu.VMEM((r, d), F32)],
        compiler_params=_cparams(("parallel", "arbitrary"), VMEM_LIMIT),
        name="expert_ffn",
    )(xs, w_gate, w_up, w_down, tv)


COMBINE_UNROLL = 4
COMBINE_PAD = 8


def _combine_kernel(alpha, n_exp, r, n_tiles, idx_ref, st_ref, y_hbm, x_ref, mod_ref, lw_ref, lb_ref, o_ref,
                    acc, head, tail, head_sem, tail_sem):
    g = pl.program_id(0) * pl.num_programs(1) + pl.program_id(1)
    tile_base = g * TOKEN_TILE
    ch = COMBINE_CHUNK
    cur = g % 2

    def run(e, tile):
        s0 = st_ref[e * (n_tiles + 1) + tile]
        return s0, st_ref[e * (n_tiles + 1) + tile + 1] - s0

    def rows_of(first, count):
        start = first * ROW_SUB
        return pl.ds(start if isinstance(start, int) else pl.multiple_of(start, ROW_SUB), count * ROW_SUB)

    def head_copy(e, tile, slot):
        s0, cnt = run(e, tile)
        return pltpu.make_async_copy(y_hbm.at[e, rows_of(s0, ch)], head.at[slot, e, rows_of(0, ch)],
                                     head_sem.at[slot, e]), cnt

    def start_heads(tile, slot):
        for e in range(n_exp):
            copy, cnt = head_copy(e, tile, slot)
            pl.when(cnt > 0)(copy.start)

    @pl.when(g == 0)
    def _():
        head[...] = jnp.zeros_like(head)
        tail[...] = jnp.zeros_like(tail)
        start_heads(0, 0)

    @pl.when(g + 1 < n_tiles)
    def _():
        start_heads(g + 1, 1 - cur)

    acc[...] = jnp.zeros_like(acc)

    def add_rows(src, first, count):
        def body(it, carry):
            for u in range(COMBINE_UNROLL):
                j = it * COMBINE_UNROLL + u
                t = jnp.where(j < count, idx_ref[first + j] - tile_base, TOKEN_TILE)
                acc[rows_of(t, 1), :] = acc[rows_of(t, 1), :] + src[rows_of(j, 1), :]
            return carry
        lax.fori_loop(0, (count + COMBINE_UNROLL - 1) // COMBINE_UNROLL, body, 0)

    for e in range(n_exp):
        s0, cnt = run(e, g)
        copy, _ = head_copy(e, g, cur)
        pl.when(cnt > 0)(copy.wait)
        add_rows(head.at[cur, e], e * r + s0, jnp.minimum(cnt, ch))
        for c in range(1, TOKEN_TILE // ch):
            @pl.when(cnt > c * ch)
            def _(e=e, c=c, s0=s0, cnt=cnt):
                more = pltpu.make_async_copy(y_hbm.at[e, rows_of(s0 + c * ch, ch)], tail.at[rows_of(0, ch)], tail_sem)
                more.start()
                more.wait()
                add_rows(tail, e * r + s0 + c * ch, jnp.minimum(cnt - c * ch, ch))

    f = jnp.concatenate([acc[pl.ds(j, TOKEN_TILE, stride=ROW_SUB), :] for j in range(ROW_SUB)], axis=1)
    res = alpha * x_ref[0] + mod_ref[0, 0, 5:6, :] * f
    o_ref[0] = _ln_rows(res) * lw_ref[...] + lb_ref[...]


def _combine_residual_ln(y, idx, starts, x, mod, ln_w, ln_b, alpha):
    b, na, d = x.shape
    e, r = idx.shape
    n_tiles = b * na // TOKEN_TILE
    tile = pl.BlockSpec((1, TOKEN_TILE, d), lambda bi, i, *_: (bi, i, 0))
    vec = pl.BlockSpec((1, d), lambda bi, i, *_: (0, 0))
    chunk_rows = (COMBINE_CHUNK + COMBINE_PAD) * ROW_SUB
    return pl.pallas_call(
        functools.partial(_combine_kernel, alpha, e, r, n_tiles),
        out_shape=jax.ShapeDtypeStruct((b, na, d), F32),
        grid_spec=pltpu.PrefetchScalarGridSpec(
            num_scalar_prefetch=2, grid=(b, na // TOKEN_TILE),
            in_specs=[pl.BlockSpec(memory_space=pl.ANY), tile,
                      pl.BlockSpec((1, 1, N_MOD, d), lambda bi, i, *_: (bi, jnp.minimum(i, 1), 0, 0)), vec, vec],
            out_specs=tile,
            scratch_shapes=[pltpu.VMEM(((TOKEN_TILE + COMBINE_PAD) * ROW_SUB, LANES), F32),
                            pltpu.VMEM((2, e, chunk_rows, LANES), F32),
                            pltpu.VMEM((chunk_rows, LANES), F32),
                            pltpu.SemaphoreType.DMA((2, e)), pltpu.SemaphoreType.DMA(())]),
        compiler_params=_cparams(("arbitrary", "arbitrary"), VMEM_LIMIT),
        name="moe_combine_ln",
    )(jnp.pad(idx.reshape(-1), (0, COMBINE_PAD)), starts.reshape(-1), y, x, mod, ln_w.reshape(1, d), ln_b.reshape(1, d))


def _moe_residual_ln(x, mod, router_w, router_b, w_gate, w_up, w_down, layer, ln_w, ln_b, alpha):
    b, na, d = x.shape
    n_lat = na - CTX_LEN
    cap_c = EC_CAPACITY_FACTOR * CTX_LEN // N_EXPERTS
    cap_l = EC_CAPACITY_FACTOR * n_lat // N_EXPERTS
    h, aff = _router(x, mod, router_w, router_b)
    sel, _ = _select(aff, ((0, CTX_LEN, cap_c), (CTX_LEN, na, cap_l)))
    idx_c = jnp.argsort(1 - sel[:, :, :CTX_LEN], axis=-1, stable=True)[..., :cap_c]
    idx_l = jnp.argsort(1 - sel[:, :, CTX_LEN:], axis=-1, stable=True)[..., :cap_l] + CTX_LEN
    idx = jnp.concatenate([idx_c, idx_l], axis=-1).astype(jnp.int32)
    tv = jnp.take_along_axis(aff, idx, axis=-1)
    flat = idx + (jnp.arange(b, dtype=jnp.int32) * na)[:, None, None]
    flat = jnp.transpose(flat, (1, 0, 2)).reshape(N_EXPERTS, -1)
    tv = jnp.transpose(tv, (1, 0, 2)).reshape(N_EXPERTS, -1, 1)
    per_tile = jnp.sum(sel.reshape(b, N_EXPERTS, na // TOKEN_TILE, TOKEN_TILE), axis=-1)
    per_tile = jnp.transpose(per_tile, (1, 0, 2)).reshape(N_EXPERTS, -1)
    starts = jnp.concatenate([jnp.zeros((N_EXPERTS, 1), jnp.int32), jnp.cumsum(per_tile, axis=1, dtype=jnp.int32)], axis=1)
    xs = _gather_rows(h.reshape(b * na * ROW_SUB, LANES), flat)
    y = _expert_ffn(xs, w_gate, w_up, w_down, tv, layer)
    return _combine_residual_ln(y, flat, starts, x, mod, ln_w, ln_b, alpha)


def _dwconv_kernel(silu, x_ref, w_ref, b_ref, o_ref):
    x = x_ref[0]
    na = x.shape[0]
    row = lax.broadcasted_iota(jnp.int32, x.shape, 0)
    first = (row == 0) | (row == CTX_LEN)
    last = (row == CTX_LEN - 1) | (row == na - 1)
    prev = jnp.where(first, 0.0, pltpu.roll(x, 1, axis=0))
    nxt = jnp.where(last, 0.0, pltpu.roll(x, na - 1, axis=0))
    y = w_ref[0:1, :] * prev + w_ref[1:2, :] * x + w_ref[2:3, :] * nxt + b_ref[...]
    if silu:
        y = y * jax.nn.sigmoid(y)
    o_ref[0, 0] = y


def _segment_dwconv(p, w, bias, groups, silu):
    b, na, c = p.shape
    per = c // groups // LANES
    return pl.pallas_call(
        functools.partial(_dwconv_kernel, silu),
        out_shape=jax.ShapeDtypeStruct((groups, b, na, c // groups), F32),
        grid=(b, c // LANES),
        in_specs=[pl.BlockSpec((1, na, LANES), lambda bi, j: (bi, 0, j)),
                  pl.BlockSpec((SHORT_CONV, LANES), lambda bi, j: (0, j)),
                  pl.BlockSpec((1, LANES), lambda bi, j: (0, j))],
        out_specs=pl.BlockSpec((1, 1, na, LANES), lambda bi, j: (j // per, bi, 0, j % per)),
        compiler_params=_cparams(("parallel", "parallel"), VMEM_LIMIT),
        name="short_conv",
    )(p, w, bias.reshape(1, c))


def _rope_tables(n_lat):
    rows = n_lat // GRID_W
    row = jnp.repeat(jnp.arange(rows, dtype=F32), GRID_W)
    col = (jnp.arange(n_lat) % GRID_W).astype(F32)
    nf = ATTN_HEAD_DIM // 4
    inv = ROPE_THETA ** (-jnp.arange(nf, dtype=F32) / nf)
    ar = row[:, None] * inv
    ac = col[:, None] * inv
    ang = jnp.concatenate([ar, ar, ac, ac], axis=-1)
    return jnp.cos(ang), jnp.sin(ang)


def _mixing(x, mod, w_in, mlstm_conv_w, mlstm_conv_b, mlstm_gate_b, mlstm_norm_w, attn_q_norm_w,
            attn_k_norm_w, hyena_conv_w, hyena_conv_b, hyena_filter, hyena_skip, with_ctx_out):
    b, na, _ = x.shape
    n_lat = na - CTX_LEN
    mqk, mv, mo, aq, ak, av, hy, mg = _in_projection(x, mod, w_in)
    mg = mg[..., :MLSTM_GATES]

    qk = _segment_dwconv(mqk, _pad_heads(mlstm_conv_w), _pad_heads(mlstm_conv_b), groups=1,
                         silu=True)[0]
    g = mg.reshape(b, na, N_DIR, 2, MLSTM_HEADS) + mlstm_gate_b
    g = jnp.stack([g[:, :, :, 0], jax.nn.log_sigmoid(g[:, :, :, 1])], axis=3)
    gcols = jnp.transpose(g, (2, 0, 1, 3, 4)).reshape(N_DIR, b, na, 2 * MLSTM_HEADS)
    grows = jnp.transpose(gcols, (0, 1, 3, 2))
    hm = _mlstm_scan(qk, mv, grows, gcols)

    q, k, v = _qkv_prep(aq, ak, av, attn_q_norm_w, attn_k_norm_w)
    ya = _attention(q, k, v)

    hv, hx1, hx2 = _segment_dwconv(hy, hyena_conv_w, hyena_conv_b, groups=3, silu=False)
    yh = _hyena_latent(hv[:, CTX_LEN:], hx1[:, CTX_LEN:], hx2[:, CTX_LEN:],
                       *_hyena_taps(n_lat, *hyena_filter), hyena_skip)
    if with_ctx_out:
        yh_c = _hyena_context(hv[:, :CTX_LEN], hx1[:, :CTX_LEN], hx2[:, :CTX_LEN],
                              *_hyena_taps(CTX_LEN, *hyena_filter), hyena_skip)
    else:
        yh_c = jnp.zeros((b, CTX_LEN, D_HYENA), F32)
    yh = jnp.concatenate([yh_c, yh], axis=1)
    return hm, mo, ya, yh


def kernel(x, c, ctx, c_ctx, w_mod, b_mod, w_in, mlstm_conv_w, mlstm_conv_b, mlstm_gate_b, mlstm_norm_w, attn_q_norm_w, attn_k_norm_w, hyena_conv_w, hyena_conv_b, hyena_f_w1, hyena_f_b1, hyena_f_freq, hyena_f_w2, hyena_f_b2, hyena_f_w3, hyena_skip, w_out, ln_mix_w, ln_mix_b, router_w, router_b, exp_w_gate, exp_w_up, exp_w_down, ln_ffn_w, ln_ffn_b):
    bsz, seq, d = x.shape
    assert d == D_MODEL and ctx.shape[1] == CTX_LEN == TOKEN_TILE and seq == FFT_R * FFT_R // 2
    alpha = (2.0 * DEPTH) ** 0.25
    xa = jnp.concatenate([ctx, x], axis=1)
    crows = jnp.concatenate([c, jnp.broadcast_to(c_ctx, (8 - bsz, d))], axis=0)
    for l in range(DEPTH):
        last = l == DEPTH - 1
        m = _modulation(crows, w_mod, b_mod[l], l).reshape(8, N_MOD, d)
        mod = jnp.stack([jnp.broadcast_to(m[bsz], (bsz, N_MOD, d)), m[:bsz]], axis=1)
        hm, mo, ya, yh = _mixing(
            xa, mod, w_in[l], mlstm_conv_w[l], mlstm_conv_b[l], mlstm_gate_b[l], mlstm_norm_w[l],
            attn_q_norm_w[l], attn_k_norm_w[l], hyena_conv_w[l], hyena_conv_b[l],
            (hyena_f_w1[l], hyena_f_b1[l], hyena_f_freq[l], hyena_f_w2[l], hyena_f_b2[l], hyena_f_w3[l]),
            hyena_skip[l], not last)
        xa = _out_projection(hm, mo, mlstm_norm_w[l], ya, yh, xa, mod, w_out[l].astype(BF16),
                             ln_mix_w[l], ln_mix_b[l], alpha)
        xa = _moe_residual_ln(xa, mod, router_w[l], router_b[l], exp_w_gate, exp_w_up, exp_w_down, l,
                              ln_ffn_w[l], ln_ffn_b[l], alpha)
    return xa[:, CTX_LEN:]
```
